```python
import math
import jax
import jax.numpy as jnp
from jax import lax
import numpy as np

D_MODEL = 1024
BATCH = 2
SEQ = 8192
DEPTH = 2

GRID_W = 64
CTX_LEN = 256
EPS = 1e-6

CONV_CH = 256
CONV_WIDTH = 31
GDN_HEADS = 4
GDN_DK = 64
GDN_DV = 64
GDN_WIDTH = GDN_HEADS * GDN_DV
GDN_QKV = 2 * GDN_HEADS * GDN_DK + GDN_WIDTH
SHORT_CONV = 5
GDN_CHUNK = 64
HEAD_DIM = 64
ATTN_Q_HEADS = 8
ATTN_KV_HEADS = 2
ATTN_GROUP = ATTN_Q_HEADS // ATTN_KV_HEADS
ATTN_WIDTH = ATTN_Q_HEADS * HEAD_DIM
Q_BLOCK = 128
ROPE_THETA = 10000.0
MIX_WIDTH = CONV_CH + GDN_WIDTH + ATTN_WIDTH
N_GROUPS = 4
EXPERTS_PER_GROUP = 8
N_EXPERTS = N_GROUPS * EXPERTS_PER_GROUP
TOP_K = 2
D_EXPERT = 512
MOE_BLOCK = 128

IN_SIZES = (GDN_QKV, 2 * GDN_HEADS, 2 * GDN_HEADS,
            ATTN_KV_HEADS * HEAD_DIM, ATTN_KV_HEADS * HEAD_DIM,
            CONV_CH, CONV_CH, GDN_WIDTH, ATTN_WIDTH)
CTX_STATE_COLS = GDN_QKV + 4 * GDN_HEADS + 2 * ATTN_KV_HEADS * HEAD_DIM
D_IN = CTX_STATE_COLS + 2 * CONV_CH + GDN_WIDTH + ATTN_WIDTH

kernel_name = "hybrid_conv_deltanet_gqa_hmoe_dit"


def split_cols(t, sizes):
    out, off = [], 0
    for s in sizes:
        out.append(t[..., off:off + s])
        off += s
    return out


def rms_norm(x, g):
    xf = x.astype(jnp.float32)
    y = xf * lax.rsqrt(jnp.mean(xf * xf, axis=-1, keepdims=True) + EPS)
    return (y * g.astype(jnp.float32)).astype(x.dtype)


def layer_norm(x, g, b):
    xf = x.astype(jnp.float32)
    mu = jnp.mean(xf, axis=-1, keepdims=True)
    xc = xf - mu
    y = xc * lax.rsqrt(jnp.mean(xc * xc, axis=-1, keepdims=True) + EPS)
    return (y * g.astype(jnp.float32) + b.astype(jnp.float32)).astype(x.dtype)


def l2_norm(x):
    return x * lax.rsqrt(jnp.sum(x * x, axis=-1, keepdims=True) + EPS)


def depthwise_conv(x, w):
    pad = w.shape[0] // 2
    return lax.conv_general_dilated(
        x, w[:, None, :].astype(x.dtype), window_strides=(1,), padding=[(pad, pad)],
        dimension_numbers=("NWC", "WIO", "NWC"), feature_group_count=x.shape[-1])


def axial_rope_tables(n_tokens):
    rows = n_tokens // GRID_W
    row = jnp.repeat(jnp.arange(rows), GRID_W).astype(jnp.float32)
    col = (jnp.arange(rows * GRID_W) % GRID_W).astype(jnp.float32)
    n_freq = HEAD_DIM // 4
    inv = ROPE_THETA ** (-jnp.arange(n_freq, dtype=jnp.float32) / n_freq)
    ang = jnp.concatenate([row[:, None] * inv, col[:, None] * inv], axis=-1)
    return jnp.cos(ang), jnp.sin(ang)


def apply_axial_rope(x, cos, sin):
    L = x.shape[1]
    xr = x.astype(jnp.float32).reshape(*x.shape[:-1], 2, 2, HEAD_DIM // 4)
    x1, x2 = xr[..., 0, :], xr[..., 1, :]
    c = cos.reshape(L, 2, HEAD_DIM // 4)[None, :, None]
    s = sin.reshape(L, 2, HEAD_DIM // 4)[None, :, None]
    out = jnp.stack([x1 * c - x2 * s, x2 * c + x1 * s], axis=-2)
    return out.reshape(x.shape).astype(x.dtype)


def conformer_conv(val, gate, dw_w, dw_b, ln_g, ln_b, pw_w, pw_b):
    h = val * jax.nn.sigmoid(gate)
    h = depthwise_conv(h, dw_w) + dw_b
    h = jax.nn.silu(layer_norm(h, ln_g, ln_b))
    return h @ pw_w + pw_b


def gdn_features(qkv, a_in, b_in, conv_w, a_log, dt_bias):
    B, L, _ = qkv.shape
    qkv = jax.nn.silu(depthwise_conv(qkv, conv_w)).astype(jnp.float32)
    q, k, v = split_cols(qkv, (GDN_HEADS * GDN_DK, GDN_HEADS * GDN_DK, GDN_WIDTH))
    q = l2_norm(q.reshape(B, L, GDN_HEADS, GDN_DK)) * (GDN_DK ** -0.5)
    k = l2_norm(k.reshape(B, L, GDN_HEADS, GDN_DK))
    v = v.reshape(B, L, GDN_HEADS, GDN_DV)
    a_in = a_in.astype(jnp.float32).reshape(B, L, 2, GDN_HEADS)
    b_in = b_in.astype(jnp.float32).reshape(B, L, 2, GDN_HEADS)
    g = -jnp.exp(a_log.astype(jnp.float32)) * jax.nn.softplus(a_in + dt_bias.astype(jnp.float32))
    beta = jax.nn.sigmoid(b_in)
    return q, k, v, g, beta


def gdn_direction(q, k, v, g, beta, s0, with_out):
    B, L, H, _ = q.shape
    dv = v.shape[-1]
    n = L // GDN_CHUNK

    def chunks(t):
        return jnp.moveaxis(t.reshape(B, n, GDN_CHUNK, H, *t.shape[3:]), 3, 1)

    q, k, v, g, beta = (chunks(t) for t in (q, k, v, g, beta))
    gam = jnp.cumsum(g, axis=-1)
    idx = jnp.arange(GDN_CHUNK)
    incl = idx[:, None] >= idx[None, :]
    strict = idx[:, None] > idx[None, :]
    decay = jnp.exp(jnp.where(incl, gam[..., :, None] - gam[..., None, :], -jnp.inf))
    kb = k * beta[..., None]
    a = jnp.where(strict, jnp.einsum("bhncd,bhnsd->bhncs", kb, k) * decay, 0.0)
    rhs = jnp.concatenate([v * beta[..., None], kb * jnp.exp(gam)[..., None]], axis=-1)
    sol = lax.linalg.triangular_solve(a, rhs, left_side=True, lower=True, unit_diagonal=True)
    u, w = sol[..., :dv], sol[..., dv:]
    last = gam[..., -1]
    kd = k * jnp.exp(last[..., None] - gam)[..., None]
    xs = [u, w, kd, jnp.exp(last)]
    if with_out:
        p = jnp.einsum("bhncd,bhnsd->bhncs", q, k) * decay
        xs += [p, q * jnp.exp(gam)[..., None]]
    xs = tuple(jnp.moveaxis(t, 2, 0) for t in xs)

    def step(s, inp):
        u_n, w_n, kd_n, gl_n = inp[:4]
        v_new = u_n - jnp.einsum("bhck,bhkv->bhcv", w_n, s)
        s_new = s * gl_n[..., None, None] + jnp.einsum("bhck,bhcv->bhkv", kd_n, v_new)
        if with_out:
            p_n, qg_n = inp[4:]
            o = jnp.einsum("bhck,bhkv->bhcv", qg_n, s) + jnp.einsum("bhcs,bhsv->bhcv", p_n, v_new)
            return s_new, o
        return s_new, None

    s_fin, o = lax.scan(step, s0, xs)
    if not with_out:
        return None, s_fin
    o = jnp.transpose(o, (1, 0, 3, 2, 4)).reshape(B, L, H, dv)
    return o, s_fin


def gdn_bidirectional(feats_c, feats_l, with_ctx_out):
    qc, kc, vc, gc, bc = feats_c
    ql, kl, vl, gl, bl = feats_l
    B = ql.shape[0]
    s0 = jnp.zeros((B, GDN_HEADS, GDN_DK, GDN_DV), jnp.float32)
    o_lat, o_ctx = 0.0, 0.0
    for d in range(2):
        flip = (lambda t: jnp.flip(t, axis=1)) if d == 1 else (lambda t: t)
        oc, sc = gdn_direction(flip(qc), flip(kc), flip(vc), flip(gc[:, :, d]), flip(bc[:, :, d]), s0, with_ctx_out)
        ol, _ = gdn_direction(flip(ql), flip(kl), flip(vl), flip(gl[:, :, d]), flip(bl[:, :, d]), sc, True)
        o_lat = o_lat + flip(ol)
        if with_ctx_out:
            o_ctx = o_ctx + flip(oc)
    return o_lat, (o_ctx if with_ctx_out else None)


def gdn_output(o, z, norm_g, dtype):
    B, L = o.shape[:2]
    zf = z.astype(jnp.float32).reshape(B, L, GDN_HEADS, GDN_DV)
    y = rms_norm(o, norm_g) * jax.nn.silu(zf)
    return y.reshape(B, L, GDN_WIDTH).astype(dtype)


def attend(qb, keys, vals):
    s = jnp.einsum("bqkgd,bskd->bkgqs", qb, keys, preferred_element_type=jnp.float32) * (HEAD_DIM ** -0.5)
    p = jax.nn.softmax(s, axis=-1)
    return jnp.einsum("bkgqs,bskd->bqkgd", p.astype(vals.dtype), vals)


def gqa_attention(q_l, k_l, v_l, q_c, k_c, v_c, q_norm_g, k_norm_g, cos, sin):
    B, L, _ = q_l.shape
    Lc = k_c.shape[1]

    def heads(t, h):
        return t.reshape(t.shape[0], t.shape[1], h, HEAD_DIM)

    ql = apply_axial_rope(rms_norm(heads(q_l, ATTN_Q_HEADS), q_norm_g), cos, sin)
    kl = apply_axial_rope(rms_norm(heads(k_l, ATTN_KV_HEADS), k_norm_g), cos, sin)
    kc = rms_norm(heads(k_c, ATTN_KV_HEADS), k_norm_g)
    vc = heads(v_c, ATTN_KV_HEADS)
    keys = jnp.concatenate([kc, kl], axis=1)
    vals = jnp.concatenate([vc, heads(v_l, ATTN_KV_HEADS)], axis=1)
    nb = L // Q_BLOCK
    qb = ql.reshape(B, nb, Q_BLOCK, ATTN_KV_HEADS, ATTN_GROUP, HEAD_DIM).transpose(1, 0, 2, 3, 4, 5)
    ob = lax.map(lambda t: attend(t, keys, vals), qb)
    o_l = ob.transpose(1, 0, 2, 3, 4, 5).reshape(B, L, ATTN_WIDTH)
    if q_c is None:
        return o_l, None
    qc = rms_norm(heads(q_c, ATTN_Q_HEADS), q_norm_g).reshape(B, Lc, ATTN_KV_HEADS, ATTN_GROUP, HEAD_DIM)
    o_c = attend(qc, kc, vc).reshape(B, Lc, ATTN_WIDTH)
    return o_l, o_c


def hierarchical_moe(x, wr_g, br_g, wr_e, br_e, w_gate, w_up, w_down):
    T, D = x.shape
    xf = x.astype(jnp.float32)
    pg = jax.nn.softmax(xf @ wr_g.astype(jnp.float32) + br_g.astype(jnp.float32), axis=-1)
    grp = jnp.argmax(pg, axis=-1)
    pg_sel = jnp.take_along_axis(pg, grp[:, None], axis=-1)
    le = (xf @ wr_e.astype(jnp.float32) + br_e.astype(jnp.float32)).reshape(T, N_GROUPS, EXPERTS_PER_GROUP)
    le = jnp.take_along_axis(le, grp[:, None, None], axis=1)[:, 0]
    top_p, top_i = lax.top_k(jax.nn.softmax(le, axis=-1), TOP_K)
    wts = pg_sel * top_p / jnp.sum(top_p, axis=-1, keepdims=True)
    eid = grp[:, None] * EXPERTS_PER_GROUP + top_i
    flat_e = eid.reshape(-1)
    flat_t = jnp.repeat(jnp.arange(T, dtype=jnp.int32), TOP_K)
    flat_w = wts.reshape(-1)
    order = jnp.argsort(flat_e)
    se, st, sw = flat_e[order], flat_t[order], flat_w[order]
    counts = jnp.bincount(flat_e, length=N_EXPERTS)
    starts = jnp.cumsum(counts) - counts
    pcounts = (counts + MOE_BLOCK - 1) // MOE_BLOCK * MOE_BLOCK
    pends = jnp.cumsum(pcounts)
    pstarts = pends - pcounts
    slot = pstarts[se] + jnp.arange(T * TOP_K) - starts[se]
    n_slots = -(-(T * TOP_K) // MOE_BLOCK) * MOE_BLOCK + N_EXPERTS * MOE_BLOCK
    slot_tok = jnp.full((n_slots,), T, jnp.int32).at[slot].set(st)
    slot_w = jnp.zeros((n_slots,), x.dtype).at[slot].set(sw.astype(x.dtype))
    nblk = n_slots // MOE_BLOCK
    blk_e = jnp.minimum(jnp.searchsorted(pends, jnp.arange(nblk) * MOE_BLOCK, side="right"), N_EXPERTS - 1)
    xpad = jnp.concatenate([x, jnp.zeros((1, D), x.dtype)], axis=0)
    xb = xpad[slot_tok].reshape(nblk, MOE_BLOCK, D)

    def expert_block(args):
        xe, e = args
        h = jax.nn.silu(xe @ w_gate[e]) * (xe @ w_up[e])
        return h @ w_down[e]

    yb = lax.map(expert_block, (xb, blk_e)).reshape(n_slots, D)
    out = jnp.zeros_like(xpad).at[slot_tok].add(yb * slot_w[:, None])
    return out[:T]


def trunk_layer(x, ctx, c, c_ctx, cos, sin, with_ctx_out,
                mod_w, mod_b, norm1_g, norm2_g, w_in,
                conv_dw_w, conv_dw_b, conv_ln_g, conv_ln_b, conv_pw_w, conv_pw_b,
                gdn_conv_w, gdn_a_log, gdn_dt_bias, gdn_norm_g,
                attn_q_norm_g, attn_k_norm_g, w_out,
                router_group_w, router_group_b, router_expert_w, router_expert_b,
                expert_w_gate, expert_w_up, expert_w_down):
    B, L, D = x.shape
    Lc = ctx.shape[1]
    mod_l = (jax.nn.silu(c) @ mod_w + mod_b)[:, None, :]
    mod_c = (jax.nn.silu(c_ctx) @ mod_w + mod_b)[None, None, :]
    sh1_l, sc1_l, g1_l, sh2_l, sc2_l, g2_l = split_cols(mod_l, (D,) * 6)
    sh1_c, sc1_c, g1_c, sh2_c, sc2_c, g2_c = split_cols(mod_c, (D,) * 6)

    h_l = rms_norm(x, norm1_g) * (1 + sc1_l) + sh1_l
    h_c = rms_norm(ctx, norm1_g) * (1 + sc1_c) + sh1_c
    qkv_l, a_l, b_l, k_l, v_l, gv_l, gg_l, z_l, q_l = split_cols(h_l @ w_in, IN_SIZES)
    if with_ctx_out:
        qkv_c, a_c, b_c, k_c, v_c, gv_c, gg_c, z_c, q_c = split_cols(h_c @ w_in, IN_SIZES)
    else:
        qkv_c, a_c, b_c, k_c, v_c = split_cols(h_c @ w_in[:, :CTX_STATE_COLS], IN_SIZES[:5])
        q_c = None

    conv_args = (conv_dw_w, conv_dw_b, conv_ln_g, conv_ln_b, conv_pw_w, conv_pw_b)
    conv_out_l = conformer_conv(gv_l, gg_l, *conv_args)
    feats_c = gdn_features(qkv_c, a_c, b_c, gdn_conv_w, gdn_a_log, gdn_dt_bias)
    feats_l = gdn_features(qkv_l, a_l, b_l, gdn_conv_w, gdn_a_log, gdn_dt_bias)
    o_gdn_l, o_gdn_c = gdn_bidirectional(feats_c, feats_l, with_ctx_out)
    gdn_out_l = gdn_output(o_gdn_l, z_l, gdn_norm_g, x.dtype)
    attn_l, attn_c = gqa_attention(q_l, k_l, v_l, q_c, k_c, v_c, attn_q_norm_g, attn_k_norm_g, cos, sin)
    x = x + g1_l * (jnp.concatenate([conv_out_l, gdn_out_l, attn_l], axis=-1) @ w_out)
    if with_ctx_out:
        conv_out_c = conformer_conv(gv_c, gg_c, *conv_args)
        gdn_out_c = gdn_output(o_gdn_c, z_c, gdn_norm_g, ctx.dtype)
        ctx = ctx + g1_c * (jnp.concatenate([conv_out_c, gdn_out_c, attn_c], axis=-1) @ w_out)

    moe_args = (router_group_w, router_group_b, router_expert_w, router_expert_b,
                expert_w_gate, expert_w_up, expert_w_down)
    h2_l = (rms_norm(x, norm2_g) * (1 + sc2_l) + sh2_l).reshape(B * L, D)
    if with_ctx_out:
        h2_c = (rms_norm(ctx, norm2_g) * (1 + sc2_c) + sh2_c).reshape(B * Lc, D)
        y = hierarchical_moe(jnp.concatenate([h2_l, h2_c], axis=0), *moe_args)
        x = x + g2_l * y[:B * L].reshape(B, L, D)
        ctx = ctx + g2_c * y[B * L:].reshape(B, Lc, D)
        return x, ctx
    y = hierarchical_moe(h2_l, *moe_args)
    return x + g2_l * y.reshape(B, L, D), None


def setup_inputs(seed: int = 0) -> dict:
    key = jax.random.key(seed)
    ks = jax.random.split(key, 32)
    f32 = jnp.float32
    D = D_MODEL

    def nrm(k, shape, scale):
        return jax.random.normal(k, shape, f32) * scale

    def gain(k, shape):
        return 1.0 + 0.1 * jax.random.normal(k, shape, f32)

    dt = jnp.exp(jax.random.uniform(ks[17], (DEPTH, 2, GDN_HEADS), f32, math.log(1e-3), math.log(1e-1)))
    return {
        "x": nrm(ks[0], (BATCH, SEQ, D), 1.0),
        "c": nrm(ks[1], (BATCH, D), 1.0),
        "ctx": nrm(ks[2], (BATCH, CTX_LEN, D), 1.0),
        "c_ctx": nrm(ks[3], (D,), 1.0),
        "mod_w": nrm(ks[4], (DEPTH, D, 6 * D), 0.5 * D ** -0.5),
        "mod_b": nrm(ks[5], (DEPTH, 6 * D), 0.02),
        "norm1_g": gain(ks[6], (DEPTH, D)),
        "norm2_g": gain(ks[7], (DEPTH, D)),
        "w_in": nrm(ks[8], (DEPTH, D, D_IN), D ** -0.5),
        "conv_dw_w": nrm(ks[9], (DEPTH, CONV_WIDTH, CONV_CH), CONV_WIDTH ** -0.5),
        "conv_dw_b": nrm(ks[10], (DEPTH, CONV_CH), 0.02),
        "conv_ln_g": gain(ks[11], (DEPTH, CONV_CH)),
        "conv_ln_b": nrm(ks[12], (DEPTH, CONV_CH), 0.02),
        "conv_pw_w": nrm(ks[13], (DEPTH, CONV_CH, CONV_CH), CONV_CH ** -0.5),
        "conv_pw_b": nrm(ks[14], (DEPTH, CONV_CH), 0.02),
        "gdn_conv_w": nrm(ks[15], (DEPTH, SHORT_CONV, GDN_QKV), SHORT_CONV ** -0.5),
        "gdn_a_log": jnp.log(jax.random.uniform(ks[16], (DEPTH, 2, GDN_HEADS), f32, 1.0, 16.0)),
        "gdn_dt_bias": dt + jnp.log(-jnp.expm1(-dt)),
        "gdn_norm_g": gain(ks[18], (DEPTH, GDN_DV)),
        "attn_q_norm_g": gain(ks[19], (DEPTH, HEAD_DIM)),
        "attn_k_norm_g": gain(ks[20], (DEPTH, HEAD_DIM)),
        "w_out": nrm(ks[21], (DEPTH, MIX_WIDTH, D), MIX_WIDTH ** -0.5),
        "router_group_w": nrm(ks[22], (DEPTH, D, N_GROUPS), D ** -0.5),
        "router_group_b": nrm(ks[23], (DEPTH, N_GROUPS), 0.01),
        "router_expert_w": nrm(ks[24], (DEPTH, D, N_EXPERTS), D ** -0.5),
        "router_expert_b": nrm(ks[25], (DEPTH, N_EXPERTS), 0.01),
        "expert_w_gate": nrm(ks[26], (DEPTH, N_EXPERTS, D, D_EXPERT), D ** -0.5),
        "expert_w_up": nrm(ks[27], (DEPTH, N_EXPERTS, D, D_EXPERT), D ** -0.5),
        "expert_w_down": nrm(ks[28], (DEPTH, N_EXPERTS, D_EXPERT, D), D_EXPERT ** -0.5),
        "final_norm_g": gain(ks[29], (D,)),
    }


def reference(x, c, ctx, c_ctx, mod_w, mod_b, norm1_g, norm2_g, w_in,
              conv_dw_w, conv_dw_b, conv_ln_g, conv_ln_b, conv_pw_w, conv_pw_b,
              gdn_conv_w, gdn_a_log, gdn_dt_bias, gdn_norm_g,
              attn_q_norm_g, attn_k_norm_g, w_out,
              router_group_w, router_group_b, router_expert_w, router_expert_b,
              expert_w_gate, expert_w_up, expert_w_down, final_norm_g):
    cos, sin = axial_rope_tables(x.shape[1])
    for layer in range(DEPTH):
        x, ctx = trunk_layer(
            x, ctx, c, c_ctx, cos, sin, layer < DEPTH - 1,
            mod_w[layer], mod_b[layer], norm1_g[layer], norm2_g[layer], w_in[layer],
            conv_dw_w[layer], conv_dw_b[layer], conv_ln_g[layer], conv_ln_b[layer],
            conv_pw_w[layer], conv_pw_b[layer],
            gdn_conv_w[layer], gdn_a_log[layer], gdn_dt_bias[layer], gdn_norm_g[layer],
            attn_q_norm_g[layer], attn_k_norm_g[layer], w_out[layer],
            router_group_w[layer], router_group_b[layer], router_expert_w[layer], router_expert_b[layer],
            expert_w_gate[layer], expert_w_up[layer], expert_w_down[layer])
    return rms_norm(x, final_norm_g)
```

```python
import functools
import math

import jax
import jax.numpy as jnp
from jax import lax
from jax.experimental import pallas as pl
from jax.experimental.pallas import tpu as pltpu

F32 = jnp.float32
BF16 = jnp.bfloat16
HIGHEST = lax.Precision.HIGHEST

EPS = 1e-6
GRID_W = 64
CONV_CH = 256
CONV_WIDTH = 31
GDN_HEADS = 4
GDN_DK = 64
GDN_WIDTH = 256
GDN_QKV = 768
SHORT_CONV = 5
HEAD_DIM = 64
ATTN_Q_HEADS = 8
ATTN_KV_HEADS = 2
ATTN_GROUP = 4
ATTN_WIDTH = 512
ROPE_THETA = 10000.0
N_GROUPS = 4
EXPERTS_PER_GROUP = 8
N_EXPERTS = 32
TOP_K = 2
D_EXPERT = 512

LANES = 128
TM = 256
GDN_C = 128
CONV_HALO = 16
SHORT_HALO = 8
ATT_TQ = 256
ATT_TK = 256
MOE_BM = 256
VMEM_LIMIT = 56 * 1024 * 1024

IN_A0 = GDN_QKV
IN_PAD = LANES - 16
C_QKV = (0, 768)
C_AB = (768, 896)
C_KV = (896, 1152)
C_GVGG = (1152, 1664)
C_Z = (1664, 1920)
C_Q = (1920, 2432)
IN_COLS = 2432


def _cparams(sem):
    return pltpu.CompilerParams(dimension_semantics=sem, vmem_limit_bytes=VMEM_LIMIT)


def _silu(x):
    return x * jax.nn.sigmoid(x)


def _dot(a, b, **kw):
    return jnp.dot(a, b, preferred_element_type=F32, **kw)


def _dot_nt(a, b):
    return lax.dot_general(a, b, (((1,), (1,)), ((), ())), preferred_element_type=F32)


def _dot_tn(a, b):
    return lax.dot_general(a, b, (((0,), (0,)), ((), ())), preferred_element_type=F32)


def _mod_kernel(c_ref, w_ref, b_ref, o_ref):
    c = c_ref[...]
    o_ref[0] = _dot(_silu(c), w_ref[0], precision=HIGHEST) + b_ref[0]


def _modulation(cvec, mod_w, mod_b):
    depth, d, n = mod_w.shape
    nr = cvec.shape[0]
    tn = 768
    return pl.pallas_call(
        _mod_kernel,
        grid=(depth, n // tn),
        in_specs=[pl.BlockSpec((nr, d), lambda l, j: (0, 0)),
                  pl.BlockSpec((1, d, tn), lambda l, j: (l, 0, j)),
                  pl.BlockSpec((1, 1, tn), lambda l, j: (l, 0, j))],
        out_specs=pl.BlockSpec((1, nr, tn), lambda l, j: (l, 0, j)),
        out_shape=jax.ShapeDtypeStruct((depth, nr, n), F32),
        compiler_params=_cparams(("parallel", "parallel")),
        name="modulation",
    )(cvec, mod_w, mod_b.reshape(depth, 1, n))


def _rms(x, g):
    return x * lax.rsqrt(jnp.mean(x * x, axis=-1, keepdims=True) + EPS) * g


def _inproj_kernel(x_ref, mod_ref, g_ref, w_ref, qkv_ref, ab_ref, kv_ref, gvgg_ref, z_ref, q_ref):
    x = x_ref[0]
    sh = mod_ref[0, 0:1, :]
    sc = mod_ref[0, 1:2, :]
    h = (_rms(x, g_ref[...]) * (1.0 + sc) + sh).astype(BF16)
    for ref, (c0, c1) in ((qkv_ref, C_QKV), (ab_ref, C_AB), (kv_ref, C_KV),
                          (gvgg_ref, C_GVGG), (z_ref, C_Z), (q_ref, C_Q)):
        ref[0] = _dot(h, w_ref[:, c0:c1])


def _in_proj(x, mod, g1, w_in_p, nct):
    nb, s, d = x.shape
    nt = s // TM
    widths = [c1 - c0 for c0, c1 in (C_QKV, C_AB, C_KV, C_GVGG, C_Z, C_Q)]
    tok = lambda w: pl.BlockSpec((1, TM, w), lambda b, t: (b, t, 0))
    return pl.pallas_call(
        _inproj_kernel,
        grid=(nb, nt),
        in_specs=[tok(d),
                  pl.BlockSpec((1, 6, d), lambda b, t: (jnp.where(t < nct, nb, b), 0, 0)),
                  pl.BlockSpec((1, d), lambda b, t: (0, 0)),
                  pl.BlockSpec((d, IN_COLS), lambda b, t: (0, 0))],
        out_specs=[tok(w) for w in widths],
        out_shape=[jax.ShapeDtypeStruct((nb, s, w), F32) for w in widths],
        compiler_params=_cparams(("parallel", "parallel")),
        name="in_proj",
    )(x, mod, g1, w_in_p)


def _halo_specs(width, halo, s):
    per = TM // halo
    cur = pl.BlockSpec((1, TM, width), lambda b, t: (b, t, 0))
    prev = pl.BlockSpec((1, halo, width), lambda b, t: (b, jnp.maximum(t * per - 1, 0), 0))
    nxt = pl.BlockSpec((1, halo, width), lambda b, t: (b, jnp.minimum((t + 1) * per, s // halo - 1), 0))
    return [cur, prev, nxt]


def _halo_flags(nct, nt):
    t = pl.program_id(1)
    prev_ok = jnp.logical_and(t != 0, t != nct)
    next_ok = jnp.logical_and(t != nct - 1, t != nt - 1)
    return prev_ok, next_ok


def _conformer_kernel(cur_ref, prev_ref, next_ref, dww_ref, dwb_ref, lng_ref, lnb_ref, pww_ref, pwb_ref,
                      o_ref, ext_ref, *, nct, nt):
    prev_ok, next_ok = _halo_flags(nct, nt)

    def glu(v):
        return v[:, :CONV_CH] * jax.nn.sigmoid(v[:, CONV_CH:])

    ext_ref[0:CONV_HALO, :] = jnp.where(prev_ok, glu(prev_ref[0]), 0.0)
    ext_ref[CONV_HALO:CONV_HALO + TM, :] = glu(cur_ref[0])
    ext_ref[CONV_HALO + TM:2 * CONV_HALO + TM, :] = jnp.where(next_ok, glu(next_ref[0]), 0.0)
    rb = 64
    off = CONV_HALO - CONV_WIDTH // 2
    for r in range(TM // rb):
        acc = jnp.zeros((rb, CONV_CH), F32) + dwb_ref[...]
        for j in range(CONV_WIDTH):
            acc = acc + ext_ref[pl.ds(r * rb + off + j, rb), :] * dww_ref[j:j + 1, :]
        mu = jnp.mean(acc, axis=-1, keepdims=True)
        xc = acc - mu
        y = xc * lax.rsqrt(jnp.mean(xc * xc, axis=-1, keepdims=True) + EPS) * lng_ref[...] + lnb_ref[...]
        h = _silu(y).astype(BF16)
        o_ref[0, r * rb:(r + 1) * rb, :] = _dot(h, pww_ref[...]) + pwb_ref[...]


def _conformer(gvgg, dww, dwb, lng, lnb, pww, pwb, nct):
    nb, s, _ = gvgg.shape
    nt = s // TM
    row = lambda w: pl.BlockSpec((1, w), lambda b, t: (0, 0))
    return pl.pallas_call(
        functools.partial(_conformer_kernel, nct=nct, nt=nt),
        grid=(nb, nt),
        in_specs=_halo_specs(2 * CONV_CH, CONV_HALO, s) + [
            pl.BlockSpec((CONV_WIDTH + 1, CONV_CH), lambda b, t: (0, 0)),
            row(CONV_CH), row(CONV_CH), row(CONV_CH),
            pl.BlockSpec((CONV_CH, CONV_CH), lambda b, t: (0, 0)),
            row(CONV_CH)],
        out_specs=pl.BlockSpec((1, TM, CONV_CH), lambda b, t: (b, t, 0)),
        out_shape=jax.ShapeDtypeStruct((nb, s, CONV_CH), F32),
        scratch_shapes=[pltpu.VMEM((TM + 2 * CONV_HALO, CONV_CH), F32)],
        compiler_params=_cparams(("parallel", "parallel")),
        name="conformer",
    )(gvgg, gvgg, gvgg, dww, dwb, lng, lnb, pww, pwb)


def _gdn_feat_kernel(cur_ref, prev_ref, next_ref, cw_ref, ab_ref, alog_ref, dtb_ref,
                     q_ref, k_ref, v_ref, gb_ref, ext_ref, *, nct, nt):
    prev_ok, next_ok = _halo_flags(nct, nt)
    ext_ref[0:SHORT_HALO, :] = jnp.where(prev_ok, prev_ref[0], 0.0)
    ext_ref[SHORT_HALO:SHORT_HALO + TM, :] = cur_ref[0]
    ext_ref[SHORT_HALO + TM:2 * SHORT_HALO + TM, :] = jnp.where(next_ok, next_ref[0], 0.0)
    rb = 32
    off = SHORT_HALO - SHORT_CONV // 2
    for r in range(TM // rb):
        acc = jnp.zeros((rb, GDN_QKV), F32)
        for j in range(SHORT_CONV):
            acc = acc + ext_ref[pl.ds(r * rb + off + j, rb), :] * cw_ref[j:j + 1, :]
        y = _silu(acc)
        rows = slice(r * rb, (r + 1) * rb)
        for h in range(GDN_HEADS):
            qh = y[:, h * GDN_DK:(h + 1) * GDN_DK]
            kh = y[:, GDN_WIDTH + h * GDN_DK:GDN_WIDTH + (h + 1) * GDN_DK]
            vh = y[:, 2 * GDN_WIDTH + h * GDN_DK:2 * GDN_WIDTH + (h + 1) * GDN_DK]
            qn = qh * lax.rsqrt(jnp.sum(qh * qh, axis=-1, keepdims=True) + EPS) * (GDN_DK ** -0.5)
            kn = kh * lax.rsqrt(jnp.sum(kh * kh, axis=-1, keepdims=True) + EPS)
            q_ref[0, h, rows, :] = qn
            k_ref[0, h, rows, :] = kn
            v_ref[0, h, rows, :] = vh
    ab = ab_ref[0]
    lane = lax.broadcasted_iota(jnp.int32, ab.shape, 1)
    xa = ab + dtb_ref[...]
    softplus = jnp.maximum(xa, 0.0) + jnp.log(1.0 + jnp.exp(-jnp.abs(xa)))
    g = -jnp.exp(alog_ref[...]) * softplus
    beta = jax.nn.sigmoid(ab)
    gb_ref[0] = jnp.where(lane < 2 * GDN_HEADS, g, jnp.where(lane < 4 * GDN_HEADS, beta, 0.0))


def _gdn_features(qkv, ab, cw, alog, dtb, nct):
    nb, s, _ = qkv.shape
    nt = s // TM
    head = pl.BlockSpec((1, GDN_HEADS, TM, GDN_DK), lambda b, t: (b, 0, t, 0))
    hshape = jax.ShapeDtypeStruct((nb, GDN_HEADS, s, GDN_DK), F32)
    return pl.pallas_call(
        functools.partial(_gdn_feat_kernel, nct=nct, nt=nt),
        grid=(nb, nt),
        in_specs=_halo_specs(GDN_QKV, SHORT_HALO, s) + [
            pl.BlockSpec((8, GDN_QKV), lambda b, t: (0, 0)),
            pl.BlockSpec((1, TM, LANES), lambda b, t: (b, t, 0)),
            pl.BlockSpec((1, LANES), lambda b, t: (0, 0)),
            pl.BlockSpec((1, LANES), lambda b, t: (0, 0))],
        out_specs=[head, head, head, pl.BlockSpec((1, TM, LANES), lambda b, t: (b, t, 0))],
        out_shape=[hshape, hshape, hshape, jax.ShapeDtypeStruct((nb, s, LANES), F32)],
        scratch_shapes=[pltpu.VMEM((TM + 2 * SHORT_HALO, GDN_QKV), F32)],
        compiler_params=_cparams(("parallel", "parallel")),
        name="gdn_features",
    )(qkv, qkv, qkv, cw, ab, alog, dtb)


def _gdn_scan_kernel(qf_ref, kf_ref, vf_ref, gf_ref, qb_ref, kb_ref, vb_ref, gbk_ref,
                     of_ref, ob_ref, s_ref):
    n = pl.program_id(1)

    @pl.when(n == 0)
    def _():
        s_ref[...] = jnp.zeros(s_ref.shape, F32)

    c = GDN_C
    row = lax.broadcasted_iota(jnp.int32, (c, c), 0)
    col = lax.broadcasted_iota(jnp.int32, (c, c), 1)
    eye = (row == col).astype(F32)
    blk = {2 ** e: (row >> e) == (col >> e) for e in range(1, int(math.log2(c)) + 1)}
    for d, (q_ref, k_ref, v_ref, g_ref, o_ref) in enumerate(
            ((qf_ref, kf_ref, vf_ref, gf_ref, of_ref), (qb_ref, kb_ref, vb_ref, gbk_ref, ob_ref))):
        incl = (row >= col) if d == 0 else (row <= col)
        strict = (row > col) if d == 0 else (row < col)
        gb = g_ref[0]
        gam_all = _dot(incl.astype(F32), gb, precision=HIGHEST)
        gam_t = gam_all.T
        last_row = c - 1 if d == 0 else 0
        for h in range(GDN_HEADS):
            cc = d * GDN_HEADS + h
            gam = gam_all[:, cc:cc + 1]
            gam_r = gam_t[cc:cc + 1, :]
            beta = gb[:, 2 * GDN_HEADS + cc:2 * GDN_HEADS + cc + 1]
            last = gam_all[last_row:last_row + 1, cc:cc + 1]
            decay = jnp.exp(jnp.where(incl, gam - gam_r, -1e30))
            qh = q_ref[0, h]
            kh = k_ref[0, h]
            vh = v_ref[0, h]
            kbeta = kh * beta
            a = jnp.where(strict, _dot_nt(kbeta, kh) * decay, 0.0)
            t = eye - jnp.where(blk[2], a, 0.0)
            bsz = 2
            while bsz < c:
                lb = jnp.where(jnp.logical_and(blk[2 * bsz], jnp.logical_not(blk[bsz])), a, 0.0)
                t = t - _dot(_dot(t, lb), t)
                bsz *= 2
            u = _dot(t, vh * beta)
            w = _dot(t, kbeta * jnp.exp(gam))
            kd = kh * jnp.exp(last - gam)
            pm = _dot_nt(qh, kh) * decay
            qg = qh * jnp.exp(gam)
            s = s_ref[d, h]
            v_new = u - _dot(w, s)
            o_ref[0, :, h * GDN_DK:(h + 1) * GDN_DK] = _dot(qg, s) + _dot(pm, v_new)
            s_ref[d, h] = s * jnp.exp(last) + _dot_tn(kd, v_new)


def _gdn_scan(q, k, v, gb, ncc):
    nb, _, s, _ = q.shape
    nc = s // GDN_C

    def bwd(n):
        return jnp.where(n < ncc, ncc - 1 - n, nc + ncc - 1 - n)

    hf = pl.BlockSpec((1, GDN_HEADS, GDN_C, GDN_DK), lambda b, n: (b, 0, n, 0))
    hb = pl.BlockSpec((1, GDN_HEADS, GDN_C, GDN_DK), lambda b, n: (b, 0, bwd(n), 0))
    gf = pl.BlockSpec((1, GDN_C, LANES), lambda b, n: (b, n, 0))
    gk = pl.BlockSpec((1, GDN_C, LANES), lambda b, n: (b, bwd(n), 0))
    of = pl.BlockSpec((1, GDN_C, GDN_WIDTH), lambda b, n: (b, n, 0))
    ob = pl.BlockSpec((1, GDN_C, GDN_WIDTH), lambda b, n: (b, bwd(n), 0))
    oshape = jax.ShapeDtypeStruct((nb, s, GDN_WIDTH), F32)
    return pl.pallas_call(
        _gdn_scan_kernel,
        grid=(nb, nc),
        in_specs=[hf, hf, hf, gf, hb, hb, hb, gk],
        out_specs=[of, ob],
        out_shape=[oshape, oshape],
        scratch_shapes=[pltpu.VMEM((2, GDN_HEADS, GDN_DK, GDN_DK), F32)],
        compiler_params=_cparams(("parallel", "arbitrary")),
        name="gdn_scan",
    )(q, k, v, gb, q, k, v, gb)


def _attn_prep_kernel(q_ref, kv_ref, cos_ref, sin_ref, qg_ref, kg_ref, bd_ref,
                      qh_ref, kh_ref, vh_ref, *, q_scale):
    cos = cos_ref[...]
    sin = sin_ref[...]
    bd = bd_ref[...]
    lane = lax.broadcasted_iota(jnp.int32, cos.shape, 1)
    first = (lane % 32) < 16

    def norm_rope(x, g):
        ss = _dot(x * x, bd, precision=HIGHEST)
        y = x * lax.rsqrt(ss * (1.0 / HEAD_DIM) + EPS) * g
        swapped = jnp.where(first, pltpu.roll(y, LANES - 16, 1), pltpu.roll(y, 16, 1))
        return y * cos + swapped * sin

    q = q_ref[0]
    for j in range(ATTN_WIDTH // LANES):
        y = norm_rope(q[:, j * LANES:(j + 1) * LANES], qg_ref[...]) * q_scale
        qh_ref[0, 2 * j] = y[:, :HEAD_DIM].astype(BF16)
        qh_ref[0, 2 * j + 1] = y[:, HEAD_DIM:].astype(BF16)
    kv = kv_ref[0]
    y = norm_rope(kv[:, :LANES], kg_ref[...])
    kh_ref[0, 0] = y[:, :HEAD_DIM].astype(BF16)
    kh_ref[0, 1] = y[:, HEAD_DIM:].astype(BF16)
    vh_ref[0, 0] = kv[:, LANES:LANES + HEAD_DIM].astype(BF16)
    vh_ref[0, 1] = kv[:, LANES + HEAD_DIM:].astype(BF16)


def _attn_prep(q, kv, cos_t, sin_t, qg, kg, bd, q_scale):
    nb, s, _ = q.shape
    nt = s // TM
    tab = pl.BlockSpec((TM, LANES), lambda b, t: (t, 0))
    row = pl.BlockSpec((1, LANES), lambda b, t: (0, 0))
    heads = lambda nh: pl.BlockSpec((1, nh, TM, HEAD_DIM), lambda b, t: (b, 0, t, 0))
    return pl.pallas_call(
        functools.partial(_attn_prep_kernel, q_scale=q_scale),
        grid=(nb, nt),
        in_specs=[pl.BlockSpec((1, TM, ATTN_WIDTH), lambda b, t: (b, t, 0)),
                  pl.BlockSpec((1, TM, 2 * LANES), lambda b, t: (b, t, 0)),
                  tab, tab, row, row,
                  pl.BlockSpec((LANES, LANES), lambda b, t: (0, 0))],
        out_specs=[heads(ATTN_Q_HEADS), heads(ATTN_KV_HEADS), heads(ATTN_KV_HEADS)],
        out_shape=[jax.ShapeDtypeStruct((nb, ATTN_Q_HEADS, s, HEAD_DIM), BF16),
                   jax.ShapeDtypeStruct((nb, ATTN_KV_HEADS, s, HEAD_DIM), BF16),
                   jax.ShapeDtypeStruct((nb, ATTN_KV_HEADS, s, HEAD_DIM), BF16)],
        compiler_params=_cparams(("parallel", "parallel")),
        name="attn_prep",
    )(q, kv, cos_t, sin_t, qg, kg, bd)


def _attn_kernel(q_ref, k_ref, v_ref, o_ref, m_ref, l_ref, acc_ref, *, nctq, lc, s_len):
    qi = pl.program_id(2)
    rows = ATTN_GROUP * ATT_TQ
    q = q_ref[0].reshape(rows, HEAD_DIM)
    m_ref[...] = jnp.full(m_ref.shape, -jnp.inf, F32)
    l_ref[...] = jnp.zeros(l_ref.shape, F32)
    acc_ref[...] = jnp.zeros(acc_ref.shape, F32)
    nk = jnp.where(qi < nctq, lc // ATT_TK, s_len // ATT_TK)

    def body(j, carry):
        ks = pl.multiple_of(j * ATT_TK, ATT_TK)
        k = k_ref[0, 0, pl.ds(ks, ATT_TK), :]
        v = v_ref[0, 0, pl.ds(ks, ATT_TK), :]
        s = _dot_nt(q, k)
        m_old = m_ref[...]
        m_new = jnp.maximum(m_old, jnp.max(s, axis=-1, keepdims=True))
        alpha = jnp.exp2(m_old - m_new)
        p = jnp.exp2(s - m_new)
        l_ref[...] = alpha * l_ref[...] + jnp.sum(p, axis=-1, keepdims=True)
        acc_ref[...] = alpha * acc_ref[...] + _dot(p.astype(BF16), v)
        m_ref[...] = m_new
        return carry

    lax.fori_loop(0, nk, body, 0)
    out = acc_ref[...] / l_ref[...]
    for j in range(ATTN_GROUP):
        o_ref[0, :, j * HEAD_DIM:(j + 1) * HEAD_DIM] = out[j * ATT_TQ:(j + 1) * ATT_TQ, :]


def _attention(qh, kh, vh, lc):
    nb, _, s, _ = qh.shape
    nq = s // ATT_TQ
    rows = ATTN_GROUP * ATT_TQ
    return pl.pallas_call(
        functools.partial(_attn_kernel, nctq=lc // ATT_TQ, lc=lc, s_len=s),
        grid=(nb, ATTN_KV_HEADS, nq),
        in_specs=[pl.BlockSpec((1, ATTN_GROUP, ATT_TQ, HEAD_DIM), lambda b, g, i: (b, g, i, 0)),
                  pl.BlockSpec((1, 1, s, HEAD_DIM), lambda b, g, i: (b, g, 0, 0)),
                  pl.BlockSpec((1, 1, s, HEAD_DIM), lambda b, g, i: (b, g, 0, 0))],
        out_specs=pl.BlockSpec((1, ATT_TQ, ATTN_GROUP * HEAD_DIM), lambda b, g, i: (b, i, g)),
        out_shape=jax.ShapeDtypeStruct((nb, s, ATTN_WIDTH), F32),
        scratch_shapes=[pltpu.VMEM((rows, 1), F32), pltpu.VMEM((rows, 1), F32),
                        pltpu.VMEM((rows, HEAD_DIM), F32)],
        compiler_params=_cparams(("parallel", "parallel", "parallel")),
        name="attention",
    )(qh, kh, vh)


def _outproj_kernel(conv_ref, of_ref, ob_ref, z_ref, attn_ref, x_ref, mod_ref, gg_ref, bd_ref, w_ref,
                    n2_ref, wr_ref, br_ref, xo_ref, h2_ref, lg_ref):
    o = of_ref[0] + ob_ref[0]
    ss = _dot(o * o, bd_ref[...], precision=HIGHEST)
    gdn = o * lax.rsqrt(ss * (1.0 / GDN_DK) + EPS) * gg_ref[...] * _silu(z_ref[0])
    y = _dot(conv_ref[0].astype(BF16), w_ref[0:CONV_CH, :])
    y = y + _dot(gdn.astype(BF16), w_ref[CONV_CH:CONV_CH + GDN_WIDTH, :])
    y = y + _dot(attn_ref[0].astype(BF16), w_ref[CONV_CH + GDN_WIDTH:, :])
    x = x_ref[0] + mod_ref[0, 2:3, :] * y
    xo_ref[0] = x
    h2 = _rms(x, n2_ref[...]) * (1.0 + mod_ref[0, 4:5, :]) + mod_ref[0, 3:4, :]
    h2_ref[0] = h2.astype(BF16)
    lg_ref[0] = _dot(h2, wr_ref[...], precision=HIGHEST) + br_ref[...]


def _out_proj(conv, o_f, o_b, z, attn, x, mod, gdn_g, bd, w_out, n2, wr, br, nct):
    nb, s, d = x.shape
    nt = s // TM
    tok = lambda w: pl.BlockSpec((1, TM, w), lambda b, t: (b, t, 0))
    full = lambda a: pl.BlockSpec(a.shape, lambda b, t: (0,) * a.ndim)
    return pl.pallas_call(
        _outproj_kernel,
        grid=(nb, nt),
        in_specs=[tok(CONV_CH), tok(GDN_WIDTH), tok(GDN_WIDTH), tok(GDN_WIDTH), tok(ATTN_WIDTH), tok(d),
                  pl.BlockSpec((1, 6, d), lambda b, t: (jnp.where(t < nct, nb, b), 0, 0)),
                  full(gdn_g), full(bd), full(w_out), full(n2), full(wr), full(br)],
        out_specs=[tok(d), tok(d), tok(LANES)],
        out_shape=[jax.ShapeDtypeStruct((nb, s, d), F32), jax.ShapeDtypeStruct((nb, s, d), BF16),
                   jax.ShapeDtypeStruct((nb, s, LANES), F32)],
        compiler_params=_cparams(("parallel", "parallel")),
        name="out_proj",
    )(conv, o_f, o_b, z, attn, x, mod, gdn_g, bd, w_out, n2, wr, br)


def _expert_kernel(be_ref, nu_ref, x_ref, wg_ref, wu_ref, wd_ref, o_ref):
    i = pl.program_id(0)

    @pl.when(i < nu_ref[0])
    def _():
        x = x_ref[...]
        g = _dot(x, wg_ref[0, 0].astype(BF16))
        u = _dot(x, wu_ref[0, 0].astype(BF16))
        h = (_silu(g) * u).astype(BF16)
        o_ref[...] = _dot(h, wd_ref[0, 0].astype(BF16))

    @pl.when(i >= nu_ref[0])
    def _():
        o_ref[...] = jnp.zeros(o_ref.shape, F32)


def _experts(xs, blk_e, n_used, w_gate, w_up, w_down, layer):
    n_slots, d = xs.shape
    nblk = n_slots // MOE_BM
    de = w_gate.shape[-1]
    grid_spec = pltpu.PrefetchScalarGridSpec(
        num_scalar_prefetch=2,
        grid=(nblk,),
        in_specs=[pl.BlockSpec((MOE_BM, d), lambda i, be, nu: (jnp.minimum(i, nu[0] - 1), 0)),
                  pl.BlockSpec((1, 1, d, de), lambda i, be, nu: (layer, be[i], 0, 0)),
                  pl.BlockSpec((1, 1, d, de), lambda i, be, nu: (layer, be[i], 0, 0)),
                  pl.BlockSpec((1, 1, de, d), lambda i, be, nu: (layer, be[i], 0, 0))],
        out_specs=pl.BlockSpec((MOE_BM, d), lambda i, be, nu: (i, 0)),
    )
    return pl.pallas_call(
        _expert_kernel,
        grid_spec=grid_spec,
        out_shape=jax.ShapeDtypeStruct((n_slots, d), F32),
        compiler_params=_cparams(("arbitrary",)),
        name="experts",
    )(blk_e, n_used, xs, w_gate, w_up, w_down)


def _combine_kernel(x_ref, y_ref, mod_ref, fg_ref, o_ref, *, final):
    x = x_ref[0] + mod_ref[0, 5:6, :] * y_ref[0]
    if final:
        x = _rms(x, fg_ref[...])
    o_ref[0] = x


def _combine(x, y, mod, fg, nct, final):
    nb, s, d = x.shape
    nt = s // TM
    tok = pl.BlockSpec((1, TM, d), lambda b, t: (b, t, 0))
    return pl.pallas_call(
        functools.partial(_combine_kernel, final=final),
        grid=(nb, nt),
        in_specs=[tok, tok,
                  pl.BlockSpec((1, 6, d), lambda b, t: (jnp.where(t < nct, nb, b), 0, 0)),
                  pl.BlockSpec((1, d), lambda b, t: (0, 0))],
        out_specs=tok,
        out_shape=jax.ShapeDtypeStruct((nb, s, d), F32),
        compiler_params=_cparams(("parallel", "parallel")),
        name="combine",
    )(x, y, mod, fg)


def _route(logits, n_tok):
    lg = logits.reshape(n_tok, LANES)
    pg = jax.nn.softmax(lg[:, :N_GROUPS], axis=-1)
    grp = jnp.argmax(pg, axis=-1)
    pg_sel = jnp.take_along_axis(pg, grp[:, None], axis=-1)
    le = lg[:, N_GROUPS:N_GROUPS + N_EXPERTS].reshape(n_tok, N_GROUPS, EXPERTS_PER_GROUP)
    le = jnp.take_along_axis(le, grp[:, None, None], axis=1)[:, 0]
    top_p, top_i = lax.top_k(jax.nn.softmax(le, axis=-1), TOP_K)
    wts = pg_sel * top_p / jnp.sum(top_p, axis=-1, keepdims=True)
    eid = (grp[:, None] * EXPERTS_PER_GROUP + top_i).astype(jnp.int32)
    flat_e = eid.reshape(-1)
    onehot = (flat_e[:, None] == jnp.arange(N_EXPERTS, dtype=jnp.int32)[None, :]).astype(jnp.int32)
    rank = jnp.take_along_axis(jnp.cumsum(onehot, axis=0) - onehot, flat_e[:, None], axis=1)[:, 0]
    counts = jnp.sum(onehot, axis=0)
    pcounts = (counts + MOE_BM - 1) // MOE_BM * MOE_BM
    pends = jnp.cumsum(pcounts)
    slot = (pends - pcounts)[flat_e] + rank
    n_slots = (-(-n_tok * TOP_K // MOE_BM) + N_EXPERTS) * MOE_BM
    flat_t = jnp.repeat(jnp.arange(n_tok, dtype=jnp.int32), TOP_K)
    slot_tok = jnp.zeros((n_slots,), jnp.int32).at[slot].set(flat_t)
    nblk = n_slots // MOE_BM
    blk_e = jnp.minimum(jnp.searchsorted(pends, jnp.arange(nblk) * MOE_BM, side="right"),
                        N_EXPERTS - 1).astype(jnp.int32)
    n_used = (pends[-1] // MOE_BM).astype(jnp.int32).reshape(1)
    return slot.reshape(n_tok, TOP_K), wts, slot_tok, blk_e, n_used


def _rope_tables(lc, l):
    rows = l // GRID_W
    row = jnp.repeat(jnp.arange(rows), GRID_W).astype(F32)
    col = (jnp.arange(rows * GRID_W) % GRID_W).astype(F32)
    n_freq = HEAD_DIM // 4
    inv = ROPE_THETA ** (-jnp.arange(n_freq, dtype=F32) / n_freq)
    ar, ac = row[:, None] * inv, col[:, None] * inv
    cos = jnp.concatenate([jnp.cos(ar), jnp.cos(ar), jnp.cos(ac), jnp.cos(ac)], axis=-1)
    sin = jnp.concatenate([-jnp.sin(ar), jnp.sin(ar), -jnp.sin(ac), jnp.sin(ac)], axis=-1)
    cos = jnp.concatenate([jnp.ones((lc, HEAD_DIM), F32), cos], axis=0)
    sin = jnp.concatenate([jnp.zeros((lc, HEAD_DIM), F32), sin], axis=0)
    return jnp.tile(cos, (1, 2)), jnp.tile(sin, (1, 2))


def _block_diag_ones(n, blk):
    i = jnp.arange(n)
    return (i[:, None] // blk == i[None, :] // blk).astype(F32)


def kernel(x, c, ctx, c_ctx, mod_w, mod_b, norm1_g, norm2_g, w_in, conv_dw_w, conv_dw_b, conv_ln_g, conv_ln_b, conv_pw_w, conv_pw_b, gdn_conv_w, gdn_a_log, gdn_dt_bias, gdn_norm_g, attn_q_norm_g, attn_k_norm_g, w_out, router_group_w, router_group_b, router_expert_w, router_expert_b, expert_w_gate, expert_w_up, expert_w_down, final_norm_g):
    nb, l, d = x.shape
    lc = ctx.shape[1]
    depth = mod_w.shape[0]
    s = lc + l
    assert lc % TM == 0 and l % TM == 0 and lc % GDN_C == 0 and l % GDN_C == 0
    assert lc % ATT_TK == 0 and s % ATT_TK == 0 and lc % ATT_TQ == 0
    nct = lc // TM
    n_tok = nb * s

    nr = -(-(nb + 1) // 8) * 8
    cvec = jnp.zeros((nr, d), F32).at[:nb].set(c).at[nb].set(c_ctx)
    mod_all = _modulation(cvec, mod_w, mod_b).reshape(depth, nr, 6, d)

    cos_t, sin_t = _rope_tables(lc, l)
    bd64 = _block_diag_ones(LANES, HEAD_DIM)
    bd_gdn = _block_diag_ones(GDN_WIDTH, GDN_DK)
    q_scale = (HEAD_DIM ** -0.5) * math.log2(math.e)
    row = lambda v: v.reshape(1, -1).astype(F32)
    pad_lanes = lambda v: jnp.zeros((1, LANES), F32).at[0, :v.size].set(v.reshape(-1))

    xs = jnp.concatenate([ctx, x], axis=1)
    for layer in range(depth):
        mod = mod_all[layer]
        wi = w_in[layer]
        w_in_p = jnp.concatenate(
            [wi[:, :IN_A0 + 16], jnp.zeros((d, IN_PAD), F32), wi[:, IN_A0 + 16:]], axis=1).astype(BF16)
        qkv, ab, kv, gvgg, z, q = _in_proj(xs, mod, row(norm1_g[layer]), w_in_p, nct)

        dww = jnp.concatenate([conv_dw_w[layer], jnp.zeros((1, CONV_CH), F32)], axis=0)
        conv = _conformer(gvgg, dww, row(conv_dw_b[layer]), row(conv_ln_g[layer]), row(conv_ln_b[layer]),
                          conv_pw_w[layer].astype(BF16), row(conv_pw_b[layer]), nct)

        cw = jnp.concatenate([gdn_conv_w[layer], jnp.zeros((8 - SHORT_CONV, GDN_QKV), F32)], axis=0)
        gq, gk, gv, gb = _gdn_features(qkv, ab, cw, pad_lanes(gdn_a_log[layer]), pad_lanes(gdn_dt_bias[layer]), nct)
        o_f, o_b = _gdn_scan(gq, gk, gv, gb, lc // GDN_C)

        qh, kh, vh = _attn_prep(q, kv, cos_t, sin_t,
                                row(jnp.tile(attn_q_norm_g[layer], 2)), row(jnp.tile(attn_k_norm_g[layer], 2)),
                                bd64, q_scale)
        attn = _attention(qh, kh, vh, lc)

        wr = jnp.zeros((d, LANES), F32).at[:, :N_GROUPS].set(router_group_w[layer])
        wr = wr.at[:, N_GROUPS:N_GROUPS + N_EXPERTS].set(router_expert_w[layer])
        br = jnp.zeros((1, LANES), F32).at[0, :N_GROUPS].set(router_group_b[layer])
        br = br.at[0, N_GROUPS:N_GROUPS + N_EXPERTS].set(router_expert_b[layer])
        xs, h2, logits = _out_proj(conv, o_f, o_b, z, attn, xs, mod,
                                   row(jnp.tile(gdn_norm_g[layer], GDN_HEADS)), bd_gdn,
                                   w_out[layer].astype(BF16), row(norm2_g[layer]), wr, br, nct)

        slot, wts, slot_tok, blk_e, n_used = _route(logits, n_tok)
        xe = h2.reshape(n_tok, d)[slot_tok]
        ye = _experts(xe, blk_e, n_used, expert_w_gate, expert_w_up, expert_w_down, layer)
        y = ye[slot[:, 0]] * wts[:, 0:1] + ye[slot[:, 1]] * wts[:, 1:2]
        xs = _combine(xs, y.reshape(nb, s, d), mod, row(final_norm_g), nct, layer == depth - 1)
    return xs[:, lc:, :]
```

```python
import functools
import math

import jax
import jax.numpy as jnp
from jax import lax
from jax.experimental import pallas as pl
from jax.experimental.pallas import tpu as pltpu

F32 = jnp.float32
BF16 = jnp.bfloat16
HIGHEST = lax.Precision.HIGHEST

EPS = 1e-6
GRID_W = 64
CONV_CH = 256
CONV_WIDTH = 31
GDN_HEADS = 4
GDN_DK = 64
GDN_WIDTH = 256
GDN_QKV = 768
SHORT_CONV = 5
HEAD_DIM = 64
ATTN_Q_HEADS = 8
ATTN_KV_HEADS = 2
ATTN_GROUP = 4
ATTN_WIDTH = 512
ROPE_THETA = 10000.0
N_GROUPS = 4
EXPERTS_PER_GROUP = 8
N_EXPERTS = 32
TOP_K = 2
D_EXPERT = 512

LANES = 128
TM = 256
GDN_C = 128
CONV_HALO = 16
SHORT_HALO = 8
ATT_TQ = 256
ATT_TK = 256
VT_ROWS = 80
MOE_BM = 256
VMEM_LIMIT = 56 * 1024 * 1024

IN_A0 = GDN_QKV
IN_PAD = LANES - 16
C_QKV = (0, 768)
C_AB = (768, 896)
C_KV = (896, 1152)
C_GVGG = (1152, 1664)
C_Z = (1664, 1920)
C_Q = (1920, 2432)
IN_COLS = 2432


def _cparams(sem):
    return pltpu.CompilerParams(dimension_semantics=sem, vmem_limit_bytes=VMEM_LIMIT)


def _silu(x):
    return x * jax.nn.sigmoid(x)


def _dot(a, b, **kw):
    return jnp.dot(a, b, preferred_element_type=F32, **kw)


def _dot_nt(a, b):
    return lax.dot_general(a, b, (((1,), (1,)), ((), ())), preferred_element_type=F32)


def _dot_tn(a, b):
    return lax.dot_general(a, b, (((0,), (0,)), ((), ())), preferred_element_type=F32)


def _mod_kernel(c_ref, w_ref, b_ref, o_ref):
    c = c_ref[...]
    o_ref[0] = _dot(_silu(c), w_ref[0], precision=HIGHEST) + b_ref[0]


def _modulation(cvec, mod_w, mod_b):
    depth, d, n = mod_w.shape
    nr = cvec.shape[0]
    tn = 768
    return pl.pallas_call(
        _mod_kernel,
        grid=(depth, n // tn),
        in_specs=[pl.BlockSpec((nr, d), lambda l, j: (0, 0)),
                  pl.BlockSpec((1, d, tn), lambda l, j: (l, 0, j)),
                  pl.BlockSpec((1, 1, tn), lambda l, j: (l, 0, j))],
        out_specs=pl.BlockSpec((1, nr, tn), lambda l, j: (l, 0, j)),
        out_shape=jax.ShapeDtypeStruct((depth, nr, n), F32),
        compiler_params=_cparams(("parallel", "parallel")),
        name="modulation",
    )(cvec, mod_w, mod_b.reshape(depth, 1, n))


def _rms(x, g):
    return x * lax.rsqrt(jnp.mean(x * x, axis=-1, keepdims=True) + EPS) * g


def _inproj_kernel(x_ref, mod_ref, g_ref, w_ref, qkv_ref, ab_ref, kv_ref, gvgg_ref, z_ref, q_ref):
    x = x_ref[0]
    sh = mod_ref[0, 0:1, :]
    sc = mod_ref[0, 1:2, :]
    h = (_rms(x, g_ref[...]) * (1.0 + sc) + sh).astype(BF16)
    for ref, (c0, c1) in ((qkv_ref, C_QKV), (ab_ref, C_AB), (kv_ref, C_KV),
                          (gvgg_ref, C_GVGG), (z_ref, C_Z), (q_ref, C_Q)):
        ref[0] = _dot(h, w_ref[:, c0:c1])


def _in_proj(x, mod, g1, w_in_p, nct):
    nb, s, d = x.shape
    nt = s // TM
    widths = [c1 - c0 for c0, c1 in (C_QKV, C_AB, C_KV, C_GVGG, C_Z, C_Q)]
    tok = lambda w: pl.BlockSpec((1, TM, w), lambda b, t: (b, t, 0))
    return pl.pallas_call(
        _inproj_kernel,
        grid=(nb, nt),
        in_specs=[tok(d),
                  pl.BlockSpec((1, 6, d), lambda b, t: (jnp.where(t < nct, nb, b), 0, 0)),
                  pl.BlockSpec((1, d), lambda b, t: (0, 0)),
                  pl.BlockSpec((d, IN_COLS), lambda b, t: (0, 0))],
        out_specs=[tok(w) for w in widths],
        out_shape=[jax.ShapeDtypeStruct((nb, s, w), F32) for w in widths],
        compiler_params=_cparams(("parallel", "parallel")),
        name="in_proj",
    )(x, mod, g1, w_in_p)


def _halo_specs(width, halo, s):
    per = TM // halo
    cur = pl.BlockSpec((1, TM, width), lambda b, t: (b, t, 0))
    prev = pl.BlockSpec((1, halo, width), lambda b, t: (b, jnp.maximum(t * per - 1, 0), 0))
    nxt = pl.BlockSpec((1, halo, width), lambda b, t: (b, jnp.minimum((t + 1) * per, s // halo - 1), 0))
    return [cur, prev, nxt]


def _halo_flags(nct, nt):
    t = pl.program_id(1)
    prev_ok = jnp.logical_and(t != 0, t != nct)
    next_ok = jnp.logical_and(t != nct - 1, t != nt - 1)
    return prev_ok, next_ok


def _conformer_kernel(cur_ref, prev_ref, next_ref, dww_ref, dwb_ref, lng_ref, lnb_ref, pww_ref, pwb_ref,
                      o_ref, ext_ref, *, nct, nt):
    prev_ok, next_ok = _halo_flags(nct, nt)

    def glu(v):
        return v[:, :CONV_CH] * jax.nn.sigmoid(v[:, CONV_CH:])

    ext_ref[0:CONV_HALO, :] = jnp.where(prev_ok, glu(prev_ref[0]), 0.0)
    ext_ref[CONV_HALO:CONV_HALO + TM, :] = glu(cur_ref[0])
    ext_ref[CONV_HALO + TM:2 * CONV_HALO + TM, :] = jnp.where(next_ok, glu(next_ref[0]), 0.0)
    rb = 64
    off = CONV_HALO - CONV_WIDTH // 2
    for r in range(TM // rb):
        acc = jnp.zeros((rb, CONV_CH), F32) + dwb_ref[...]
        for j in range(CONV_WIDTH):
            acc = acc + ext_ref[pl.ds(r * rb + off + j, rb), :] * dww_ref[j:j + 1, :]
        mu = jnp.mean(acc, axis=-1, keepdims=True)
        xc = acc - mu
        y = xc * lax.rsqrt(jnp.mean(xc * xc, axis=-1, keepdims=True) + EPS) * lng_ref[...] + lnb_ref[...]
        h = _silu(y).astype(BF16)
        o_ref[0, r * rb:(r + 1) * rb, :] = _dot(h, pww_ref[...]) + pwb_ref[...]


def _conformer(gvgg, dww, dwb, lng, lnb, pww, pwb, nct):
    nb, s, _ = gvgg.shape
    nt = s // TM
    row = lambda w: pl.BlockSpec((1, w), lambda b, t: (0, 0))
    return pl.pallas_call(
        functools.partial(_conformer_kernel, nct=nct, nt=nt),
        grid=(nb, nt),
        in_specs=_halo_specs(2 * CONV_CH, CONV_HALO, s) + [
            pl.BlockSpec((CONV_WIDTH + 1, CONV_CH), lambda b, t: (0, 0)),
            row(CONV_CH), row(CONV_CH), row(CONV_CH),
            pl.BlockSpec((CONV_CH, CONV_CH), lambda b, t: (0, 0)),
            row(CONV_CH)],
        out_specs=pl.BlockSpec((1, TM, CONV_CH), lambda b, t: (b, t, 0)),
        out_shape=jax.ShapeDtypeStruct((nb, s, CONV_CH), F32),
        scratch_shapes=[pltpu.VMEM((TM + 2 * CONV_HALO, CONV_CH), F32)],
        compiler_params=_cparams(("parallel", "parallel")),
        name="conformer",
    )(gvgg, gvgg, gvgg, dww, dwb, lng, lnb, pww, pwb)


def _gdn_feat_kernel(cur_ref, prev_ref, next_ref, cw_ref, ab_ref, alog_ref, dtb_ref,
                     q_ref, k_ref, v_ref, gb_ref, ext_ref, *, nct, nt):
    prev_ok, next_ok = _halo_flags(nct, nt)
    ext_ref[0:SHORT_HALO, :] = jnp.where(prev_ok, prev_ref[0], 0.0)
    ext_ref[SHORT_HALO:SHORT_HALO + TM, :] = cur_ref[0]
    ext_ref[SHORT_HALO + TM:2 * SHORT_HALO + TM, :] = jnp.where(next_ok, next_ref[0], 0.0)
    rb = 32
    off = SHORT_HALO - SHORT_CONV // 2
    for r in range(TM // rb):
        acc = jnp.zeros((rb, GDN_QKV), F32)
        for j in range(SHORT_CONV):
            acc = acc + ext_ref[pl.ds(r * rb + off + j, rb), :] * cw_ref[j:j + 1, :]
        y = _silu(acc)
        rows = slice(r * rb, (r + 1) * rb)
        for h in range(GDN_HEADS):
            qh = y[:, h * GDN_DK:(h + 1) * GDN_DK]
            kh = y[:, GDN_WIDTH + h * GDN_DK:GDN_WIDTH + (h + 1) * GDN_DK]
            vh = y[:, 2 * GDN_WIDTH + h * GDN_DK:2 * GDN_WIDTH + (h + 1) * GDN_DK]
            qn = qh * lax.rsqrt(jnp.sum(qh * qh, axis=-1, keepdims=True) + EPS) * (GDN_DK ** -0.5)
            kn = kh * lax.rsqrt(jnp.sum(kh * kh, axis=-1, keepdims=True) + EPS)
            q_ref[0, h, rows, :] = qn
            k_ref[0, h, rows, :] = kn
            v_ref[0, h, rows, :] = vh
    ab = ab_ref[0]
    lane = lax.broadcasted_iota(jnp.int32, ab.shape, 1)
    xa = ab + dtb_ref[...]
    softplus = jnp.maximum(xa, 0.0) + jnp.log(1.0 + jnp.exp(-jnp.abs(xa)))
    g = -jnp.exp(alog_ref[...]) * softplus
    beta = jax.nn.sigmoid(ab)
    gb_ref[0] = jnp.where(lane < 2 * GDN_HEADS, g, jnp.where(lane < 4 * GDN_HEADS, beta, 0.0))


def _gdn_features(qkv, ab, cw, alog, dtb, nct):
    nb, s, _ = qkv.shape
    nt = s // TM
    head = pl.BlockSpec((1, GDN_HEADS, TM, GDN_DK), lambda b, t: (b, 0, t, 0))
    hshape = jax.ShapeDtypeStruct((nb, GDN_HEADS, s, GDN_DK), F32)
    return pl.pallas_call(
        functools.partial(_gdn_feat_kernel, nct=nct, nt=nt),
        grid=(nb, nt),
        in_specs=_halo_specs(GDN_QKV, SHORT_HALO, s) + [
            pl.BlockSpec((8, GDN_QKV), lambda b, t: (0, 0)),
            pl.BlockSpec((1, TM, LANES), lambda b, t: (b, t, 0)),
            pl.BlockSpec((1, LANES), lambda b, t: (0, 0)),
            pl.BlockSpec((1, LANES), lambda b, t: (0, 0))],
        out_specs=[head, head, head, pl.BlockSpec((1, TM, LANES), lambda b, t: (b, t, 0))],
        out_shape=[hshape, hshape, hshape, jax.ShapeDtypeStruct((nb, s, LANES), F32)],
        scratch_shapes=[pltpu.VMEM((TM + 2 * SHORT_HALO, GDN_QKV), F32)],
        compiler_params=_cparams(("parallel", "parallel")),
        name="gdn_features",
    )(qkv, qkv, qkv, cw, ab, alog, dtb)


def _gdn_scan_kernel(qf_ref, kf_ref, vf_ref, gf_ref, qb_ref, kb_ref, vb_ref, gbk_ref,
                     of_ref, ob_ref, s_ref):
    n = pl.program_id(1)

    @pl.when(n == 0)
    def _():
        s_ref[...] = jnp.zeros(s_ref.shape, F32)

    c = GDN_C
    row = lax.broadcasted_iota(jnp.int32, (c, c), 0)
    col = lax.broadcasted_iota(jnp.int32, (c, c), 1)
    eye = (row == col).astype(F32)
    blk = {2 ** e: (row >> e) == (col >> e) for e in range(1, int(math.log2(c)) + 1)}
    for d, (q_ref, k_ref, v_ref, g_ref, o_ref) in enumerate(
            ((qf_ref, kf_ref, vf_ref, gf_ref, of_ref), (qb_ref, kb_ref, vb_ref, gbk_ref, ob_ref))):
        incl = (row >= col) if d == 0 else (row <= col)
        strict = (row > col) if d == 0 else (row < col)
        gb = g_ref[0]
        gam_all = _dot(incl.astype(F32), gb, precision=HIGHEST)
        gam_t = gam_all.T
        last_row = c - 1 if d == 0 else 0
        for h in range(GDN_HEADS):
            cc = d * GDN_HEADS + h
            gam = gam_all[:, cc:cc + 1]
            gam_r = gam_t[cc:cc + 1, :]
            beta = gb[:, 2 * GDN_HEADS + cc:2 * GDN_HEADS + cc + 1]
            last = gam_all[last_row:last_row + 1, cc:cc + 1]
            decay = jnp.exp(jnp.where(incl, gam - gam_r, -1e30))
            qh = q_ref[0, h]
            kh = k_ref[0, h]
            vh = v_ref[0, h]
            kbeta = kh * beta
            a = jnp.where(strict, _dot_nt(kbeta, kh) * decay, 0.0)
            t = eye - jnp.where(blk[2], a, 0.0)
            bsz = 2
            while bsz < c:
                lb = jnp.where(jnp.logical_and(blk[2 * bsz], jnp.logical_not(blk[bsz])), a, 0.0)
                t = t - _dot(_dot(t, lb), t)
                bsz *= 2
            u = _dot(t, vh * beta)
            w = _dot(t, kbeta * jnp.exp(gam))
            kd = kh * jnp.exp(last - gam)
            pm = _dot_nt(qh, kh) * decay
            qg = qh * jnp.exp(gam)
            s = s_ref[d, h]
            v_new = u - _dot(w, s)
            o_ref[0, :, h * GDN_DK:(h + 1) * GDN_DK] = _dot(qg, s) + _dot(pm, v_new)
            s_ref[d, h] = s * jnp.exp(last) + _dot_tn(kd, v_new)


def _gdn_scan(q, k, v, gb, ncc):
    nb, _, s, _ = q.shape
    nc = s // GDN_C

    def bwd(n):
        return jnp.where(n < ncc, ncc - 1 - n, nc + ncc - 1 - n)

    hf = pl.BlockSpec((1, GDN_HEADS, GDN_C, GDN_DK), lambda b, n: (b, 0, n, 0))
    hb = pl.BlockSpec((1, GDN_HEADS, GDN_C, GDN_DK), lambda b, n: (b, 0, bwd(n), 0))
    gf = pl.BlockSpec((1, GDN_C, LANES), lambda b, n: (b, n, 0))
    gk = pl.BlockSpec((1, GDN_C, LANES), lambda b, n: (b, bwd(n), 0))
    of = pl.BlockSpec((1, GDN_C, GDN_WIDTH), lambda b, n: (b, n, 0))
    ob = pl.BlockSpec((1, GDN_C, GDN_WIDTH), lambda b, n: (b, bwd(n), 0))
    oshape = jax.ShapeDtypeStruct((nb, s, GDN_WIDTH), F32)
    return pl.pallas_call(
        _gdn_scan_kernel,
        grid=(nb, nc),
        in_specs=[hf, hf, hf, gf, hb, hb, hb, gk],
        out_specs=[of, ob],
        out_shape=[oshape, oshape],
        scratch_shapes=[pltpu.VMEM((2, GDN_HEADS, GDN_DK, GDN_DK), F32)],
        compiler_params=_cparams(("parallel", "arbitrary")),
        name="gdn_scan",
    )(q, k, v, gb, q, k, v, gb)


def _attn_prep_kernel(q_ref, kv_ref, cos_ref, sin_ref, qg_ref, kg_ref, bd_ref,
                      qt_ref, kh_ref, vt_ref, *, q_scale):
    cos = cos_ref[...]
    sin = sin_ref[...]
    bd = bd_ref[...]
    lane = lax.broadcasted_iota(jnp.int32, cos.shape, 1)
    first = (lane % 32) < 16

    def norm_rope(x, g):
        ss = _dot(x * x, bd, precision=HIGHEST)
        y = x * lax.rsqrt(ss * (1.0 / HEAD_DIM) + EPS) * g
        swapped = jnp.where(first, pltpu.roll(y, LANES - 16, 1), pltpu.roll(y, 16, 1))
        return y * cos + swapped * sin

    q = q_ref[0]
    for j in range(ATTN_WIDTH // LANES):
        yt = (norm_rope(q[:, j * LANES:(j + 1) * LANES], qg_ref[...]) * q_scale).T
        qt_ref[0, 2 * j] = yt[:HEAD_DIM].astype(BF16)
        qt_ref[0, 2 * j + 1] = yt[HEAD_DIM:].astype(BF16)
    kv = kv_ref[0]
    y = norm_rope(kv[:, :LANES], kg_ref[...])
    kh_ref[0, 0] = y[:, :HEAD_DIM].astype(BF16)
    kh_ref[0, 1] = y[:, HEAD_DIM:].astype(BF16)
    vt = kv[:, LANES:].T.astype(BF16)
    ones = jnp.ones((VT_ROWS - HEAD_DIM, ATT_TK), BF16)
    for g in range(ATTN_KV_HEADS):
        for cidx in range(TM // ATT_TK):
            vt_ref[0, g, cidx, 0:HEAD_DIM, :] = vt[g * HEAD_DIM:(g + 1) * HEAD_DIM,
                                                   cidx * ATT_TK:(cidx + 1) * ATT_TK]
            vt_ref[0, g, cidx, HEAD_DIM:VT_ROWS, :] = ones


def _attn_prep(q, kv, cos_t, sin_t, qg, kg, bd, q_scale):
    nb, s, _ = q.shape
    nt = s // TM
    tab = pl.BlockSpec((TM, LANES), lambda b, t: (t, 0))
    row = pl.BlockSpec((1, LANES), lambda b, t: (0, 0))
    per = TM // ATT_TK
    return pl.pallas_call(
        functools.partial(_attn_prep_kernel, q_scale=q_scale),
        grid=(nb, nt),
        in_specs=[pl.BlockSpec((1, TM, ATTN_WIDTH), lambda b, t: (b, t, 0)),
                  pl.BlockSpec((1, TM, 2 * LANES), lambda b, t: (b, t, 0)),
                  tab, tab, row, row,
                  pl.BlockSpec((LANES, LANES), lambda b, t: (0, 0))],
        out_specs=[pl.BlockSpec((1, ATTN_Q_HEADS, HEAD_DIM, TM), lambda b, t: (b, 0, 0, t)),
                   pl.BlockSpec((1, ATTN_KV_HEADS, TM, HEAD_DIM), lambda b, t: (b, 0, t, 0)),
                   pl.BlockSpec((1, ATTN_KV_HEADS, per, VT_ROWS, ATT_TK), lambda b, t: (b, 0, t, 0, 0))],
        out_shape=[jax.ShapeDtypeStruct((nb, ATTN_Q_HEADS, HEAD_DIM, s), BF16),
                   jax.ShapeDtypeStruct((nb, ATTN_KV_HEADS, s, HEAD_DIM), BF16),
                   jax.ShapeDtypeStruct((nb, ATTN_KV_HEADS, s // ATT_TK, VT_ROWS, ATT_TK), BF16)],
        compiler_params=_cparams(("parallel", "parallel")),
        name="attn_prep",
    )(q, kv, cos_t, sin_t, qg, kg, bd)


def _attn_kernel(qt_ref, k_ref, vt_ref, o_ref, m_ref, acc_ref, s_ref, *, nctq, lc, s_len):
    qi = pl.program_id(2)
    m_ref[...] = jnp.full(m_ref.shape, -jnp.inf, F32)
    acc_ref[...] = jnp.zeros(acc_ref.shape, F32)
    nk = jnp.where(qi < nctq, lc // ATT_TK, s_len // ATT_TK)

    def score(j, slot):
        ks = pl.multiple_of(j * ATT_TK, ATT_TK)
        k = k_ref[0, 0, pl.ds(ks, ATT_TK), :]
        for h in range(ATTN_GROUP):
            s_ref[slot, h] = _dot(k, qt_ref[0, h])

    def step(j, slot):
        score(jnp.minimum(j + 1, nk - 1), 1 - slot)
        vt = vt_ref[0, 0, j]
        for h in range(ATTN_GROUP):
            s = s_ref[slot, h]
            m_old = m_ref[h]
            m_new = jnp.maximum(m_old, jnp.max(s, axis=0, keepdims=True))
            alpha = jnp.exp2(m_old - m_new)
            p = jnp.exp2(s - m_new)
            acc_ref[h] = alpha * acc_ref[h] + _dot(vt, p.astype(BF16))
            m_ref[h] = m_new

    def pair(i, carry):
        step(2 * i, 0)
        step(2 * i + 1, 1)
        return carry

    score(0, 0)
    lax.fori_loop(0, nk // 2, pair, 0)

    @pl.when(nk % 2 == 1)
    def _():
        step(nk - 1, 0)

    for hp in range(ATTN_GROUP // 2):
        pair = jnp.concatenate(
            [acc_ref[h, 0:HEAD_DIM, :] / acc_ref[h, HEAD_DIM:HEAD_DIM + 1, :] for h in (2 * hp, 2 * hp + 1)], axis=0)
        o_ref[0, :, hp * LANES:(hp + 1) * LANES] = pair.T


def _attention(qt, kh, vt, lc):
    nb, _, _, s = qt.shape
    nq = s // ATT_TQ
    return pl.pallas_call(
        functools.partial(_attn_kernel, nctq=lc // ATT_TQ, lc=lc, s_len=s),
        grid=(nb, ATTN_KV_HEADS, nq),
        in_specs=[pl.BlockSpec((1, ATTN_GROUP, HEAD_DIM, ATT_TQ), lambda b, g, i: (b, g, 0, i)),
                  pl.BlockSpec((1, 1, s, HEAD_DIM), lambda b, g, i: (b, g, 0, 0)),
                  pl.BlockSpec((1, 1, s // ATT_TK, VT_ROWS, ATT_TK), lambda b, g, i: (b, g, 0, 0, 0))],
        out_specs=pl.BlockSpec((1, ATT_TQ, ATTN_GROUP * HEAD_DIM), lambda b, g, i: (b, i, g)),
        out_shape=jax.ShapeDtypeStruct((nb, s, ATTN_WIDTH), F32),
        scratch_shapes=[pltpu.VMEM((ATTN_GROUP, 1, ATT_TQ), F32),
                        pltpu.VMEM((ATTN_GROUP, VT_ROWS, ATT_TQ), F32),
                        pltpu.VMEM((2, ATTN_GROUP, ATT_TK, ATT_TQ), F32)],
        compiler_params=_cparams(("parallel", "parallel", "parallel")),
        name="attention",
    )(qt, kh, vt)


def _outproj_kernel(conv_ref, of_ref, ob_ref, z_ref, attn_ref, x_ref, mod_ref, gg_ref, bd_ref, w_ref,
                    n2_ref, wr_ref, br_ref, xo_ref, h2_ref, lg_ref):
    o = of_ref[0] + ob_ref[0]
    ss = _dot(o * o, bd_ref[...], precision=HIGHEST)
    gdn = o * lax.rsqrt(ss * (1.0 / GDN_DK) + EPS) * gg_ref[...] * _silu(z_ref[0])
    y = _dot(conv_ref[0].astype(BF16), w_ref[0:CONV_CH, :])
    y = y + _dot(gdn.astype(BF16), w_ref[CONV_CH:CONV_CH + GDN_WIDTH, :])
    y = y + _dot(attn_ref[0].astype(BF16), w_ref[CONV_CH + GDN_WIDTH:, :])
    x = x_ref[0] + mod_ref[0, 2:3, :] * y
    xo_ref[0] = x
    h2 = _rms(x, n2_ref[...]) * (1.0 + mod_ref[0, 4:5, :]) + mod_ref[0, 3:4, :]
    h2_ref[0] = h2.astype(BF16)
    lg_ref[0] = _dot(h2, wr_ref[...], precision=HIGHEST) + br_ref[...]


def _out_proj(conv, o_f, o_b, z, attn, x, mod, gdn_g, bd, w_out, n2, wr, br, nct):
    nb, s, d = x.shape
    nt = s // TM
    tok = lambda w: pl.BlockSpec((1, TM, w), lambda b, t: (b, t, 0))
    full = lambda a: pl.BlockSpec(a.shape, lambda b, t: (0,) * a.ndim)
    return pl.pallas_call(
        _outproj_kernel,
        grid=(nb, nt),
        in_specs=[tok(CONV_CH), tok(GDN_WIDTH), tok(GDN_WIDTH), tok(GDN_WIDTH), tok(ATTN_WIDTH), tok(d),
                  pl.BlockSpec((1, 6, d), lambda b, t: (jnp.where(t < nct, nb, b), 0, 0)),
                  full(gdn_g), full(bd), full(w_out), full(n2), full(wr), full(br)],
        out_specs=[tok(d), tok(d), tok(LANES)],
        out_shape=[jax.ShapeDtypeStruct((nb, s, d), F32), jax.ShapeDtypeStruct((nb, s, d), BF16),
                   jax.ShapeDtypeStruct((nb, s, LANES), F32)],
        compiler_params=_cparams(("parallel", "parallel")),
        name="out_proj",
    )(conv, o_f, o_b, z, attn, x, mod, gdn_g, bd, w_out, n2, wr, br)


def _expert_kernel(be_ref, nu_ref, x_ref, wg_ref, wu_ref, wd_ref, o_ref):
    i = pl.program_id(0)

    @pl.when(i < nu_ref[0])
    def _():
        x = x_ref[...]
        g = _dot(x, wg_ref[0, 0].astype(BF16))
        u = _dot(x, wu_ref[0, 0].astype(BF16))
        h = (_silu(g) * u).astype(BF16)
        o_ref[...] = _dot(h, wd_ref[0, 0].astype(BF16))

    @pl.when(i >= nu_ref[0])
    def _():
        o_ref[...] = jnp.zeros(o_ref.shape, F32)


def _experts(xs, blk_e, n_used, w_gate, w_up, w_down, layer):
    n_slots, d = xs.shape
    nblk = n_slots // MOE_BM
    de = w_gate.shape[-1]
    grid_spec = pltpu.PrefetchScalarGridSpec(
        num_scalar_prefetch=2,
        grid=(nblk,),
        in_specs=[pl.BlockSpec((MOE_BM, d), lambda i, be, nu: (jnp.minimum(i, nu[0] - 1), 0)),
                  pl.BlockSpec((1, 1, d, de), lambda i, be, nu: (layer, be[i], 0, 0)),
                  pl.BlockSpec((1, 1, d, de), lambda i, be, nu: (layer, be[i], 0, 0)),
                  pl.BlockSpec((1, 1, de, d), lambda i, be, nu: (layer, be[i], 0, 0))],
        out_specs=pl.BlockSpec((MOE_BM, d), lambda i, be, nu: (i, 0)),
    )
    return pl.pallas_call(
        _expert_kernel,
        grid_spec=grid_spec,
        out_shape=jax.ShapeDtypeStruct((n_slots, d), F32),
        compiler_params=_cparams(("arbitrary",)),
        name="experts",
    )(blk_e, n_used, xs, w_gate, w_up, w_down)


def _combine_kernel(x_ref, y_ref, mod_ref, fg_ref, o_ref, *, final):
    x = x_ref[0] + mod_ref[0, 5:6, :] * y_ref[0]
    if final:
        x = _rms(x, fg_ref[...])
    o_ref[0] = x


def _combine(x, y, mod, fg, nct, final):
    nb, s, d = x.shape
    nt = s // TM
    tok = pl.BlockSpec((1, TM, d), lambda b, t: (b, t, 0))
    return pl.pallas_call(
        functools.partial(_combine_kernel, final=final),
        grid=(nb, nt),
        in_specs=[tok, tok,
                  pl.BlockSpec((1, 6, d), lambda b, t: (jnp.where(t < nct, nb, b), 0, 0)),
                  pl.BlockSpec((1, d), lambda b, t: (0, 0))],
        out_specs=tok,
        out_shape=jax.ShapeDtypeStruct((nb, s, d), F32),
        compiler_params=_cparams(("parallel", "parallel")),
        name="combine",
    )(x, y, mod, fg)


def _route(logits, n_tok):
    lg = logits.reshape(n_tok, LANES)
    pg = jax.nn.softmax(lg[:, :N_GROUPS], axis=-1)
    grp = jnp.argmax(pg, axis=-1)
    pg_sel = jnp.take_along_axis(pg, grp[:, None], axis=-1)
    le = lg[:, N_GROUPS:N_GROUPS + N_EXPERTS].reshape(n_tok, N_GROUPS, EXPERTS_PER_GROUP)
    le = jnp.take_along_axis(le, grp[:, None, None], axis=1)[:, 0]
    top_p, top_i = lax.top_k(jax.nn.softmax(le, axis=-1), TOP_K)
    wts = pg_sel * top_p / jnp.sum(top_p, axis=-1, keepdims=True)
    eid = (grp[:, None] * EXPERTS_PER_GROUP + top_i).astype(jnp.int32)
    flat_e = eid.reshape(-1)
    onehot = (flat_e[:, None] == jnp.arange(N_EXPERTS, dtype=jnp.int32)[None, :]).astype(jnp.int32)
    rank = jnp.take_along_axis(jnp.cumsum(onehot, axis=0) - onehot, flat_e[:, None], axis=1)[:, 0]
    counts = jnp.sum(onehot, axis=0)
    pcounts = (counts + MOE_BM - 1) // MOE_BM * MOE_BM
    pends = jnp.cumsum(pcounts)
    slot = (pends - pcounts)[flat_e] + rank
    n_slots = (-(-n_tok * TOP_K // MOE_BM) + N_EXPERTS) * MOE_BM
    flat_t = jnp.repeat(jnp.arange(n_tok, dtype=jnp.int32), TOP_K)
    slot_tok = jnp.zeros((n_slots,), jnp.int32).at[slot].set(flat_t)
    nblk = n_slots // MOE_BM
    blk_e = jnp.minimum(jnp.searchsorted(pends, jnp.arange(nblk) * MOE_BM, side="right"),
                        N_EXPERTS - 1).astype(jnp.int32)
    n_used = (pends[-1] // MOE_BM).astype(jnp.int32).reshape(1)
    return slot.reshape(n_tok, TOP_K), wts, slot_tok, blk_e, n_used


def _rope_tables(lc, l):
    rows = l // GRID_W
    row = jnp.repeat(jnp.arange(rows), GRID_W).astype(F32)
    col = (jnp.arange(rows * GRID_W) % GRID_W).astype(F32)
    n_freq = HEAD_DIM // 4
    inv = ROPE_THETA ** (-jnp.arange(n_freq, dtype=F32) / n_freq)
    ar, ac = row[:, None] * inv, col[:, None] * inv
    cos = jnp.concatenate([jnp.cos(ar), jnp.cos(ar), jnp.cos(ac), jnp.cos(ac)], axis=-1)
    sin = jnp.concatenate([-jnp.sin(ar), jnp.sin(ar), -jnp.sin(ac), jnp.sin(ac)], axis=-1)
    cos = jnp.concatenate([jnp.ones((lc, HEAD_DIM), F32), cos], axis=0)
    sin = jnp.concatenate([jnp.zeros((lc, HEAD_DIM), F32), sin], axis=0)
    return jnp.tile(cos, (1, 2)), jnp.tile(sin, (1, 2))


def _block_diag_ones(n, blk):
    i = jnp.arange(n)
    return (i[:, None] // blk == i[None, :] // blk).astype(F32)


def kernel(x, c, ctx, c_ctx, mod_w, mod_b, norm1_g, norm2_g, w_in, conv_dw_w, conv_dw_b, conv_ln_g, conv_ln_b, conv_pw_w, conv_pw_b, gdn_conv_w, gdn_a_log, gdn_dt_bias, gdn_norm_g, attn_q_norm_g, attn_k_norm_g, w_out, router_group_w, router_group_b, router_expert_w, router_expert_b, expert_w_gate, expert_w_up, expert_w_down, final_norm_g):
    nb, l, d = x.shape
    lc = ctx.shape[1]
    depth = mod_w.shape[0]
    s = lc + l
    assert lc % TM == 0 and l % TM == 0 and lc % GDN_C == 0 and l % GDN_C == 0
    assert lc % ATT_TK == 0 and s % ATT_TK == 0 and lc % ATT_TQ == 0
    nct = lc // TM
    n_tok = nb * s

    nr = -(-(nb + 1) // 8) * 8
    cvec = jnp.zeros((nr, d), F32).at[:nb].set(c).at[nb].set(c_ctx)
    mod_all = _modulation(cvec, mod_w, mod_b).reshape(depth, nr, 6, d)

    cos_t, sin_t = _rope_tables(lc, l)
    bd64 = _block_diag_ones(LANES, HEAD_DIM)
    bd_gdn = _block_diag_ones(GDN_WIDTH, GDN_DK)
    q_scale = (HEAD_DIM ** -0.5) * math.log2(math.e)
    row = lambda v: v.reshape(1, -1).astype(F32)
    pad_lanes = lambda v: jnp.zeros((1, LANES), F32).at[0, :v.size].set(v.reshape(-1))

    xs = jnp.concatenate([ctx, x], axis=1)
    for layer in range(depth):
        mod = mod_all[layer]
        wi = w_in[layer]
        w_in_p = jnp.concatenate(
            [wi[:, :IN_A0 + 16], jnp.zeros((d, IN_PAD), F32), wi[:, IN_A0 + 16:]], axis=1).astype(BF16)
        qkv, ab, kv, gvgg, z, q = _in_proj(xs, mod, row(norm1_g[layer]), w_in_p, nct)

        dww = jnp.concatenate([conv_dw_w[layer], jnp.zeros((1, CONV_CH), F32)], axis=0)
        conv = _conformer(gvgg, dww, row(conv_dw_b[layer]), row(conv_ln_g[layer]), row(conv_ln_b[layer]),
                          conv_pw_w[layer].astype(BF16), row(conv_pw_b[layer]), nct)

        cw = jnp.concatenate([gdn_conv_w[layer], jnp.zeros((8 - SHORT_CONV, GDN_QKV), F32)], axis=0)
        gq, gk, gv, gb = _gdn_features(qkv, ab, cw, pad_lanes(gdn_a_log[layer]), pad_lanes(gdn_dt_bias[layer]), nct)
        o_f, o_b = _gdn_scan(gq, gk, gv, gb, lc // GDN_C)

        qt, kh, vt = _attn_prep(q, kv, cos_t, sin_t,
                                row(jnp.tile(attn_q_norm_g[layer], 2)), row(jnp.tile(attn_k_norm_g[layer], 2)),
                                bd64, q_scale)
        attn = _attention(qt, kh, vt, lc)

        wr = jnp.zeros((d, LANES), F32).at[:, :N_GROUPS].set(router_group_w[layer])
        wr = wr.at[:, N_GROUPS:N_GROUPS + N_EXPERTS].set(router_expert_w[layer])
        br = jnp.zeros((1, LANES), F32).at[0, :N_GROUPS].set(router_group_b[layer])
        br = br.at[0, N_GROUPS:N_GROUPS + N_EXPERTS].set(router_expert_b[layer])
        xs, h2, logits = _out_proj(conv, o_f, o_b, z, attn, xs, mod,
                                   row(jnp.tile(gdn_norm_g[layer], GDN_HEADS)), bd_gdn,
                                   w_out[layer].astype(BF16), row(norm2_g[layer]), wr, br, nct)

        slot, wts, slot_tok, blk_e, n_used = _route(logits, n_tok)
        xe = h2.reshape(n_tok, d)[slot_tok]
        ye = _experts(xe, blk_e, n_used, expert_w_gate, expert_w_up, expert_w_down, layer)
        y = ye[slot[:, 0]] * wts[:, 0:1] + ye[slot[:, 1]] * wts[:, 1:2]
        xs = _combine(xs, y.reshape(nb, s, d), mod, row(final_norm_g), nct, layer == depth - 1)
    return xs[:, lc:, :]
```

```python
import functools
import math

import jax
import jax.numpy as jnp
from jax import lax
from jax.experimental import pallas as pl
from jax.experimental.pallas import tpu as pltpu

F32 = jnp.float32
BF16 = jnp.bfloat16
HIGHEST = lax.Precision.HIGHEST

EPS = 1e-6
GRID_W = 64
CONV_CH = 256
CONV_WIDTH = 31
GDN_HEADS = 4
GDN_DK = 64
GDN_WIDTH = 256
GDN_QKV = 768
SHORT_CONV = 5
HEAD_DIM = 64
ATTN_Q_HEADS = 8
ATTN_KV_HEADS = 2
ATTN_GROUP = 4
ATTN_WIDTH = 512
ROPE_THETA = 10000.0
N_GROUPS = 4
EXPERTS_PER_GROUP = 8
N_EXPERTS = 32
TOP_K = 2
D_EXPERT = 512

LANES = 128
TM = 256
GDN_C = 128
CONV_HALO = 16
SHORT_HALO = 8
ATT_TQ = 256
ATT_TK = 256
VT_ROWS = 80
MOE_BM = 256
VMEM_LIMIT = 56 * 1024 * 1024

IN_A0 = GDN_QKV
IN_PAD = LANES - 16
C_QKV = (0, 768)
C_AB = (768, 896)
C_KV = (896, 1152)
C_GVGG = (1152, 1664)
C_Z = (1664, 1920)
C_Q = (1920, 2432)
IN_COLS = 2432


def _cparams(sem):
    return pltpu.CompilerParams(dimension_semantics=sem, vmem_limit_bytes=VMEM_LIMIT)


def _silu(x):
    return x * jax.nn.sigmoid(x)


def _dot(a, b, **kw):
    return jnp.dot(a, b, preferred_element_type=F32, **kw)


def _dot_nt(a, b):
    return lax.dot_general(a, b, (((1,), (1,)), ((), ())), preferred_element_type=F32)


def _dot_tn(a, b):
    return lax.dot_general(a, b, (((0,), (0,)), ((), ())), preferred_element_type=F32)


def _mod_kernel(c_ref, w_ref, b_ref, o_ref):
    c = c_ref[...]
    o_ref[0] = _dot(_silu(c), w_ref[0], precision=HIGHEST) + b_ref[0]


def _modulation(cvec, mod_w, mod_b):
    depth, d, n = mod_w.shape
    nr = cvec.shape[0]
    tn = 768
    return pl.pallas_call(
        _mod_kernel,
        grid=(depth, n // tn),
        in_specs=[pl.BlockSpec((nr, d), lambda l, j: (0, 0)),
                  pl.BlockSpec((1, d, tn), lambda l, j: (l, 0, j)),
                  pl.BlockSpec((1, 1, tn), lambda l, j: (l, 0, j))],
        out_specs=pl.BlockSpec((1, nr, tn), lambda l, j: (l, 0, j)),
        out_shape=jax.ShapeDtypeStruct((depth, nr, n), F32),
        compiler_params=_cparams(("parallel", "parallel")),
        name="modulation",
    )(cvec, mod_w, mod_b.reshape(depth, 1, n))


def _rms(x, g):
    return x * lax.rsqrt(jnp.mean(x * x, axis=-1, keepdims=True) + EPS) * g


def _inproj_kernel(x_ref, mod_ref, g_ref, w_ref, qkv_ref, ab_ref, kv_ref, gvgg_ref, z_ref, q_ref):
    x = x_ref[0]
    sh = mod_ref[0, 0:1, :]
    sc = mod_ref[0, 1:2, :]
    h = (_rms(x, g_ref[...]) * (1.0 + sc) + sh).astype(BF16)
    for ref, (c0, c1) in ((qkv_ref, C_QKV), (ab_ref, C_AB), (kv_ref, C_KV),
                          (gvgg_ref, C_GVGG), (z_ref, C_Z), (q_ref, C_Q)):
        ref[0] = _dot(h, w_ref[:, c0:c1])


def _in_proj(x, mod, g1, w_in_p, nct):
    nb, s, d = x.shape
    nt = s // TM
    widths = [c1 - c0 for c0, c1 in (C_QKV, C_AB, C_KV, C_GVGG, C_Z, C_Q)]
    tok = lambda w: pl.BlockSpec((1, TM, w), lambda b, t: (b, t, 0))
    return pl.pallas_call(
        _inproj_kernel,
        grid=(nb, nt),
        in_specs=[tok(d),
                  pl.BlockSpec((1, 6, d), lambda b, t: (jnp.where(t < nct, nb, b), 0, 0)),
                  pl.BlockSpec((1, d), lambda b, t: (0, 0)),
                  pl.BlockSpec((d, IN_COLS), lambda b, t: (0, 0))],
        out_specs=[tok(w) for w in widths],
        out_shape=[jax.ShapeDtypeStruct((nb, s, w), F32) for w in widths],
        compiler_params=_cparams(("parallel", "parallel")),
        name="in_proj",
    )(x, mod, g1, w_in_p)


def _halo_specs(width, halo, s):
    per = TM // halo
    cur = pl.BlockSpec((1, TM, width), lambda b, t: (b, t, 0))
    prev = pl.BlockSpec((1, halo, width), lambda b, t: (b, jnp.maximum(t * per - 1, 0), 0))
    nxt = pl.BlockSpec((1, halo, width), lambda b, t: (b, jnp.minimum((t + 1) * per, s // halo - 1), 0))
    return [cur, prev, nxt]


def _halo_flags(nct, nt):
    t = pl.program_id(1)
    prev_ok = jnp.logical_and(t != 0, t != nct)
    next_ok = jnp.logical_and(t != nct - 1, t != nt - 1)
    return prev_ok, next_ok


def _conformer_kernel(cur_ref, prev_ref, next_ref, dww_ref, dwb_ref, lng_ref, lnb_ref, pww_ref, pwb_ref,
                      o_ref, ext_ref, *, nct, nt):
    prev_ok, next_ok = _halo_flags(nct, nt)

    def glu(v):
        return v[:, :CONV_CH] * jax.nn.sigmoid(v[:, CONV_CH:])

    ext_ref[0:CONV_HALO, :] = jnp.where(prev_ok, glu(prev_ref[0]), 0.0)
    ext_ref[CONV_HALO:CONV_HALO + TM, :] = glu(cur_ref[0])
    ext_ref[CONV_HALO + TM:2 * CONV_HALO + TM, :] = jnp.where(next_ok, glu(next_ref[0]), 0.0)
    rb = 64
    off = CONV_HALO - CONV_WIDTH // 2
    for r in range(TM // rb):
        acc = jnp.zeros((rb, CONV_CH), F32) + dwb_ref[...]
        for j in range(CONV_WIDTH):
            acc = acc + ext_ref[pl.ds(r * rb + off + j, rb), :] * dww_ref[j:j + 1, :]
        mu = jnp.mean(acc, axis=-1, keepdims=True)
        xc = acc - mu
        y = xc * lax.rsqrt(jnp.mean(xc * xc, axis=-1, keepdims=True) + EPS) * lng_ref[...] + lnb_ref[...]
        h = _silu(y).astype(BF16)
        o_ref[0, r * rb:(r + 1) * rb, :] = _dot(h, pww_ref[...]) + pwb_ref[...]


def _conformer(gvgg, dww, dwb, lng, lnb, pww, pwb, nct):
    nb, s, _ = gvgg.shape
    nt = s // TM
    row = lambda w: pl.BlockSpec((1, w), lambda b, t: (0, 0))
    return pl.pallas_call(
        functools.partial(_conformer_kernel, nct=nct, nt=nt),
        grid=(nb, nt),
        in_specs=_halo_specs(2 * CONV_CH, CONV_HALO, s) + [
            pl.BlockSpec((CONV_WIDTH + 1, CONV_CH), lambda b, t: (0, 0)),
            row(CONV_CH), row(CONV_CH), row(CONV_CH),
            pl.BlockSpec((CONV_CH, CONV_CH), lambda b, t: (0, 0)),
            row(CONV_CH)],
        out_specs=pl.BlockSpec((1, TM, CONV_CH), lambda b, t: (b, t, 0)),
        out_shape=jax.ShapeDtypeStruct((nb, s, CONV_CH), F32),
        scratch_shapes=[pltpu.VMEM((TM + 2 * CONV_HALO, CONV_CH), F32)],
        compiler_params=_cparams(("parallel", "parallel")),
        name="conformer",
    )(gvgg, gvgg, gvgg, dww, dwb, lng, lnb, pww, pwb)


def _gdn_feat_kernel(cur_ref, prev_ref, next_ref, cw_ref, ab_ref, alog_ref, dtb_ref,
                     q_ref, k_ref, v_ref, gb_ref, ext_ref, *, nct, nt):
    prev_ok, next_ok = _halo_flags(nct, nt)
    ext_ref[0:SHORT_HALO, :] = jnp.where(prev_ok, prev_ref[0], 0.0)
    ext_ref[SHORT_HALO:SHORT_HALO + TM, :] = cur_ref[0]
    ext_ref[SHORT_HALO + TM:2 * SHORT_HALO + TM, :] = jnp.where(next_ok, next_ref[0], 0.0)
    rb = 32
    off = SHORT_HALO - SHORT_CONV // 2
    for r in range(TM // rb):
        acc = jnp.zeros((rb, GDN_QKV), F32)
        for j in range(SHORT_CONV):
            acc = acc + ext_ref[pl.ds(r * rb + off + j, rb), :] * cw_ref[j:j + 1, :]
        y = _silu(acc)
        rows = slice(r * rb, (r + 1) * rb)
        for h in range(GDN_HEADS):
            qh = y[:, h * GDN_DK:(h + 1) * GDN_DK]
            kh = y[:, GDN_WIDTH + h * GDN_DK:GDN_WIDTH + (h + 1) * GDN_DK]
            vh = y[:, 2 * GDN_WIDTH + h * GDN_DK:2 * GDN_WIDTH + (h + 1) * GDN_DK]
            qn = qh * lax.rsqrt(jnp.sum(qh * qh, axis=-1, keepdims=True) + EPS) * (GDN_DK ** -0.5)
            kn = kh * lax.rsqrt(jnp.sum(kh * kh, axis=-1, keepdims=True) + EPS)
            q_ref[0, h, rows, :] = qn
            k_ref[0, h, rows, :] = kn
            v_ref[0, h, rows, :] = vh
    ab = ab_ref[0]
    lane = lax.broadcasted_iota(jnp.int32, ab.shape, 1)
    xa = ab + dtb_ref[...]
    softplus = jnp.maximum(xa, 0.0) + jnp.log(1.0 + jnp.exp(-jnp.abs(xa)))
    g = -jnp.exp(alog_ref[...]) * softplus
    beta = jax.nn.sigmoid(ab)
    gb_ref[0] = jnp.where(lane < 2 * GDN_HEADS, g, jnp.where(lane < 4 * GDN_HEADS, beta, 0.0))


def _gdn_features(qkv, ab, cw, alog, dtb, nct):
    nb, s, _ = qkv.shape
    nt = s // TM
    head = pl.BlockSpec((1, GDN_HEADS, TM, GDN_DK), lambda b, t: (b, 0, t, 0))
    hshape = jax.ShapeDtypeStruct((nb, GDN_HEADS, s, GDN_DK), F32)
    return pl.pallas_call(
        functools.partial(_gdn_feat_kernel, nct=nct, nt=nt),
        grid=(nb, nt),
        in_specs=_halo_specs(GDN_QKV, SHORT_HALO, s) + [
            pl.BlockSpec((8, GDN_QKV), lambda b, t: (0, 0)),
            pl.BlockSpec((1, TM, LANES), lambda b, t: (b, t, 0)),
            pl.BlockSpec((1, LANES), lambda b, t: (0, 0)),
            pl.BlockSpec((1, LANES), lambda b, t: (0, 0))],
        out_specs=[head, head, head, pl.BlockSpec((1, TM, LANES), lambda b, t: (b, t, 0))],
        out_shape=[hshape, hshape, hshape, jax.ShapeDtypeStruct((nb, s, LANES), F32)],
        scratch_shapes=[pltpu.VMEM((TM + 2 * SHORT_HALO, GDN_QKV), F32)],
        compiler_params=_cparams(("parallel", "parallel")),
        name="gdn_features",
    )(qkv, qkv, qkv, cw, ab, alog, dtb)


def _gdn_scan_kernel(qf_ref, kf_ref, vf_ref, gf_ref, qb_ref, kb_ref, vb_ref, gbk_ref,
                     of_ref, ob_ref, s_ref):
    n = pl.program_id(1)

    @pl.when(n == 0)
    def _():
        s_ref[...] = jnp.zeros(s_ref.shape, F32)

    c = GDN_C
    row = lax.broadcasted_iota(jnp.int32, (c, c), 0)
    col = lax.broadcasted_iota(jnp.int32, (c, c), 1)
    eye = (row == col).astype(F32)
    blk = {2 ** e: (row >> e) == (col >> e) for e in range(1, int(math.log2(c)) + 1)}

    chains = []
    for d, (q_ref, k_ref, v_ref, g_ref, o_ref) in enumerate(
            ((qf_ref, kf_ref, vf_ref, gf_ref, of_ref), (qb_ref, kb_ref, vb_ref, gbk_ref, ob_ref))):
        incl = (row >= col) if d == 0 else (row <= col)
        strict = (row > col) if d == 0 else (row < col)
        gb = g_ref[0]
        gam_all = _dot(incl.astype(F32), gb, precision=HIGHEST)
        gam_t = gam_all.T
        last_row = c - 1 if d == 0 else 0
        for h in range(GDN_HEADS):
            cc = d * GDN_HEADS + h
            gam = gam_all[:, cc:cc + 1]
            beta = gb[:, 2 * GDN_HEADS + cc:2 * GDN_HEADS + cc + 1]
            kh = k_ref[0, h]
            chains.append(dict(
                d=d, h=h, o_ref=o_ref, strict=strict, gam=gam, beta=beta, kh=kh, qh=q_ref[0, h], vh=v_ref[0, h],
                last=gam_all[last_row:last_row + 1, cc:cc + 1],
                decay=jnp.exp(jnp.where(incl, gam - gam_t[cc:cc + 1, :], -1e30)),
                kbeta=kh * beta))
    for ch in chains:
        ch["a"] = jnp.where(ch["strict"], _dot_nt(ch["kbeta"], ch["kh"]) * ch["decay"], 0.0)
    for ch in chains:
        ch["t"] = eye - jnp.where(blk[2], ch["a"], 0.0)
    bsz = 2
    while bsz < c:
        off = jnp.logical_and(blk[2 * bsz], jnp.logical_not(blk[bsz]))
        for ch in chains:
            ch["tl"] = _dot(ch["t"], jnp.where(off, ch["a"], 0.0))
        for ch in chains:
            ch["t"] = ch["t"] - _dot(ch["tl"], ch["t"])
        bsz *= 2
    for ch in chains:
        rhs = jnp.concatenate([ch["vh"] * ch["beta"], ch["kbeta"] * jnp.exp(ch["gam"])], axis=1)
        ch["uw"] = _dot(ch["t"], rhs)
    for ch in chains:
        ch["pm"] = _dot_nt(ch["qh"], ch["kh"]) * ch["decay"]
    for ch in chains:
        ch["s"] = s_ref[ch["d"], ch["h"]]
        ch["v_new"] = ch["uw"][:, :GDN_DK] - _dot(ch["uw"][:, GDN_DK:], ch["s"])
    for ch in chains:
        h = ch["h"]
        qg = ch["qh"] * jnp.exp(ch["gam"])
        ch["o_ref"][0, :, h * GDN_DK:(h + 1) * GDN_DK] = _dot(qg, ch["s"]) + _dot(ch["pm"], ch["v_new"])
    for ch in chains:
        kd = ch["kh"] * jnp.exp(ch["last"] - ch["gam"])
        s_ref[ch["d"], ch["h"]] = ch["s"] * jnp.exp(ch["last"]) + _dot_tn(kd, ch["v_new"])


def _gdn_scan(q, k, v, gb, ncc):
    nb, _, s, _ = q.shape
    nc = s // GDN_C

    def bwd(n):
        return jnp.where(n < ncc, ncc - 1 - n, nc + ncc - 1 - n)

    hf = pl.BlockSpec((1, GDN_HEADS, GDN_C, GDN_DK), lambda b, n: (b, 0, n, 0))
    hb = pl.BlockSpec((1, GDN_HEADS, GDN_C, GDN_DK), lambda b, n: (b, 0, bwd(n), 0))
    gf = pl.BlockSpec((1, GDN_C, LANES), lambda b, n: (b, n, 0))
    gk = pl.BlockSpec((1, GDN_C, LANES), lambda b, n: (b, bwd(n), 0))
    of = pl.BlockSpec((1, GDN_C, GDN_WIDTH), lambda b, n: (b, n, 0))
    ob = pl.BlockSpec((1, GDN_C, GDN_WIDTH), lambda b, n: (b, bwd(n), 0))
    oshape = jax.ShapeDtypeStruct((nb, s, GDN_WIDTH), F32)
    return pl.pallas_call(
        _gdn_scan_kernel,
        grid=(nb, nc),
        in_specs=[hf, hf, hf, gf, hb, hb, hb, gk],
        out_specs=[of, ob],
        out_shape=[oshape, oshape],
        scratch_shapes=[pltpu.VMEM((2, GDN_HEADS, GDN_DK, GDN_DK), F32)],
        compiler_params=_cparams(("parallel", "arbitrary")),
        name="gdn_scan",
    )(q, k, v, gb, q, k, v, gb)


def _attn_prep_kernel(q_ref, kv_ref, cos_ref, sin_ref, qg_ref, kg_ref, bd_ref,
                      qt_ref, kh_ref, vt_ref, *, q_scale):
    cos = cos_ref[...]
    sin = sin_ref[...]
    bd = bd_ref[...]
    lane = lax.broadcasted_iota(jnp.int32, cos.shape, 1)
    first = (lane % 32) < 16

    def norm_rope(x, g):
        ss = _dot(x * x, bd, precision=HIGHEST)
        y = x * lax.rsqrt(ss * (1.0 / HEAD_DIM) + EPS) * g
        swapped = jnp.where(first, pltpu.roll(y, LANES - 16, 1), pltpu.roll(y, 16, 1))
        return y * cos + swapped * sin

    q = q_ref[0]
    for j in range(ATTN_WIDTH // LANES):
        yt = (norm_rope(q[:, j * LANES:(j + 1) * LANES], qg_ref[...]) * q_scale).T
        qt_ref[0, 2 * j] = yt[:HEAD_DIM].astype(BF16)
        qt_ref[0, 2 * j + 1] = yt[HEAD_DIM:].astype(BF16)
    kv = kv_ref[0]
    y = norm_rope(kv[:, :LANES], kg_ref[...])
    kh_ref[0, 0] = y[:, :HEAD_DIM].astype(BF16)
    kh_ref[0, 1] = y[:, HEAD_DIM:].astype(BF16)
    vt = kv[:, LANES:].T.astype(BF16)
    ones = jnp.ones((VT_ROWS - HEAD_DIM, ATT_TK), BF16)
    for g in range(ATTN_KV_HEADS):
        for cidx in range(TM // ATT_TK):
            vt_ref[0, g, cidx, 0:HEAD_DIM, :] = vt[g * HEAD_DIM:(g + 1) * HEAD_DIM,
                                                   cidx * ATT_TK:(cidx + 1) * ATT_TK]
            vt_ref[0, g, cidx, HEAD_DIM:VT_ROWS, :] = ones


def _attn_prep(q, kv, cos_t, sin_t, qg, kg, bd, q_scale):
    nb, s, _ = q.shape
    nt = s // TM
    tab = pl.BlockSpec((TM, LANES), lambda b, t: (t, 0))
    row = pl.BlockSpec((1, LANES), lambda b, t: (0, 0))
    per = TM // ATT_TK
    return pl.pallas_call(
        functools.partial(_attn_prep_kernel, q_scale=q_scale),
        grid=(nb, nt),
        in_specs=[pl.BlockSpec((1, TM, ATTN_WIDTH), lambda b, t: (b, t, 0)),
                  pl.BlockSpec((1, TM, 2 * LANES), lambda b, t: (b, t, 0)),
                  tab, tab, row, row,
                  pl.BlockSpec((LANES, LANES), lambda b, t: (0, 0))],
        out_specs=[pl.BlockSpec((1, ATTN_Q_HEADS, HEAD_DIM, TM), lambda b, t: (b, 0, 0, t)),
                   pl.BlockSpec((1, ATTN_KV_HEADS, TM, HEAD_DIM), lambda b, t: (b, 0, t, 0)),
                   pl.BlockSpec((1, ATTN_KV_HEADS, per, VT_ROWS, ATT_TK), lambda b, t: (b, 0, t, 0, 0))],
        out_shape=[jax.ShapeDtypeStruct((nb, ATTN_Q_HEADS, HEAD_DIM, s), BF16),
                   jax.ShapeDtypeStruct((nb, ATTN_KV_HEADS, s, HEAD_DIM), BF16),
                   jax.ShapeDtypeStruct((nb, ATTN_KV_HEADS, s // ATT_TK, VT_ROWS, ATT_TK), BF16)],
        compiler_params=_cparams(("parallel", "parallel")),
        name="attn_prep",
    )(q, kv, cos_t, sin_t, qg, kg, bd)


def _attn_kernel(qt_ref, k_ref, vt_ref, o_ref, m_ref, acc_ref, s_ref, *, nctq, lc, s_len):
    qi = pl.program_id(2)
    m_ref[...] = jnp.full(m_ref.shape, -jnp.inf, F32)
    acc_ref[...] = jnp.zeros(acc_ref.shape, F32)
    nk = jnp.where(qi < nctq, lc // ATT_TK, s_len // ATT_TK)

    def score(j, slot):
        ks = pl.multiple_of(j * ATT_TK, ATT_TK)
        k = k_ref[0, 0, pl.ds(ks, ATT_TK), :]
        for h in range(ATTN_GROUP):
            s_ref[slot, h] = _dot(k, qt_ref[0, h])

    def step(j, slot):
        score(jnp.minimum(j + 1, nk - 1), 1 - slot)
        vt = vt_ref[0, 0, j]
        for h in range(ATTN_GROUP):
            s = s_ref[slot, h]
            m_old = m_ref[h]
            m_new = jnp.maximum(m_old, jnp.max(s, axis=0, keepdims=True))
            alpha = jnp.exp2(m_old - m_new)
            p = jnp.exp2(s - m_new)
            acc_ref[h] = alpha * acc_ref[h] + _dot(vt, p.astype(BF16))
            m_ref[h] = m_new

    def pair(i, carry):
        step(2 * i, 0)
        step(2 * i + 1, 1)
        return carry

    score(0, 0)
    lax.fori_loop(0, nk // 2, pair, 0)

    @pl.when(nk % 2 == 1)
    def _():
        step(nk - 1, 0)

    for hp in range(ATTN_GROUP // 2):
        pair = jnp.concatenate(
            [acc_ref[h, 0:HEAD_DIM, :] / acc_ref[h, HEAD_DIM:HEAD_DIM + 1, :] for h in (2 * hp, 2 * hp + 1)], axis=0)
        o_ref[0, :, hp * LANES:(hp + 1) * LANES] = pair.T


def _attention(qt, kh, vt, lc):
    nb, _, _, s = qt.shape
    nq = s // ATT_TQ
    return pl.pallas_call(
        functools.partial(_attn_kernel, nctq=lc // ATT_TQ, lc=lc, s_len=s),
        grid=(nb, ATTN_KV_HEADS, nq),
        in_specs=[pl.BlockSpec((1, ATTN_GROUP, HEAD_DIM, ATT_TQ), lambda b, g, i: (b, g, 0, i)),
                  pl.BlockSpec((1, 1, s, HEAD_DIM), lambda b, g, i: (b, g, 0, 0)),
                  pl.BlockSpec((1, 1, s // ATT_TK, VT_ROWS, ATT_TK), lambda b, g, i: (b, g, 0, 0, 0))],
        out_specs=pl.BlockSpec((1, ATT_TQ, ATTN_GROUP * HEAD_DIM), lambda b, g, i: (b, i, g)),
        out_shape=jax.ShapeDtypeStruct((nb, s, ATTN_WIDTH), F32),
        scratch_shapes=[pltpu.VMEM((ATTN_GROUP, 1, ATT_TQ), F32),
                        pltpu.VMEM((ATTN_GROUP, VT_ROWS, ATT_TQ), F32),
                        pltpu.VMEM((2, ATTN_GROUP, ATT_TK, ATT_TQ), F32)],
        compiler_params=_cparams(("parallel", "parallel", "parallel")),
        name="attention",
    )(qt, kh, vt)


def _outproj_kernel(conv_ref, of_ref, ob_ref, z_ref, attn_ref, x_ref, mod_ref, gg_ref, bd_ref, w_ref,
                    n2_ref, wr_ref, br_ref, xo_ref, h2_ref, ri_ref, rw_ref):
    o = of_ref[0] + ob_ref[0]
    ss = _dot(o * o, bd_ref[...], precision=HIGHEST)
    gdn = o * lax.rsqrt(ss * (1.0 / GDN_DK) + EPS) * gg_ref[...] * _silu(z_ref[0])
    y = _dot(conv_ref[0].astype(BF16), w_ref[0:CONV_CH, :])
    y = y + _dot(gdn.astype(BF16), w_ref[CONV_CH:CONV_CH + GDN_WIDTH, :])
    y = y + _dot(attn_ref[0].astype(BF16), w_ref[CONV_CH + GDN_WIDTH:, :])
    x = x_ref[0] + mod_ref[0, 2:3, :] * y
    xo_ref[0] = x
    h2 = _rms(x, n2_ref[...]) * (1.0 + mod_ref[0, 4:5, :]) + mod_ref[0, 3:4, :]
    h2_ref[0] = h2.astype(BF16)
    lg = _dot(h2, wr_ref[...], precision=HIGHEST) + br_ref[...]
    lane = lax.broadcasted_iota(jnp.int32, lg.shape, 1)
    lane_f = lane.astype(F32)

    def first_max(v):
        mx = jnp.max(v, axis=-1, keepdims=True)
        return mx, jnp.min(jnp.where(v == mx, lane_f, float(LANES)), axis=-1, keepdims=True)

    lgg = jnp.where(lane < N_GROUPS, lg, -jnp.inf)
    gmax, grp = first_max(lgg)
    pg_sel = 1.0 / jnp.sum(jnp.exp(lgg - gmax), axis=-1, keepdims=True)
    in_grp = jnp.logical_and(lane >= N_GROUPS, ((lane - N_GROUPS) >> 3).astype(F32) == grp)
    le = jnp.where(in_grp, lg, -jnp.inf)
    e1max, i1 = first_max(le)
    e2max, i2 = first_max(jnp.where(lane_f == i1, -jnp.inf, le))
    r = jnp.exp(e2max - e1max)
    w1 = pg_sel / (1.0 + r)
    ri_ref[0] = jnp.where(lane == 0, i1, jnp.where(lane == 1, i2, float(N_GROUPS))).astype(jnp.int32) - N_GROUPS
    rw_ref[0] = jnp.where(lane == 0, w1, jnp.where(lane == 1, w1 * r, 0.0))


def _out_proj(conv, o_f, o_b, z, attn, x, mod, gdn_g, bd, w_out, n2, wr, br, nct):
    nb, s, d = x.shape
    nt = s // TM
    tok = lambda w: pl.BlockSpec((1, TM, w), lambda b, t: (b, t, 0))
    full = lambda a: pl.BlockSpec(a.shape, lambda b, t: (0,) * a.ndim)
    return pl.pallas_call(
        _outproj_kernel,
        grid=(nb, nt),
        in_specs=[tok(CONV_CH), tok(GDN_WIDTH), tok(GDN_WIDTH), tok(GDN_WIDTH), tok(ATTN_WIDTH), tok(d),
                  pl.BlockSpec((1, 6, d), lambda b, t: (jnp.where(t < nct, nb, b), 0, 0)),
                  full(gdn_g), full(bd), full(w_out), full(n2), full(wr), full(br)],
        out_specs=[tok(d), tok(d), tok(LANES), tok(LANES)],
        out_shape=[jax.ShapeDtypeStruct((nb, s, d), F32), jax.ShapeDtypeStruct((nb, s, d), BF16),
                   jax.ShapeDtypeStruct((nb, s, LANES), jnp.int32), jax.ShapeDtypeStruct((nb, s, LANES), F32)],
        compiler_params=_cparams(("parallel", "parallel")),
        name="out_proj",
    )(conv, o_f, o_b, z, attn, x, mod, gdn_g, bd, w_out, n2, wr, br)


def _expert_kernel(be_ref, nu_ref, x_ref, wg_ref, wu_ref, wd_ref, o_ref):
    i = pl.program_id(0)

    @pl.when(i < nu_ref[0])
    def _():
        x = x_ref[...]
        g = _dot(x, wg_ref[0, 0].astype(BF16))
        u = _dot(x, wu_ref[0, 0].astype(BF16))
        h = (_silu(g) * u).astype(BF16)
        o_ref[...] = _dot(h, wd_ref[0, 0].astype(BF16))

    @pl.when(i >= nu_ref[0])
    def _():
        o_ref[...] = jnp.zeros(o_ref.shape, F32)


def _experts(xs, blk_e, n_used, w_gate, w_up, w_down, layer):
    n_slots, d = xs.shape
    nblk = n_slots // MOE_BM
    de = w_gate.shape[-1]
    grid_spec = pltpu.PrefetchScalarGridSpec(
        num_scalar_prefetch=2,
        grid=(nblk,),
        in_specs=[pl.BlockSpec((MOE_BM, d), lambda i, be, nu: (jnp.minimum(i, nu[0] - 1), 0)),
                  pl.BlockSpec((1, 1, d, de), lambda i, be, nu: (layer, be[i], 0, 0)),
                  pl.BlockSpec((1, 1, d, de), lambda i, be, nu: (layer, be[i], 0, 0)),
                  pl.BlockSpec((1, 1, de, d), lambda i, be, nu: (layer, be[i], 0, 0))],
        out_specs=pl.BlockSpec((MOE_BM, d), lambda i, be, nu: (i, 0)),
    )
    return pl.pallas_call(
        _expert_kernel,
        grid_spec=grid_spec,
        out_shape=jax.ShapeDtypeStruct((n_slots, d), F32),
        compiler_params=_cparams(("arbitrary",)),
        name="experts",
    )(blk_e, n_used, xs, w_gate, w_up, w_down)


def _combine_kernel(x_ref, y_ref, mod_ref, fg_ref, o_ref, *, final):
    x = x_ref[0] + mod_ref[0, 5:6, :] * y_ref[0]
    if final:
        x = _rms(x, fg_ref[...])
    o_ref[0] = x


def _combine(x, y, mod, fg, nct, final):
    nb, s, d = x.shape
    nt = s // TM
    tok = pl.BlockSpec((1, TM, d), lambda b, t: (b, t, 0))
    return pl.pallas_call(
        functools.partial(_combine_kernel, final=final),
        grid=(nb, nt),
        in_specs=[tok, tok,
                  pl.BlockSpec((1, 6, d), lambda b, t: (jnp.where(t < nct, nb, b), 0, 0)),
                  pl.BlockSpec((1, d), lambda b, t: (0, 0))],
        out_specs=tok,
        out_shape=jax.ShapeDtypeStruct((nb, s, d), F32),
        compiler_params=_cparams(("parallel", "parallel")),
        name="combine",
    )(x, y, mod, fg)


def _rank_kernel(ri_ref, rank_ref, cnt_ref, carry_ref):
    @pl.when(pl.program_id(0) == 0)
    def _():
        carry_ref[...] = jnp.zeros(carry_ref.shape, F32)

    ri = ri_ref[0]
    lane = lax.broadcasted_iota(jnp.int32, ri.shape, 1)
    oh1 = lane == ri[:, 0:1]
    oh2 = lane == ri[:, 1:2]
    oh = jnp.logical_or(oh1, oh2).astype(F32)
    row = lax.broadcasted_iota(jnp.int32, (TM, TM), 0)
    col = lax.broadcasted_iota(jnp.int32, (TM, TM), 1)
    before = carry_ref[0:1, :] + _dot((row > col).astype(BF16), oh.astype(BF16))
    r1 = jnp.sum(jnp.where(oh1, before, 0.0), axis=-1, keepdims=True)
    r2 = jnp.sum(jnp.where(oh2, before, 0.0), axis=-1, keepdims=True)
    rank_ref[0] = jnp.where(lane == 0, r1, jnp.where(lane == 1, r2, 0.0)).astype(jnp.int32)
    carry_ref[...] = carry_ref[...] + jnp.sum(oh, axis=0, keepdims=True)
    cnt_ref[...] = carry_ref[...]


def _slot_kernel(ri_ref, rank_ref, ps_ref, slot_ref):
    ri = ri_ref[0]
    rk = rank_ref[0]
    lane = lax.broadcasted_iota(jnp.int32, ri.shape, 1)
    ps = ps_ref[...]
    s1 = jnp.sum(jnp.where(lane == ri[:, 0:1], ps, 0.0), axis=-1, keepdims=True).astype(jnp.int32) + rk[:, 0:1]
    s2 = jnp.sum(jnp.where(lane == ri[:, 1:2], ps, 0.0), axis=-1, keepdims=True).astype(jnp.int32) + rk[:, 1:2]
    slot_ref[0] = jnp.where(lane == 0, s1, jnp.where(lane == 1, s2, 0))


def _route(ri, rw, n_tok):
    ntile = n_tok // TM
    ri = ri.reshape(ntile, TM, LANES)
    tile = pl.BlockSpec((1, TM, LANES), lambda i: (i, 0, 0))
    rank, cnt = pl.pallas_call(
        _rank_kernel,
        grid=(ntile,),
        in_specs=[tile],
        out_specs=[tile, pl.BlockSpec((8, LANES), lambda i: (0, 0))],
        out_shape=[jax.ShapeDtypeStruct((ntile, TM, LANES), jnp.int32), jax.ShapeDtypeStruct((8, LANES), F32)],
        scratch_shapes=[pltpu.VMEM((8, LANES), F32)],
        compiler_params=_cparams(("arbitrary",)),
        name="moe_rank",
    )(ri)
    counts = cnt[0, :N_EXPERTS].astype(jnp.int32)
    pcounts = (counts + MOE_BM - 1) // MOE_BM * MOE_BM
    pends = jnp.cumsum(pcounts)
    pstarts = jnp.zeros((1, LANES), F32).at[0, :N_EXPERTS].set((pends - pcounts).astype(F32))
    slot = pl.pallas_call(
        _slot_kernel,
        grid=(ntile,),
        in_specs=[tile, tile, pl.BlockSpec((1, LANES), lambda i: (0, 0))],
        out_specs=tile,
        out_shape=jax.ShapeDtypeStruct((ntile, TM, LANES), jnp.int32),
        compiler_params=_cparams(("parallel",)),
        name="moe_slot",
    )(ri, rank, pstarts)
    slot = slot.reshape(n_tok, LANES)[:, :TOP_K]
    wts = rw.reshape(n_tok, LANES)[:, :TOP_K]
    n_slots = (-(-n_tok * TOP_K // MOE_BM) + N_EXPERTS) * MOE_BM
    flat_t = jnp.repeat(jnp.arange(n_tok, dtype=jnp.int32), TOP_K)
    slot_tok = jnp.zeros((n_slots,), jnp.int32).at[slot.reshape(-1)].set(flat_t)
    nblk = n_slots // MOE_BM
    blk_e = jnp.minimum(jnp.searchsorted(pends, jnp.arange(nblk) * MOE_BM, side="right"),
                        N_EXPERTS - 1).astype(jnp.int32)
    n_used = (pends[-1] // MOE_BM).astype(jnp.int32).reshape(1)
    return slot, wts, slot_tok, blk_e, n_used


def _rope_tables(lc, l):
    rows = l // GRID_W
    row = jnp.repeat(jnp.arange(rows), GRID_W).astype(F32)
    col = (jnp.arange(rows * GRID_W) % GRID_W).astype(F32)
    n_freq = HEAD_DIM // 4
    inv = ROPE_THETA ** (-jnp.arange(n_freq, dtype=F32) / n_freq)
    ar, ac = row[:, None] * inv, col[:, None] * inv
    cos = jnp.concatenate([jnp.cos(ar), jnp.cos(ar), jnp.cos(ac), jnp.cos(ac)], axis=-1)
    sin = jnp.concatenate([-jnp.sin(ar), jnp.sin(ar), -jnp.sin(ac), jnp.sin(ac)], axis=-1)
    cos = jnp.concatenate([jnp.ones((lc, HEAD_DIM), F32), cos], axis=0)
    sin = jnp.concatenate([jnp.zeros((lc, HEAD_DIM), F32), sin], axis=0)
    return jnp.tile(cos, (1, 2)), jnp.tile(sin, (1, 2))


def _block_diag_ones(n, blk):
    i = jnp.arange(n)
    return (i[:, None] // blk == i[None, :] // blk).astype(F32)


def kernel(x, c, ctx, c_ctx, mod_w, mod_b, norm1_g, norm2_g, w_in, conv_dw_w, conv_dw_b, conv_ln_g, conv_ln_b, conv_pw_w, conv_pw_b, gdn_conv_w, gdn_a_log, gdn_dt_bias, gdn_norm_g, attn_q_norm_g, attn_k_norm_g, w_out, router_group_w, router_group_b, router_expert_w, router_expert_b, expert_w_gate, expert_w_up, expert_w_down, final_norm_g):
    nb, l, d = x.shape
    lc = ctx.shape[1]
    depth = mod_w.shape[0]
    s = lc + l
    assert lc % TM == 0 and l % TM == 0 and lc % GDN_C == 0 and l % GDN_C == 0
    assert lc % ATT_TK == 0 and s % ATT_TK == 0 and lc % ATT_TQ == 0
    nct = lc // TM
    n_tok = nb * s

    nr = -(-(nb + 1) // 8) * 8
    cvec = jnp.zeros((nr, d), F32).at[:nb].set(c).at[nb].set(c_ctx)
    mod_all = _modulation(cvec, mod_w, mod_b).reshape(depth, nr, 6, d)

    cos_t, sin_t = _rope_tables(lc, l)
    bd64 = _block_diag_ones(LANES, HEAD_DIM)
    bd_gdn = _block_diag_ones(GDN_WIDTH, GDN_DK)
    q_scale = (HEAD_DIM ** -0.5) * math.log2(math.e)
    row = lambda v: v.reshape(1, -1).astype(F32)
    pad_lanes = lambda v: jnp.zeros((1, LANES), F32).at[0, :v.size].set(v.reshape(-1))

    xs = jnp.concatenate([ctx, x], axis=1)
    for layer in range(depth):
        mod = mod_all[layer]
        wi = w_in[layer]
        w_in_p = jnp.concatenate(
            [wi[:, :IN_A0 + 16], jnp.zeros((d, IN_PAD), F32), wi[:, IN_A0 + 16:]], axis=1).astype(BF16)
        qkv, ab, kv, gvgg, z, q = _in_proj(xs, mod, row(norm1_g[layer]), w_in_p, nct)

        dww = jnp.concatenate([conv_dw_w[layer], jnp.zeros((1, CONV_CH), F32)], axis=0)
        conv = _conformer(gvgg, dww, row(conv_dw_b[layer]), row(conv_ln_g[layer]), row(conv_ln_b[layer]),
                          conv_pw_w[layer].astype(BF16), row(conv_pw_b[layer]), nct)

        cw = jnp.concatenate([gdn_conv_w[layer], jnp.zeros((8 - SHORT_CONV, GDN_QKV), F32)], axis=0)
        gq, gk, gv, gb = _gdn_features(qkv, ab, cw, pad_lanes(gdn_a_log[layer]), pad_lanes(gdn_dt_bias[layer]), nct)
        o_f, o_b = _gdn_scan(gq, gk, gv, gb, lc // GDN_C)

        qt, kh, vt = _attn_prep(q, kv, cos_t, sin_t,
                                row(jnp.tile(attn_q_norm_g[layer], 2)), row(jnp.tile(attn_k_norm_g[layer], 2)),
                                bd64, q_scale)
        attn = _attention(qt, kh, vt, lc)

        wr = jnp.zeros((d, LANES), F32).at[:, :N_GROUPS].set(router_group_w[layer])
        wr = wr.at[:, N_GROUPS:N_GROUPS + N_EXPERTS].set(router_expert_w[layer])
        br = jnp.zeros((1, LANES), F32).at[0, :N_GROUPS].set(router_group_b[layer])
        br = br.at[0, N_GROUPS:N_GROUPS + N_EXPERTS].set(router_expert_b[layer])
        xs, h2, ri, rw = _out_proj(conv, o_f, o_b, z, attn, xs, mod,
                                   row(jnp.tile(gdn_norm_g[layer], GDN_HEADS)), bd_gdn,
                                   w_out[layer].astype(BF16), row(norm2_g[layer]), wr, br, nct)

        slot, wts, slot_tok, blk_e, n_used = _route(ri, rw, n_tok)
        xe = h2.reshape(n_tok, d)[slot_tok]
        ye = _experts(xe, blk_e, n_used, expert_w_gate, expert_w_up, expert_w_down, layer)
        y = ye[slot[:, 0]] * wts[:, 0:1] + ye[slot[:, 1]] * wts[:, 1:2]
        xs = _combine(xs, y.reshape(nb, s, d), mod, row(final_norm_g), nct, layer == depth - 1)
    return xs[:, lc:, :]
```

```python
import functools
import math

import jax
import jax.numpy as jnp
from jax import lax
from jax.experimental import pallas as pl
from jax.experimental.pallas import tpu as pltpu

F32 = jnp.float32
BF16 = jnp.bfloat16
HIGHEST = lax.Precision.HIGHEST

EPS = 1e-6
GRID_W = 64
CONV_CH = 256
CONV_WIDTH = 31
GDN_HEADS = 4
GDN_DK = 64
GDN_WIDTH = 256
GDN_QKV = 768
SHORT_CONV = 5
HEAD_DIM = 64
ATTN_Q_HEADS = 8
ATTN_KV_HEADS = 2
ATTN_GROUP = 4
ATTN_WIDTH = 512
ROPE_THETA = 10000.0
N_GROUPS = 4
EXPERTS_PER_GROUP = 8
N_EXPERTS = 32
TOP_K = 2
D_EXPERT = 512

LANES = 128
TM = 256
GDN_C = 128
CONV_HALO = 16
SHORT_HALO = 8
ATT_TQ = 256
ATT_TK = 256
VT_ROWS = 80
MOE_BM = 256
VMEM_LIMIT = 56 * 1024 * 1024

IN_A0 = GDN_QKV
IN_PAD = LANES - 16
C_QKV = (0, 768)
C_AB = (768, 896)
C_KV = (896, 1152)
C_GVGG = (1152, 1664)
C_Z = (1664, 1920)
C_Q = (1920, 2432)
IN_COLS = 2432


def _cparams(sem):
    return pltpu.CompilerParams(dimension_semantics=sem, vmem_limit_bytes=VMEM_LIMIT)


def _silu(x):
    return x * jax.nn.sigmoid(x)


def _dot(a, b, **kw):
    return jnp.dot(a, b, preferred_element_type=F32, **kw)


def _dot_nt(a, b):
    return lax.dot_general(a, b, (((1,), (1,)), ((), ())), preferred_element_type=F32)


def _dot_tn(a, b):
    return lax.dot_general(a, b, (((0,), (0,)), ((), ())), preferred_element_type=F32)


def _mod_kernel(c_ref, w_ref, b_ref, o_ref):
    c = c_ref[...]
    o_ref[0] = _dot(_silu(c), w_ref[0], precision=HIGHEST) + b_ref[0]


def _modulation(cvec, mod_w, mod_b):
    depth, d, n = mod_w.shape
    nr = cvec.shape[0]
    tn = 768
    return pl.pallas_call(
        _mod_kernel,
        grid=(depth, n // tn),
        in_specs=[pl.BlockSpec((nr, d), lambda l, j: (0, 0)),
                  pl.BlockSpec((1, d, tn), lambda l, j: (l, 0, j)),
                  pl.BlockSpec((1, 1, tn), lambda l, j: (l, 0, j))],
        out_specs=pl.BlockSpec((1, nr, tn), lambda l, j: (l, 0, j)),
        out_shape=jax.ShapeDtypeStruct((depth, nr, n), F32),
        compiler_params=_cparams(("parallel", "parallel")),
        name="modulation",
    )(cvec, mod_w, mod_b.reshape(depth, 1, n))


def _rms(x, g):
    return x * lax.rsqrt(jnp.mean(x * x, axis=-1, keepdims=True) + EPS) * g


def _inproj_kernel(x_ref, mod_ref, g_ref, w_ref, qkv_ref, ab_ref, kv_ref, gvgg_ref, z_ref, q_ref):
    x = x_ref[0]
    sh = mod_ref[0, 0:1, :]
    sc = mod_ref[0, 1:2, :]
    h = (_rms(x, g_ref[...]) * (1.0 + sc) + sh).astype(BF16)
    for ref, (c0, c1) in ((qkv_ref, C_QKV), (ab_ref, C_AB), (kv_ref, C_KV),
                          (gvgg_ref, C_GVGG), (z_ref, C_Z), (q_ref, C_Q)):
        ref[0] = _dot(h, w_ref[:, c0:c1])


def _in_proj(x, mod, g1, w_in_p, nct):
    nb, s, d = x.shape
    nt = s // TM
    widths = [c1 - c0 for c0, c1 in (C_QKV, C_AB, C_KV, C_GVGG, C_Z, C_Q)]
    tok = lambda w: pl.BlockSpec((1, TM, w), lambda b, t: (b, t, 0))
    return pl.pallas_call(
        _inproj_kernel,
        grid=(nb, nt),
        in_specs=[tok(d),
                  pl.BlockSpec((1, 6, d), lambda b, t: (jnp.where(t < nct, nb, b), 0, 0)),
                  pl.BlockSpec((1, d), lambda b, t: (0, 0)),
                  pl.BlockSpec((d, IN_COLS), lambda b, t: (0, 0))],
        out_specs=[tok(w) for w in widths],
        out_shape=[jax.ShapeDtypeStruct((nb, s, w), F32) for w in widths],
        compiler_params=_cparams(("parallel", "parallel")),
        name="in_proj",
    )(x, mod, g1, w_in_p)


def _halo_specs(width, halo, s):
    per = TM // halo
    cur = pl.BlockSpec((1, TM, width), lambda b, t: (b, t, 0))
    prev = pl.BlockSpec((1, halo, width), lambda b, t: (b, jnp.maximum(t * per - 1, 0), 0))
    nxt = pl.BlockSpec((1, halo, width), lambda b, t: (b, jnp.minimum((t + 1) * per, s // halo - 1), 0))
    return [cur, prev, nxt]


def _halo_flags(nct, nt):
    t = pl.program_id(1)
    prev_ok = jnp.logical_and(t != 0, t != nct)
    next_ok = jnp.logical_and(t != nct - 1, t != nt - 1)
    return prev_ok, next_ok


def _conformer_kernel(cur_ref, prev_ref, next_ref, dww_ref, dwb_ref, lng_ref, lnb_ref, pww_ref, pwb_ref,
                      o_ref, ext_ref, *, nct, nt):
    prev_ok, next_ok = _halo_flags(nct, nt)

    def glu(v):
        return v[:, :CONV_CH] * jax.nn.sigmoid(v[:, CONV_CH:])

    ext_ref[0:CONV_HALO, :] = jnp.where(prev_ok, glu(prev_ref[0]), 0.0)
    ext_ref[CONV_HALO:CONV_HALO + TM, :] = glu(cur_ref[0])
    ext_ref[CONV_HALO + TM:2 * CONV_HALO + TM, :] = jnp.where(next_ok, glu(next_ref[0]), 0.0)
    rb = 64
    off = CONV_HALO - CONV_WIDTH // 2
    for r in range(TM // rb):
        acc = jnp.zeros((rb, CONV_CH), F32) + dwb_ref[...]
        for j in range(CONV_WIDTH):
            acc = acc + ext_ref[pl.ds(r * rb + off + j, rb), :] * dww_ref[j:j + 1, :]
        mu = jnp.mean(acc, axis=-1, keepdims=True)
        xc = acc - mu
        y = xc * lax.rsqrt(jnp.mean(xc * xc, axis=-1, keepdims=True) + EPS) * lng_ref[...] + lnb_ref[...]
        h = _silu(y).astype(BF16)
        o_ref[0, r * rb:(r + 1) * rb, :] = _dot(h, pww_ref[...]) + pwb_ref[...]


def _conformer(gvgg, dww, dwb, lng, lnb, pww, pwb, nct):
    nb, s, _ = gvgg.shape
    nt = s // TM
    row = lambda w: pl.BlockSpec((1, w), lambda b, t: (0, 0))
    return pl.pallas_call(
        functools.partial(_conformer_kernel, nct=nct, nt=nt),
        grid=(nb, nt),
        in_specs=_halo_specs(2 * CONV_CH, CONV_HALO, s) + [
            pl.BlockSpec((CONV_WIDTH + 1, CONV_CH), lambda b, t: (0, 0)),
            row(CONV_CH), row(CONV_CH), row(CONV_CH),
            pl.BlockSpec((CONV_CH, CONV_CH), lambda b, t: (0, 0)),
            row(CONV_CH)],
        out_specs=pl.BlockSpec((1, TM, CONV_CH), lambda b, t: (b, t, 0)),
        out_shape=jax.ShapeDtypeStruct((nb, s, CONV_CH), F32),
        scratch_shapes=[pltpu.VMEM((TM + 2 * CONV_HALO, CONV_CH), F32)],
        compiler_params=_cparams(("parallel", "parallel")),
        name="conformer",
    )(gvgg, gvgg, gvgg, dww, dwb, lng, lnb, pww, pwb)


def _gdn_feat_kernel(cur_ref, prev_ref, next_ref, cw_ref, ab_ref, alog_ref, dtb_ref,
                     q_ref, k_ref, v_ref, gb_ref, ext_ref, *, nct, nt):
    prev_ok, next_ok = _halo_flags(nct, nt)
    ext_ref[0:SHORT_HALO, :] = jnp.where(prev_ok, prev_ref[0], 0.0)
    ext_ref[SHORT_HALO:SHORT_HALO + TM, :] = cur_ref[0]
    ext_ref[SHORT_HALO + TM:2 * SHORT_HALO + TM, :] = jnp.where(next_ok, next_ref[0], 0.0)
    rb = 32
    off = SHORT_HALO - SHORT_CONV // 2
    for r in range(TM // rb):
        acc = jnp.zeros((rb, GDN_QKV), F32)
        for j in range(SHORT_CONV):
            acc = acc + ext_ref[pl.ds(r * rb + off + j, rb), :] * cw_ref[j:j + 1, :]
        y = _silu(acc)
        rows = slice(r * rb, (r + 1) * rb)
        for h in range(GDN_HEADS):
            qh = y[:, h * GDN_DK:(h + 1) * GDN_DK]
            kh = y[:, GDN_WIDTH + h * GDN_DK:GDN_WIDTH + (h + 1) * GDN_DK]
            vh = y[:, 2 * GDN_WIDTH + h * GDN_DK:2 * GDN_WIDTH + (h + 1) * GDN_DK]
            qn = qh * lax.rsqrt(jnp.sum(qh * qh, axis=-1, keepdims=True) + EPS) * (GDN_DK ** -0.5)
            kn = kh * lax.rsqrt(jnp.sum(kh * kh, axis=-1, keepdims=True) + EPS)
            q_ref[0, h, rows, :] = qn
            k_ref[0, h, rows, :] = kn
            v_ref[0, h, rows, :] = vh
    ab = ab_ref[0]
    lane = lax.broadcasted_iota(jnp.int32, ab.shape, 1)
    xa = ab + dtb_ref[...]
    softplus = jnp.maximum(xa, 0.0) + jnp.log(1.0 + jnp.exp(-jnp.abs(xa)))
    g = -jnp.exp(alog_ref[...]) * softplus
    beta = jax.nn.sigmoid(ab)
    gb_ref[0] = jnp.where(lane < 2 * GDN_HEADS, g, jnp.where(lane < 4 * GDN_HEADS, beta, 0.0))


def _gdn_features(qkv, ab, cw, alog, dtb, nct):
    nb, s, _ = qkv.shape
    nt = s // TM
    head = pl.BlockSpec((1, GDN_HEADS, TM, GDN_DK), lambda b, t: (b, 0, t, 0))
    hshape = jax.ShapeDtypeStruct((nb, GDN_HEADS, s, GDN_DK), F32)
    return pl.pallas_call(
        functools.partial(_gdn_feat_kernel, nct=nct, nt=nt),
        grid=(nb, nt),
        in_specs=_halo_specs(GDN_QKV, SHORT_HALO, s) + [
            pl.BlockSpec((8, GDN_QKV), lambda b, t: (0, 0)),
            pl.BlockSpec((1, TM, LANES), lambda b, t: (b, t, 0)),
            pl.BlockSpec((1, LANES), lambda b, t: (0, 0)),
            pl.BlockSpec((1, LANES), lambda b, t: (0, 0))],
        out_specs=[head, head, head, pl.BlockSpec((1, TM, LANES), lambda b, t: (b, t, 0))],
        out_shape=[hshape, hshape, hshape, jax.ShapeDtypeStruct((nb, s, LANES), F32)],
        scratch_shapes=[pltpu.VMEM((TM + 2 * SHORT_HALO, GDN_QKV), F32)],
        compiler_params=_cparams(("parallel", "parallel")),
        name="gdn_features",
    )(qkv, qkv, qkv, cw, ab, alog, dtb)


def _gdn_scan_kernel(qf_ref, kf_ref, vf_ref, gf_ref, qb_ref, kb_ref, vb_ref, gbk_ref,
                     of_ref, ob_ref, s_ref):
    n = pl.program_id(1)

    @pl.when(n == 0)
    def _():
        s_ref[...] = jnp.zeros(s_ref.shape, F32)

    c = GDN_C
    row = lax.broadcasted_iota(jnp.int32, (c, c), 0)
    col = lax.broadcasted_iota(jnp.int32, (c, c), 1)
    eye = (row == col).astype(F32)
    blk = {2 ** e: (row >> e) == (col >> e) for e in range(1, int(math.log2(c)) + 1)}

    chains = []
    for d, (q_ref, k_ref, v_ref, g_ref, o_ref) in enumerate(
            ((qf_ref, kf_ref, vf_ref, gf_ref, of_ref), (qb_ref, kb_ref, vb_ref, gbk_ref, ob_ref))):
        incl = (row >= col) if d == 0 else (row <= col)
        strict = (row > col) if d == 0 else (row < col)
        gb = g_ref[0]
        gam_all = _dot(incl.astype(F32), gb, precision=HIGHEST)
        gam_t = gam_all.T
        last_row = c - 1 if d == 0 else 0
        for h in range(GDN_HEADS):
            cc = d * GDN_HEADS + h
            gam = gam_all[:, cc:cc + 1]
            beta = gb[:, 2 * GDN_HEADS + cc:2 * GDN_HEADS + cc + 1]
            kh = k_ref[0, h]
            chains.append(dict(
                d=d, h=h, o_ref=o_ref, strict=strict, gam=gam, beta=beta, kh=kh, qh=q_ref[0, h], vh=v_ref[0, h],
                last=gam_all[last_row:last_row + 1, cc:cc + 1],
                decay=jnp.exp(jnp.where(incl, gam - gam_t[cc:cc + 1, :], -1e30)),
                kbeta=kh * beta))
    for ch in chains:
        ch["a"] = jnp.where(ch["strict"], _dot_nt(ch["kbeta"], ch["kh"]) * ch["decay"], 0.0)
    for ch in chains:
        ch["t"] = eye - jnp.where(blk[2], ch["a"], 0.0)
    bsz = 2
    while bsz < c:
        off = jnp.logical_and(blk[2 * bsz], jnp.logical_not(blk[bsz]))
        for ch in chains:
            ch["tl"] = _dot(ch["t"], jnp.where(off, ch["a"], 0.0))
        for ch in chains:
            ch["t"] = ch["t"] - _dot(ch["tl"], ch["t"])
        bsz *= 2
    for ch in chains:
        rhs = jnp.concatenate([ch["vh"] * ch["beta"], ch["kbeta"] * jnp.exp(ch["gam"])], axis=1)
        ch["uw"] = _dot(ch["t"], rhs)
    for ch in chains:
        ch["pm"] = _dot_nt(ch["qh"], ch["kh"]) * ch["decay"]
    for ch in chains:
        ch["s"] = s_ref[ch["d"], ch["h"]]
        ch["v_new"] = ch["uw"][:, :GDN_DK] - _dot(ch["uw"][:, GDN_DK:], ch["s"])
    for ch in chains:
        h = ch["h"]
        qg = ch["qh"] * jnp.exp(ch["gam"])
        ch["o_ref"][0, :, h * GDN_DK:(h + 1) * GDN_DK] = _dot(qg, ch["s"]) + _dot(ch["pm"], ch["v_new"])
    for ch in chains:
        kd = ch["kh"] * jnp.exp(ch["last"] - ch["gam"])
        s_ref[ch["d"], ch["h"]] = ch["s"] * jnp.exp(ch["last"]) + _dot_tn(kd, ch["v_new"])


def _gdn_scan(q, k, v, gb, ncc):
    nb, _, s, _ = q.shape
    nc = s // GDN_C

    def bwd(n):
        return jnp.where(n < ncc, ncc - 1 - n, nc + ncc - 1 - n)

    hf = pl.BlockSpec((1, GDN_HEADS, GDN_C, GDN_DK), lambda b, n: (b, 0, n, 0))
    hb = pl.BlockSpec((1, GDN_HEADS, GDN_C, GDN_DK), lambda b, n: (b, 0, bwd(n), 0))
    gf = pl.BlockSpec((1, GDN_C, LANES), lambda b, n: (b, n, 0))
    gk = pl.BlockSpec((1, GDN_C, LANES), lambda b, n: (b, bwd(n), 0))
    of = pl.BlockSpec((1, GDN_C, GDN_WIDTH), lambda b, n: (b, n, 0))
    ob = pl.BlockSpec((1, GDN_C, GDN_WIDTH), lambda b, n: (b, bwd(n), 0))
    oshape = jax.ShapeDtypeStruct((nb, s, GDN_WIDTH), F32)
    return pl.pallas_call(
        _gdn_scan_kernel,
        grid=(nb, nc),
        in_specs=[hf, hf, hf, gf, hb, hb, hb, gk],
        out_specs=[of, ob],
        out_shape=[oshape, oshape],
        scratch_shapes=[pltpu.VMEM((2, GDN_HEADS, GDN_DK, GDN_DK), F32)],
        compiler_params=_cparams(("parallel", "arbitrary")),
        name="gdn_scan",
    )(q, k, v, gb, q, k, v, gb)


def _attn_prep_kernel(q_ref, kv_ref, cos_ref, sin_ref, qg_ref, kg_ref, bd_ref,
                      qt_ref, kh_ref, vt_ref, *, q_scale):
    cos = cos_ref[...]
    sin = sin_ref[...]
    bd = bd_ref[...]
    lane = lax.broadcasted_iota(jnp.int32, cos.shape, 1)
    first = (lane % 32) < 16

    def norm_rope(x, g):
        ss = _dot(x * x, bd, precision=HIGHEST)
        y = x * lax.rsqrt(ss * (1.0 / HEAD_DIM) + EPS) * g
        swapped = jnp.where(first, pltpu.roll(y, LANES - 16, 1), pltpu.roll(y, 16, 1))
        return y * cos + swapped * sin

    q = q_ref[0]
    for j in range(ATTN_WIDTH // LANES):
        yt = (norm_rope(q[:, j * LANES:(j + 1) * LANES], qg_ref[...]) * q_scale).T
        qt_ref[0, 2 * j] = yt[:HEAD_DIM].astype(BF16)
        qt_ref[0, 2 * j + 1] = yt[HEAD_DIM:].astype(BF16)
    kv = kv_ref[0]
    y = norm_rope(kv[:, :LANES], kg_ref[...])
    kh_ref[0, 0] = y[:, :HEAD_DIM].astype(BF16)
    kh_ref[0, 1] = y[:, HEAD_DIM:].astype(BF16)
    vt = kv[:, LANES:].T.astype(BF16)
    ones = jnp.ones((VT_ROWS - HEAD_DIM, ATT_TK), BF16)
    for g in range(ATTN_KV_HEADS):
        for cidx in range(TM // ATT_TK):
            vt_ref[0, g, cidx, 0:HEAD_DIM, :] = vt[g * HEAD_DIM:(g + 1) * HEAD_DIM,
                                                   cidx * ATT_TK:(cidx + 1) * ATT_TK]
            vt_ref[0, g, cidx, HEAD_DIM:VT_ROWS, :] = ones


def _attn_prep(q, kv, cos_t, sin_t, qg, kg, bd, q_scale):
    nb, s, _ = q.shape
    nt = s // TM
    tab = pl.BlockSpec((TM, LANES), lambda b, t: (t, 0))
    row = pl.BlockSpec((1, LANES), lambda b, t: (0, 0))
    per = TM // ATT_TK
    return pl.pallas_call(
        functools.partial(_attn_prep_kernel, q_scale=q_scale),
        grid=(nb, nt),
        in_specs=[pl.BlockSpec((1, TM, ATTN_WIDTH), lambda b, t: (b, t, 0)),
                  pl.BlockSpec((1, TM, 2 * LANES), lambda b, t: (b, t, 0)),
                  tab, tab, row, row,
                  pl.BlockSpec((LANES, LANES), lambda b, t: (0, 0))],
        out_specs=[pl.BlockSpec((1, ATTN_Q_HEADS, HEAD_DIM, TM), lambda b, t: (b, 0, 0, t)),
                   pl.BlockSpec((1, ATTN_KV_HEADS, TM, HEAD_DIM), lambda b, t: (b, 0, t, 0)),
                   pl.BlockSpec((1, ATTN_KV_HEADS, per, VT_ROWS, ATT_TK), lambda b, t: (b, 0, t, 0, 0))],
        out_shape=[jax.ShapeDtypeStruct((nb, ATTN_Q_HEADS, HEAD_DIM, s), BF16),
                   jax.ShapeDtypeStruct((nb, ATTN_KV_HEADS, s, HEAD_DIM), BF16),
                   jax.ShapeDtypeStruct((nb, ATTN_KV_HEADS, s // ATT_TK, VT_ROWS, ATT_TK), BF16)],
        compiler_params=_cparams(("parallel", "parallel")),
        name="attn_prep",
    )(q, kv, cos_t, sin_t, qg, kg, bd)


def _attn_kernel(qt_ref, k_ref, vt_ref, o_ref, m_ref, acc_ref, s_ref, *, nctq, lc, s_len):
    qi = pl.program_id(2)
    m_ref[...] = jnp.full(m_ref.shape, -jnp.inf, F32)
    acc_ref[...] = jnp.zeros(acc_ref.shape, F32)
    nk = jnp.where(qi < nctq, lc // ATT_TK, s_len // ATT_TK)

    def score(j, slot):
        ks = pl.multiple_of(j * ATT_TK, ATT_TK)
        k = k_ref[0, 0, pl.ds(ks, ATT_TK), :]
        for h in range(ATTN_GROUP):
            s_ref[slot, h] = _dot(k, qt_ref[0, h])

    def step(j, slot):
        score(jnp.minimum(j + 1, nk - 1), 1 - slot)
        vt = vt_ref[0, 0, j]
        for h in range(ATTN_GROUP):
            s = s_ref[slot, h]
            m_old = m_ref[h]
            m_new = jnp.maximum(m_old, jnp.max(s, axis=0, keepdims=True))
            alpha = jnp.exp2(m_old - m_new)
            p = jnp.exp2(s - m_new)
            acc_ref[h] = alpha * acc_ref[h] + _dot(vt, p.astype(BF16))
            m_ref[h] = m_new

    def pair(i, carry):
        step(2 * i, 0)
        step(2 * i + 1, 1)
        return carry

    score(0, 0)
    lax.fori_loop(0, nk // 2, pair, 0)

    @pl.when(nk % 2 == 1)
    def _():
        step(nk - 1, 0)

    for hp in range(ATTN_GROUP // 2):
        pair = jnp.concatenate(
            [acc_ref[h, 0:HEAD_DIM, :] / acc_ref[h, HEAD_DIM:HEAD_DIM + 1, :] for h in (2 * hp, 2 * hp + 1)], axis=0)
        o_ref[0, :, hp * LANES:(hp + 1) * LANES] = pair.T


def _attention(qt, kh, vt, lc):
    nb, _, _, s = qt.shape
    nq = s // ATT_TQ
    return pl.pallas_call(
        functools.partial(_attn_kernel, nctq=lc // ATT_TQ, lc=lc, s_len=s),
        grid=(nb, ATTN_KV_HEADS, nq),
        in_specs=[pl.BlockSpec((1, ATTN_GROUP, HEAD_DIM, ATT_TQ), lambda b, g, i: (b, g, 0, i)),
                  pl.BlockSpec((1, 1, s, HEAD_DIM), lambda b, g, i: (b, g, 0, 0)),
                  pl.BlockSpec((1, 1, s // ATT_TK, VT_ROWS, ATT_TK), lambda b, g, i: (b, g, 0, 0, 0))],
        out_specs=pl.BlockSpec((1, ATT_TQ, ATTN_GROUP * HEAD_DIM), lambda b, g, i: (b, i, g)),
        out_shape=jax.ShapeDtypeStruct((nb, s, ATTN_WIDTH), F32),
        scratch_shapes=[pltpu.VMEM((ATTN_GROUP, 1, ATT_TQ), F32),
                        pltpu.VMEM((ATTN_GROUP, VT_ROWS, ATT_TQ), F32),
                        pltpu.VMEM((2, ATTN_GROUP, ATT_TK, ATT_TQ), F32)],
        compiler_params=_cparams(("parallel", "parallel", "parallel")),
        name="attention",
    )(qt, kh, vt)


def _outproj_kernel(conv_ref, of_ref, ob_ref, z_ref, attn_ref, x_ref, mod_ref, gg_ref, bd_ref, w_ref,
                    n2_ref, wr_ref, br_ref, xo_ref, h2_ref, ri_ref, rw_ref):
    o = of_ref[0] + ob_ref[0]
    ss = _dot(o * o, bd_ref[...], precision=HIGHEST)
    gdn = o * lax.rsqrt(ss * (1.0 / GDN_DK) + EPS) * gg_ref[...] * _silu(z_ref[0])
    y = _dot(conv_ref[0].astype(BF16), w_ref[0:CONV_CH, :])
    y = y + _dot(gdn.astype(BF16), w_ref[CONV_CH:CONV_CH + GDN_WIDTH, :])
    y = y + _dot(attn_ref[0].astype(BF16), w_ref[CONV_CH + GDN_WIDTH:, :])
    x = x_ref[0] + mod_ref[0, 2:3, :] * y
    xo_ref[0] = x
    h2 = _rms(x, n2_ref[...]) * (1.0 + mod_ref[0, 4:5, :]) + mod_ref[0, 3:4, :]
    h2_ref[0] = h2
    lg = _dot(h2, wr_ref[...], precision=HIGHEST) + br_ref[...]
    lane = lax.broadcasted_iota(jnp.int32, lg.shape, 1)
    lane_f = lane.astype(F32)

    def first_max(v):
        mx = jnp.max(v, axis=-1, keepdims=True)
        return mx, jnp.min(jnp.where(v == mx, lane_f, float(LANES)), axis=-1, keepdims=True)

    lgg = jnp.where(lane < N_GROUPS, lg, -jnp.inf)
    gmax, grp = first_max(lgg)
    pg_sel = 1.0 / jnp.sum(jnp.exp(lgg - gmax), axis=-1, keepdims=True)
    in_grp = jnp.logical_and(lane >= N_GROUPS, ((lane - N_GROUPS) >> 3).astype(F32) == grp)
    le = jnp.where(in_grp, lg, -jnp.inf)
    e1max, i1 = first_max(le)
    e2max, i2 = first_max(jnp.where(lane_f == i1, -jnp.inf, le))
    r = jnp.exp(e2max - e1max)
    w1 = pg_sel / (1.0 + r)
    ri_ref[0] = jnp.where(lane == 0, i1, jnp.where(lane == 1, i2, float(N_GROUPS))).astype(jnp.int32) - N_GROUPS
    rw_ref[0] = jnp.where(lane == 0, w1, jnp.where(lane == 1, w1 * r, 0.0))


def _out_proj(conv, o_f, o_b, z, attn, x, mod, gdn_g, bd, w_out, n2, wr, br, nct):
    nb, s, d = x.shape
    nt = s // TM
    tok = lambda w: pl.BlockSpec((1, TM, w), lambda b, t: (b, t, 0))
    full = lambda a: pl.BlockSpec(a.shape, lambda b, t: (0,) * a.ndim)
    return pl.pallas_call(
        _outproj_kernel,
        grid=(nb, nt),
        in_specs=[tok(CONV_CH), tok(GDN_WIDTH), tok(GDN_WIDTH), tok(GDN_WIDTH), tok(ATTN_WIDTH), tok(d),
                  pl.BlockSpec((1, 6, d), lambda b, t: (jnp.where(t < nct, nb, b), 0, 0)),
                  full(gdn_g), full(bd), full(w_out), full(n2), full(wr), full(br)],
        out_specs=[tok(d), tok(d), tok(LANES), tok(LANES)],
        out_shape=[jax.ShapeDtypeStruct((nb, s, d), F32), jax.ShapeDtypeStruct((nb, s, d), F32),
                   jax.ShapeDtypeStruct((nb, s, LANES), jnp.int32), jax.ShapeDtypeStruct((nb, s, LANES), F32)],
        compiler_params=_cparams(("parallel", "parallel")),
        name="out_proj",
    )(conv, o_f, o_b, z, attn, x, mod, gdn_g, bd, w_out, n2, wr, br)


def _dispatch_kernel(slot_ref, h_ref, xe_in_ref, xe_ref, sem):
    del xe_in_ref
    base = pl.program_id(0) * (TM * TOP_K)

    def row_copy(r, k):
        dst = slot_ref[base + TOP_K * r + k]
        return pltpu.make_async_copy(h_ref.at[pl.ds(r, 1)], xe_ref.at[pl.ds(dst, 1)], sem)

    def issue(r, carry):
        for k in range(TOP_K):
            row_copy(r, k).start()
        return carry

    lax.fori_loop(0, TM, issue, 0, unroll=8)
    for _ in range(TOP_K):
        pltpu.make_async_copy(h_ref, xe_ref.at[pl.ds(0, TM)], sem).wait()


def _dispatch(h2, slots, n_slots):
    n_tok, d = h2.shape
    grid_spec = pltpu.PrefetchScalarGridSpec(
        num_scalar_prefetch=1,
        grid=(n_tok // TM,),
        in_specs=[pl.BlockSpec((TM, d), lambda i, sl: (i, 0)),
                  pl.BlockSpec(memory_space=pl.ANY)],
        out_specs=pl.BlockSpec(memory_space=pl.ANY),
        scratch_shapes=[pltpu.SemaphoreType.DMA(())],
    )
    return pl.pallas_call(
        _dispatch_kernel,
        grid_spec=grid_spec,
        out_shape=jax.ShapeDtypeStruct((n_slots, d), F32),
        input_output_aliases={2: 0},
        compiler_params=_cparams(("arbitrary",)),
        name="moe_dispatch",
    )(slots, h2, jnp.zeros((n_slots, d), F32))


def _expert_kernel(be_ref, nu_ref, x_ref, wg_ref, wu_ref, wd_ref, o_ref):
    i = pl.program_id(0)

    @pl.when(i < nu_ref[0])
    def _():
        x = x_ref[...].astype(BF16)
        g = _dot(x, wg_ref[0, 0].astype(BF16))
        u = _dot(x, wu_ref[0, 0].astype(BF16))
        h = (_silu(g) * u).astype(BF16)
        o_ref[...] = _dot(h, wd_ref[0, 0].astype(BF16))

    @pl.when(i >= nu_ref[0])
    def _():
        o_ref[...] = jnp.zeros(o_ref.shape, F32)


def _experts(xs, blk_e, n_used, w_gate, w_up, w_down, layer):
    n_slots, d = xs.shape
    nblk = n_slots // MOE_BM
    de = w_gate.shape[-1]
    grid_spec = pltpu.PrefetchScalarGridSpec(
        num_scalar_prefetch=2,
        grid=(nblk,),
        in_specs=[pl.BlockSpec((MOE_BM, d), lambda i, be, nu: (jnp.minimum(i, nu[0] - 1), 0)),
                  pl.BlockSpec((1, 1, d, de), lambda i, be, nu: (layer, be[i], 0, 0)),
                  pl.BlockSpec((1, 1, d, de), lambda i, be, nu: (layer, be[i], 0, 0)),
                  pl.BlockSpec((1, 1, de, d), lambda i, be, nu: (layer, be[i], 0, 0))],
        out_specs=pl.BlockSpec((MOE_BM, d), lambda i, be, nu: (i, 0)),
    )
    return pl.pallas_call(
        _expert_kernel,
        grid_spec=grid_spec,
        out_shape=jax.ShapeDtypeStruct((n_slots, d), F32),
        compiler_params=_cparams(("arbitrary",)),
        name="experts",
    )(blk_e, n_used, xs, w_gate, w_up, w_down)


def _combine_kernel(slot_ref, x_ref, rw_ref, mod_ref, fg_ref, ye_ref, o_ref, buf_ref, sem, *, final, nt):
    base = (pl.program_id(0) * nt + pl.program_id(1)) * (TM * TOP_K)

    def row_copy(r, k):
        src = slot_ref[base + TOP_K * r + k]
        return pltpu.make_async_copy(ye_ref.at[pl.ds(src, 1)], buf_ref.at[k, pl.ds(r, 1)], sem.at[k])

    def issue(r, carry):
        for k in range(TOP_K):
            row_copy(r, k).start()
        return carry

    lax.fori_loop(0, TM, issue, 0, unroll=8)
    for k in range(TOP_K):
        pltpu.make_async_copy(ye_ref.at[pl.ds(0, TM)], buf_ref.at[k], sem.at[k]).wait()
    rw = rw_ref[0]
    y = rw[:, 0:1] * buf_ref[0] + rw[:, 1:2] * buf_ref[1]
    x = x_ref[0] + mod_ref[0, 5:6, :] * y
    if final:
        x = _rms(x, fg_ref[...])
    o_ref[0] = x


def _combine(x, ye, slots, rw, mod, fg, nct, final):
    nb, s, d = x.shape
    nt = s // TM
    tok = lambda w: pl.BlockSpec((1, TM, w), lambda b, t, sl: (b, t, 0))
    grid_spec = pltpu.PrefetchScalarGridSpec(
        num_scalar_prefetch=1,
        grid=(nb, nt),
        in_specs=[tok(d), tok(LANES),
                  pl.BlockSpec((1, 6, d), lambda b, t, sl: (jnp.where(t < nct, nb, b), 0, 0)),
                  pl.BlockSpec((1, d), lambda b, t, sl: (0, 0)),
                  pl.BlockSpec(memory_space=pl.ANY)],
        out_specs=tok(d),
        scratch_shapes=[pltpu.VMEM((TOP_K, TM, d), F32), pltpu.SemaphoreType.DMA((TOP_K,))],
    )
    return pl.pallas_call(
        functools.partial(_combine_kernel, final=final, nt=nt),
        grid_spec=grid_spec,
        out_shape=jax.ShapeDtypeStruct((nb, s, d), F32),
        compiler_params=_cparams(("arbitrary", "arbitrary")),
        name="moe_combine",
    )(slots, x, rw, mod, fg, ye)


def _rank_kernel(ri_ref, rank_ref, cnt_ref, carry_ref):
    @pl.when(pl.program_id(0) == 0)
    def _():
        carry_ref[...] = jnp.zeros(carry_ref.shape, F32)

    ri = ri_ref[0]
    lane = lax.broadcasted_iota(jnp.int32, ri.shape, 1)
    oh1 = lane == ri[:, 0:1]
    oh2 = lane == ri[:, 1:2]
    oh = jnp.logical_or(oh1, oh2).astype(F32)
    row = lax.broadcasted_iota(jnp.int32, (TM, TM), 0)
    col = lax.broadcasted_iota(jnp.int32, (TM, TM), 1)
    before = carry_ref[0:1, :] + _dot((row > col).astype(BF16), oh.astype(BF16))
    r1 = jnp.sum(jnp.where(oh1, before, 0.0), axis=-1, keepdims=True)
    r2 = jnp.sum(jnp.where(oh2, before, 0.0), axis=-1, keepdims=True)
    rank_ref[0] = jnp.where(lane == 0, r1, jnp.where(lane == 1, r2, 0.0)).astype(jnp.int32)
    carry_ref[...] = carry_ref[...] + jnp.sum(oh, axis=0, keepdims=True)
    cnt_ref[...] = carry_ref[...]


def _slot_kernel(ri_ref, rank_ref, ps_ref, slot_ref):
    ri = ri_ref[0]
    rk = rank_ref[0]
    lane = lax.broadcasted_iota(jnp.int32, ri.shape, 1)
    ps = ps_ref[...]
    s1 = jnp.sum(jnp.where(lane == ri[:, 0:1], ps, 0.0), axis=-1, keepdims=True).astype(jnp.int32) + rk[:, 0:1]
    s2 = jnp.sum(jnp.where(lane == ri[:, 1:2], ps, 0.0), axis=-1, keepdims=True).astype(jnp.int32) + rk[:, 1:2]
    slot_ref[0] = jnp.where(lane == 0, s1, jnp.where(lane == 1, s2, 0))


def _route(ri, n_tok):
    ntile = n_tok // TM
    ri = ri.reshape(ntile, TM, LANES)
    tile = pl.BlockSpec((1, TM, LANES), lambda i: (i, 0, 0))
    rank, cnt = pl.pallas_call(
        _rank_kernel,
        grid=(ntile,),
        in_specs=[tile],
        out_specs=[tile, pl.BlockSpec((8, LANES), lambda i: (0, 0))],
        out_shape=[jax.ShapeDtypeStruct((ntile, TM, LANES), jnp.int32), jax.ShapeDtypeStruct((8, LANES), F32)],
        scratch_shapes=[pltpu.VMEM((8, LANES), F32)],
        compiler_params=_cparams(("arbitrary",)),
        name="moe_rank",
    )(ri)
    counts = cnt[0, :N_EXPERTS].astype(jnp.int32)
    pcounts = (counts + MOE_BM - 1) // MOE_BM * MOE_BM
    pends = jnp.cumsum(pcounts)
    pstarts = jnp.zeros((1, LANES), F32).at[0, :N_EXPERTS].set((pends - pcounts).astype(F32))
    slot = pl.pallas_call(
        _slot_kernel,
        grid=(ntile,),
        in_specs=[tile, tile, pl.BlockSpec((1, LANES), lambda i: (0, 0))],
        out_specs=tile,
        out_shape=jax.ShapeDtypeStruct((ntile, TM, LANES), jnp.int32),
        compiler_params=_cparams(("parallel",)),
        name="moe_slot",
    )(ri, rank, pstarts)
    slots = slot.reshape(n_tok, LANES)[:, :TOP_K].reshape(-1)
    n_slots = (-(-n_tok * TOP_K // MOE_BM) + N_EXPERTS) * MOE_BM
    blk_start = jnp.arange(n_slots // MOE_BM, dtype=jnp.int32) * MOE_BM
    blk_e = jnp.minimum(jnp.sum((pends[None, :] <= blk_start[:, None]).astype(jnp.int32), axis=1), N_EXPERTS - 1)
    n_used = (pends[-1] // MOE_BM).astype(jnp.int32).reshape(1)
    return slots, n_slots, blk_e, n_used


def _rope_tables(lc, l):
    rows = l // GRID_W
    row = jnp.repeat(jnp.arange(rows), GRID_W).astype(F32)
    col = (jnp.arange(rows * GRID_W) % GRID_W).astype(F32)
    n_freq = HEAD_DIM // 4
    inv = ROPE_THETA ** (-jnp.arange(n_freq, dtype=F32) / n_freq)
    ar, ac = row[:, None] * inv, col[:, None] * inv
    cos = jnp.concatenate([jnp.cos(ar), jnp.cos(ar), jnp.cos(ac), jnp.cos(ac)], axis=-1)
    sin = jnp.concatenate([-jnp.sin(ar), jnp.sin(ar), -jnp.sin(ac), jnp.sin(ac)], axis=-1)
    cos = jnp.concatenate([jnp.ones((lc, HEAD_DIM), F32), cos], axis=0)
    sin = jnp.concatenate([jnp.zeros((lc, HEAD_DIM), F32), sin], axis=0)
    return jnp.tile(cos, (1, 2)), jnp.tile(sin, (1, 2))


def _block_diag_ones(n, blk):
    i = jnp.arange(n)
    return (i[:, None] // blk == i[None, :] // blk).astype(F32)


def kernel(x, c, ctx, c_ctx, mod_w, mod_b, norm1_g, norm2_g, w_in, conv_dw_w, conv_dw_b, conv_ln_g, conv_ln_b, conv_pw_w, conv_pw_b, gdn_conv_w, gdn_a_log, gdn_dt_bias, gdn_norm_g, attn_q_norm_g, attn_k_norm_g, w_out, router_group_w, router_group_b, router_expert_w, router_expert_b, expert_w_gate, expert_w_up, expert_w_down, final_norm_g):
    nb, l, d = x.shape
    lc = ctx.shape[1]
    depth = mod_w.shape[0]
    s = lc + l
    assert lc % TM == 0 and l % TM == 0 and lc % GDN_C == 0 and l % GDN_C == 0
    assert lc % ATT_TK == 0 and s % ATT_TK == 0 and lc % ATT_TQ == 0
    nct = lc // TM
    n_tok = nb * s

    nr = -(-(nb + 1) // 8) * 8
    cvec = jnp.zeros((nr, d), F32).at[:nb].set(c).at[nb].set(c_ctx)
    mod_all = _modulation(cvec, mod_w, mod_b).reshape(depth, nr, 6, d)

    cos_t, sin_t = _rope_tables(lc, l)
    bd64 = _block_diag_ones(LANES, HEAD_DIM)
    bd_gdn = _block_diag_ones(GDN_WIDTH, GDN_DK)
    q_scale = (HEAD_DIM ** -0.5) * math.log2(math.e)
    row = lambda v: v.reshape(1, -1).astype(F32)
    pad_lanes = lambda v: jnp.zeros((1, LANES), F32).at[0, :v.size].set(v.reshape(-1))

    xs = jnp.concatenate([ctx, x], axis=1)
    for layer in range(depth):
        mod = mod_all[layer]
        wi = w_in[layer]
        w_in_p = jnp.concatenate(
            [wi[:, :IN_A0 + 16], jnp.zeros((d, IN_PAD), F32), wi[:, IN_A0 + 16:]], axis=1).astype(BF16)
        qkv, ab, kv, gvgg, z, q = _in_proj(xs, mod, row(norm1_g[layer]), w_in_p, nct)

        dww = jnp.concatenate([conv_dw_w[layer], jnp.zeros((1, CONV_CH), F32)], axis=0)
        conv = _conformer(gvgg, dww, row(conv_dw_b[layer]), row(conv_ln_g[layer]), row(conv_ln_b[layer]),
                          conv_pw_w[layer].astype(BF16), row(conv_pw_b[layer]), nct)

        cw = jnp.concatenate([gdn_conv_w[layer], jnp.zeros((8 - SHORT_CONV, GDN_QKV), F32)], axis=0)
        gq, gk, gv, gb = _gdn_features(qkv, ab, cw, pad_lanes(gdn_a_log[layer]), pad_lanes(gdn_dt_bias[layer]), nct)
        o_f, o_b = _gdn_scan(gq, gk, gv, gb, lc // GDN_C)

        qt, kh, vt = _attn_prep(q, kv, cos_t, sin_t,
                                row(jnp.tile(attn_q_norm_g[layer], 2)), row(jnp.tile(attn_k_norm_g[layer], 2)),
                                bd64, q_scale)
        attn = _attention(qt, kh, vt, lc)

        wr = jnp.zeros((d, LANES), F32).at[:, :N_GROUPS].set(router_group_w[layer])
        wr = wr.at[:, N_GROUPS:N_GROUPS + N_EXPERTS].set(router_expert_w[layer])
        br = jnp.zeros((1, LANES), F32).at[0, :N_GROUPS].set(router_group_b[layer])
        br = br.at[0, N_GROUPS:N_GROUPS + N_EXPERTS].set(router_expert_b[layer])
        xs, h2, ri, rw = _out_proj(conv, o_f, o_b, z, attn, xs, mod,
                                   row(jnp.tile(gdn_norm_g[layer], GDN_HEADS)), bd_gdn,
                                   w_out[layer].astype(BF16), row(norm2_g[layer]), wr, br, nct)

        slots, n_slots, blk_e, n_used = _route(ri, n_tok)
        xe = _dispatch(h2.reshape(n_tok, d), slots, n_slots)
        ye = _experts(xe, blk_e, n_used, expert_w_gate, expert_w_up, expert_w_down, layer)
        xs = _combine(xs, ye, slots, rw, mod, row(final_norm_g), nct, layer == depth - 1)
    return xs[:, lc:, :]
```

```python
import functools
import math

import jax
import jax.numpy as jnp
from jax import lax
from jax.experimental import pallas as pl
from jax.experimental.pallas import tpu as pltpu

F32 = jnp.float32
BF16 = jnp.bfloat16
HIGHEST = lax.Precision.HIGHEST

EPS = 1e-6
GRID_W = 64
CONV_CH = 256
CONV_WIDTH = 31
GDN_HEADS = 4
GDN_DK = 64
GDN_WIDTH = 256
GDN_QKV = 768
SHORT_CONV = 5
HEAD_DIM = 64
ATTN_Q_HEADS = 8
ATTN_KV_HEADS = 2
ATTN_GROUP = 4
ATTN_WIDTH = 512
ROPE_THETA = 10000.0
N_GROUPS = 4
EXPERTS_PER_GROUP = 8
N_EXPERTS = 32
TOP_K = 2
D_EXPERT = 512

LANES = 128
TM = 256
GDN_C = 128
CONV_HALO = 16
SHORT_HALO = 8
ATT_TQ = 256
ATT_TK = 256
VT_ROWS = 80
MOE_BM = 256
VMEM_LIMIT = 56 * 1024 * 1024

IN_A0 = GDN_QKV
IN_PAD = LANES - 16
C_QKV = (0, 768)
C_AB = (768, 896)
C_KV = (896, 1152)
C_GVGG = (1152, 1664)
C_Z = (1664, 1920)
C_Q = (1920, 2432)
IN_COLS = 2432


def _cparams(sem):
    return pltpu.CompilerParams(dimension_semantics=sem, vmem_limit_bytes=VMEM_LIMIT)


def _silu(x):
    return x * jax.nn.sigmoid(x)


def _dot(a, b, **kw):
    return jnp.dot(a, b, preferred_element_type=F32, **kw)


def _dot_nt(a, b):
    return lax.dot_general(a, b, (((1,), (1,)), ((), ())), preferred_element_type=F32)


def _dot_tn(a, b):
    return lax.dot_general(a, b, (((0,), (0,)), ((), ())), preferred_element_type=F32)


def _mod_kernel(c_ref, w_ref, b_ref, o_ref):
    c = c_ref[...]
    o_ref[0] = _dot(_silu(c), w_ref[0], precision=HIGHEST) + b_ref[0]


def _modulation(cvec, mod_w, mod_b):
    depth, d, n = mod_w.shape
    nr = cvec.shape[0]
    tn = 768
    return pl.pallas_call(
        _mod_kernel,
        grid=(depth, n // tn),
        in_specs=[pl.BlockSpec((nr, d), lambda l, j: (0, 0)),
                  pl.BlockSpec((1, d, tn), lambda l, j: (l, 0, j)),
                  pl.BlockSpec((1, 1, tn), lambda l, j: (l, 0, j))],
        out_specs=pl.BlockSpec((1, nr, tn), lambda l, j: (l, 0, j)),
        out_shape=jax.ShapeDtypeStruct((depth, nr, n), F32),
        compiler_params=_cparams(("parallel", "parallel")),
        name="modulation",
    )(cvec, mod_w, mod_b.reshape(depth, 1, n))


def _rms(x, g):
    return x * lax.rsqrt(jnp.mean(x * x, axis=-1, keepdims=True) + EPS) * g


def _head_sums(sq, bd):
    hi = sq.astype(BF16)
    lo = (sq - hi.astype(F32)).astype(BF16)
    return _dot(hi, bd) + _dot(lo, bd)


def _inproj_kernel(x_ref, mod_ref, g_ref, w_ref, qkv_ref, ab_ref, kv_ref, gvgg_ref, z_ref, q_ref):
    x = x_ref[0]
    sh = mod_ref[0, 0:1, :]
    sc = mod_ref[0, 1:2, :]
    h = (_rms(x, g_ref[...]) * (1.0 + sc) + sh).astype(BF16)
    for ref, (c0, c1) in ((qkv_ref, C_QKV), (ab_ref, C_AB), (kv_ref, C_KV),
                          (gvgg_ref, C_GVGG), (z_ref, C_Z), (q_ref, C_Q)):
        ref[0] = _dot(h, w_ref[:, c0:c1])


def _in_proj(x, mod, g1, w_in_p, nct):
    nb, s, d = x.shape
    nt = s // TM
    widths = [c1 - c0 for c0, c1 in (C_QKV, C_AB, C_KV, C_GVGG, C_Z, C_Q)]
    tok = lambda w: pl.BlockSpec((1, TM, w), lambda b, t: (b, t, 0))
    return pl.pallas_call(
        _inproj_kernel,
        grid=(nb, nt),
        in_specs=[tok(d),
                  pl.BlockSpec((1, 6, d), lambda b, t: (jnp.where(t < nct, nb, b), 0, 0)),
                  pl.BlockSpec((1, d), lambda b, t: (0, 0)),
                  pl.BlockSpec((d, IN_COLS), lambda b, t: (0, 0))],
        out_specs=[tok(w) for w in widths],
        out_shape=[jax.ShapeDtypeStruct((nb, s, w), F32) for w in widths],
        compiler_params=_cparams(("parallel", "parallel")),
        name="in_proj",
    )(x, mod, g1, w_in_p)


def _halo_specs(width, halo, s):
    per = TM // halo
    cur = pl.BlockSpec((1, TM, width), lambda b, t: (b, t, 0))
    prev = pl.BlockSpec((1, halo, width), lambda b, t: (b, jnp.maximum(t * per - 1, 0), 0))
    nxt = pl.BlockSpec((1, halo, width), lambda b, t: (b, jnp.minimum((t + 1) * per, s // halo - 1), 0))
    return [cur, prev, nxt]


def _halo_flags(nct, nt):
    t = pl.program_id(1)
    prev_ok = jnp.logical_and(t != 0, t != nct)
    next_ok = jnp.logical_and(t != nct - 1, t != nt - 1)
    return prev_ok, next_ok


def _conformer_kernel(cur_ref, prev_ref, next_ref, dww_ref, dwb_ref, lng_ref, lnb_ref, pww_ref, pwb_ref,
                      o_ref, ext_ref, *, nct, nt):
    prev_ok, next_ok = _halo_flags(nct, nt)

    def glu(v):
        return v[:, :CONV_CH] * jax.nn.sigmoid(v[:, CONV_CH:])

    ext_ref[0:CONV_HALO, :] = jnp.where(prev_ok, glu(prev_ref[0]), 0.0)
    ext_ref[CONV_HALO:CONV_HALO + TM, :] = glu(cur_ref[0])
    ext_ref[CONV_HALO + TM:2 * CONV_HALO + TM, :] = jnp.where(next_ok, glu(next_ref[0]), 0.0)
    rb = 64
    off = CONV_HALO - CONV_WIDTH // 2
    for r in range(TM // rb):
        acc = jnp.zeros((rb, CONV_CH), F32) + dwb_ref[...]
        for j in range(CONV_WIDTH):
            acc = acc + ext_ref[pl.ds(r * rb + off + j, rb), :] * dww_ref[j:j + 1, :]
        mu = jnp.mean(acc, axis=-1, keepdims=True)
        xc = acc - mu
        y = xc * lax.rsqrt(jnp.mean(xc * xc, axis=-1, keepdims=True) + EPS) * lng_ref[...] + lnb_ref[...]
        h = _silu(y).astype(BF16)
        o_ref[0, r * rb:(r + 1) * rb, :] = _dot(h, pww_ref[...]) + pwb_ref[...]


def _conformer(gvgg, dww, dwb, lng, lnb, pww, pwb, nct):
    nb, s, _ = gvgg.shape
    nt = s // TM
    row = lambda w: pl.BlockSpec((1, w), lambda b, t: (0, 0))
    return pl.pallas_call(
        functools.partial(_conformer_kernel, nct=nct, nt=nt),
        grid=(nb, nt),
        in_specs=_halo_specs(2 * CONV_CH, CONV_HALO, s) + [
            pl.BlockSpec((CONV_WIDTH + 1, CONV_CH), lambda b, t: (0, 0)),
            row(CONV_CH), row(CONV_CH), row(CONV_CH),
            pl.BlockSpec((CONV_CH, CONV_CH), lambda b, t: (0, 0)),
            row(CONV_CH)],
        out_specs=pl.BlockSpec((1, TM, CONV_CH), lambda b, t: (b, t, 0)),
        out_shape=jax.ShapeDtypeStruct((nb, s, CONV_CH), F32),
        scratch_shapes=[pltpu.VMEM((TM + 2 * CONV_HALO, CONV_CH), F32)],
        compiler_params=_cparams(("parallel", "parallel")),
        name="conformer",
    )(gvgg, gvgg, gvgg, dww, dwb, lng, lnb, pww, pwb)


def _gdn_feat_kernel(cur_ref, prev_ref, next_ref, cw_ref, ab_ref, alog_ref, dtb_ref,
                     bd_ref, q_ref, k_ref, v_ref, gb_ref, ext_ref, y_ref, *, nct, nt):
    prev_ok, next_ok = _halo_flags(nct, nt)
    ext_ref[0:SHORT_HALO, :] = jnp.where(prev_ok, prev_ref[0], 0.0)
    ext_ref[SHORT_HALO:SHORT_HALO + TM, :] = cur_ref[0]
    ext_ref[SHORT_HALO + TM:2 * SHORT_HALO + TM, :] = jnp.where(next_ok, next_ref[0], 0.0)
    rb = 32
    off = SHORT_HALO - SHORT_CONV // 2
    for r in range(TM // rb):
        acc = jnp.zeros((rb, GDN_QKV), F32)
        for j in range(SHORT_CONV):
            acc = acc + ext_ref[pl.ds(r * rb + off + j, rb), :] * cw_ref[j:j + 1, :]
        y_ref[r * rb:(r + 1) * rb, :] = _silu(acc)

    def l2n(x):
        return x * lax.rsqrt(_head_sums(x * x, bd_ref[...]) + EPS)

    qn = l2n(y_ref[:, 0:GDN_WIDTH]) * (GDN_DK ** -0.5)
    kn = l2n(y_ref[:, GDN_WIDTH:2 * GDN_WIDTH])
    v = y_ref[:, 2 * GDN_WIDTH:]
    for h in range(GDN_HEADS):
        q_ref[0, h] = qn[:, h * GDN_DK:(h + 1) * GDN_DK]
        k_ref[0, h] = kn[:, h * GDN_DK:(h + 1) * GDN_DK]
        v_ref[0, h] = v[:, h * GDN_DK:(h + 1) * GDN_DK]
    ab = ab_ref[0]
    lane = lax.broadcasted_iota(jnp.int32, ab.shape, 1)
    xa = ab + dtb_ref[...]
    softplus = jnp.maximum(xa, 0.0) + jnp.log(1.0 + jnp.exp(-jnp.abs(xa)))
    g = -jnp.exp(alog_ref[...]) * softplus
    beta = jax.nn.sigmoid(ab)
    gb_ref[0] = jnp.where(lane < 2 * GDN_HEADS, g, jnp.where(lane < 4 * GDN_HEADS, beta, 0.0))


def _gdn_features(qkv, ab, cw, alog, dtb, bd, nct):
    nb, s, _ = qkv.shape
    nt = s // TM
    head = pl.BlockSpec((1, GDN_HEADS, TM, GDN_DK), lambda b, t: (b, 0, t, 0))
    hshape = jax.ShapeDtypeStruct((nb, GDN_HEADS, s, GDN_DK), F32)
    return pl.pallas_call(
        functools.partial(_gdn_feat_kernel, nct=nct, nt=nt),
        grid=(nb, nt),
        in_specs=_halo_specs(GDN_QKV, SHORT_HALO, s) + [
            pl.BlockSpec((8, GDN_QKV), lambda b, t: (0, 0)),
            pl.BlockSpec((1, TM, LANES), lambda b, t: (b, t, 0)),
            pl.BlockSpec((1, LANES), lambda b, t: (0, 0)),
            pl.BlockSpec((1, LANES), lambda b, t: (0, 0)),
            pl.BlockSpec((GDN_WIDTH, GDN_WIDTH), lambda b, t: (0, 0))],
        out_specs=[head, head, head, pl.BlockSpec((1, TM, LANES), lambda b, t: (b, t, 0))],
        out_shape=[hshape, hshape, hshape, jax.ShapeDtypeStruct((nb, s, LANES), F32)],
        scratch_shapes=[pltpu.VMEM((TM + 2 * SHORT_HALO, GDN_QKV), F32), pltpu.VMEM((TM, GDN_QKV), F32)],
        compiler_params=_cparams(("parallel", "parallel")),
        name="gdn_features",
    )(qkv, qkv, qkv, cw, ab, alog, dtb, bd)


def _gdn_scan_kernel(qf_ref, kf_ref, vf_ref, gf_ref, qb_ref, kb_ref, vb_ref, gbk_ref,
                     of_ref, ob_ref, s_ref):
    n = pl.program_id(1)

    @pl.when(n == 0)
    def _():
        s_ref[...] = jnp.zeros(s_ref.shape, F32)

    c = GDN_C
    row = lax.broadcasted_iota(jnp.int32, (c, c), 0)
    col = lax.broadcasted_iota(jnp.int32, (c, c), 1)
    eye = (row == col).astype(F32)
    blk = {2 ** e: (row >> e) == (col >> e) for e in range(1, int(math.log2(c)) + 1)}

    chains = []
    for d, (q_ref, k_ref, v_ref, g_ref, o_ref) in enumerate(
            ((qf_ref, kf_ref, vf_ref, gf_ref, of_ref), (qb_ref, kb_ref, vb_ref, gbk_ref, ob_ref))):
        incl = (row >= col) if d == 0 else (row <= col)
        strict = (row > col) if d == 0 else (row < col)
        gb = g_ref[0]
        gam_all = _dot(incl.astype(F32), gb, precision=HIGHEST)
        gam_t = gam_all.T
        last_row = c - 1 if d == 0 else 0
        for h in range(GDN_HEADS):
            cc = d * GDN_HEADS + h
            gam = gam_all[:, cc:cc + 1]
            beta = gb[:, 2 * GDN_HEADS + cc:2 * GDN_HEADS + cc + 1]
            kh = k_ref[0, h]
            chains.append(dict(
                d=d, h=h, o_ref=o_ref, strict=strict, gam=gam, beta=beta, kh=kh, qh=q_ref[0, h], vh=v_ref[0, h],
                last=gam_all[last_row:last_row + 1, cc:cc + 1],
                decay=jnp.exp(jnp.where(incl, gam - gam_t[cc:cc + 1, :], -1e30)),
                kbeta=kh * beta))
    for ch in chains:
        ch["a"] = jnp.where(ch["strict"], _dot_nt(ch["kbeta"], ch["kh"]) * ch["decay"], 0.0)
    for ch in chains:
        ch["t"] = eye - jnp.where(blk[2], ch["a"], 0.0)
    bsz = 2
    while bsz < c:
        off = jnp.logical_and(blk[2 * bsz], jnp.logical_not(blk[bsz]))
        for ch in chains:
            ch["tl"] = _dot(ch["t"], jnp.where(off, ch["a"], 0.0))
        for ch in chains:
            ch["t"] = ch["t"] - _dot(ch["tl"], ch["t"])
        bsz *= 2
    for ch in chains:
        rhs = jnp.concatenate([ch["vh"] * ch["beta"], ch["kbeta"] * jnp.exp(ch["gam"])], axis=1)
        ch["uw"] = _dot(ch["t"], rhs)
    for ch in chains:
        ch["pm"] = _dot_nt(ch["qh"], ch["kh"]) * ch["decay"]
    for ch in chains:
        ch["s"] = s_ref[ch["d"], ch["h"]]
        ch["v_new"] = ch["uw"][:, :GDN_DK] - _dot(ch["uw"][:, GDN_DK:], ch["s"])
    for ch in chains:
        h = ch["h"]
        qg = ch["qh"] * jnp.exp(ch["gam"])
        ch["o_ref"][0, :, h * GDN_DK:(h + 1) * GDN_DK] = _dot(qg, ch["s"]) + _dot(ch["pm"], ch["v_new"])
    for ch in chains:
        kd = ch["kh"] * jnp.exp(ch["last"] - ch["gam"])
        s_ref[ch["d"], ch["h"]] = ch["s"] * jnp.exp(ch["last"]) + _dot_tn(kd, ch["v_new"])


def _gdn_scan(q, k, v, gb, ncc):
    nb, _, s, _ = q.shape
    nc = s // GDN_C

    def bwd(n):
        return jnp.where(n < ncc, ncc - 1 - n, nc + ncc - 1 - n)

    hf = pl.BlockSpec((1, GDN_HEADS, GDN_C, GDN_DK), lambda b, n: (b, 0, n, 0))
    hb = pl.BlockSpec((1, GDN_HEADS, GDN_C, GDN_DK), lambda b, n: (b, 0, bwd(n), 0))
    gf = pl.BlockSpec((1, GDN_C, LANES), lambda b, n: (b, n, 0))
    gk = pl.BlockSpec((1, GDN_C, LANES), lambda b, n: (b, bwd(n), 0))
    of = pl.BlockSpec((1, GDN_C, GDN_WIDTH), lambda b, n: (b, n, 0))
    ob = pl.BlockSpec((1, GDN_C, GDN_WIDTH), lambda b, n: (b, bwd(n), 0))
    oshape = jax.ShapeDtypeStruct((nb, s, GDN_WIDTH), F32)
    return pl.pallas_call(
        _gdn_scan_kernel,
        grid=(nb, nc),
        in_specs=[hf, hf, hf, gf, hb, hb, hb, gk],
        out_specs=[of, ob],
        out_shape=[oshape, oshape],
        scratch_shapes=[pltpu.VMEM((2, GDN_HEADS, GDN_DK, GDN_DK), F32)],
        compiler_params=_cparams(("parallel", "arbitrary")),
        name="gdn_scan",
    )(q, k, v, gb, q, k, v, gb)


def _attn_prep_kernel(q_ref, kv_ref, cos_ref, sin_ref, qg_ref, kg_ref, bd_ref,
                      qt_ref, kh_ref, vt_ref, *, q_scale):
    cos = cos_ref[...]
    sin = sin_ref[...]
    bd = bd_ref[...]
    lane = lax.broadcasted_iota(jnp.int32, cos.shape, 1)
    first = (lane % 32) < 16

    def norm_rope(x, g):
        ss = _head_sums(x * x, bd)
        y = x * lax.rsqrt(ss * (1.0 / HEAD_DIM) + EPS) * g
        swapped = jnp.where(first, pltpu.roll(y, LANES - 16, 1), pltpu.roll(y, 16, 1))
        return y * cos + swapped * sin

    q = q_ref[0]
    for j in range(ATTN_WIDTH // LANES):
        yt = (norm_rope(q[:, j * LANES:(j + 1) * LANES], qg_ref[...]) * q_scale).T
        qt_ref[0, 2 * j] = yt[:HEAD_DIM].astype(BF16)
        qt_ref[0, 2 * j + 1] = yt[HEAD_DIM:].astype(BF16)
    kv = kv_ref[0]
    y = norm_rope(kv[:, :LANES], kg_ref[...])
    kh_ref[0, 0] = y[:, :HEAD_DIM].astype(BF16)
    kh_ref[0, 1] = y[:, HEAD_DIM:].astype(BF16)
    vt = kv[:, LANES:].T.astype(BF16)
    ones = jnp.ones((VT_ROWS - HEAD_DIM, ATT_TK), BF16)
    for g in range(ATTN_KV_HEADS):
        for cidx in range(TM // ATT_TK):
            vt_ref[0, g, cidx, 0:HEAD_DIM, :] = vt[g * HEAD_DIM:(g + 1) * HEAD_DIM,
                                                   cidx * ATT_TK:(cidx + 1) * ATT_TK]
            vt_ref[0, g, cidx, HEAD_DIM:VT_ROWS, :] = ones


def _attn_prep(q, kv, cos_t, sin_t, qg, kg, bd, q_scale):
    nb, s, _ = q.shape
    nt = s // TM
    tab = pl.BlockSpec((TM, LANES), lambda b, t: (t, 0))
    row = pl.BlockSpec((1, LANES), lambda b, t: (0, 0))
    per = TM // ATT_TK
    return pl.pallas_call(
        functools.partial(_attn_prep_kernel, q_scale=q_scale),
        grid=(nb, nt),
        in_specs=[pl.BlockSpec((1, TM, ATTN_WIDTH), lambda b, t: (b, t, 0)),
                  pl.BlockSpec((1, TM, 2 * LANES), lambda b, t: (b, t, 0)),
                  tab, tab, row, row,
                  pl.BlockSpec((LANES, LANES), lambda b, t: (0, 0))],
        out_specs=[pl.BlockSpec((1, ATTN_Q_HEADS, HEAD_DIM, TM), lambda b, t: (b, 0, 0, t)),
                   pl.BlockSpec((1, ATTN_KV_HEADS, TM, HEAD_DIM), lambda b, t: (b, 0, t, 0)),
                   pl.BlockSpec((1, ATTN_KV_HEADS, per, VT_ROWS, ATT_TK), lambda b, t: (b, 0, t, 0, 0))],
        out_shape=[jax.ShapeDtypeStruct((nb, ATTN_Q_HEADS, HEAD_DIM, s), BF16),
                   jax.ShapeDtypeStruct((nb, ATTN_KV_HEADS, s, HEAD_DIM), BF16),
                   jax.ShapeDtypeStruct((nb, ATTN_KV_HEADS, s // ATT_TK, VT_ROWS, ATT_TK), BF16)],
        compiler_params=_cparams(("parallel", "parallel")),
        name="attn_prep",
    )(q, kv, cos_t, sin_t, qg, kg, bd)


def _attn_kernel(qt_ref, k_ref, vt_ref, o_ref, m_ref, acc_ref, s_ref, mx_ref, *, nctq, lc, s_len):
    qi = pl.program_id(2)
    m_ref[...] = jnp.full(m_ref.shape, -jnp.inf, F32)
    acc_ref[...] = jnp.zeros(acc_ref.shape, F32)
    nk = jnp.where(qi < nctq, lc // ATT_TK, s_len // ATT_TK)

    def score(j, slot):
        ks = pl.multiple_of(j * ATT_TK, ATT_TK)
        k = k_ref[0, 0, pl.ds(ks, ATT_TK), :]
        for h in range(ATTN_GROUP):
            s = _dot(k, qt_ref[0, h])
            s_ref[slot, h] = s
            mx_ref[slot, h] = jnp.max(s, axis=0, keepdims=True)

    def step(j, slot):
        score(jnp.minimum(j + 1, nk - 1), 1 - slot)
        vt = vt_ref[0, 0, j]
        for h in range(ATTN_GROUP):
            s = s_ref[slot, h]
            m_old = m_ref[h]
            m_new = jnp.maximum(m_old, mx_ref[slot, h])
            alpha = jnp.exp2(m_old - m_new)
            p = jnp.exp2(s - m_new)
            acc_ref[h] = alpha * acc_ref[h] + _dot(vt, p.astype(BF16))
            m_ref[h] = m_new

    def pair(i, carry):
        step(2 * i, 0)
        step(2 * i + 1, 1)
        return carry

    score(0, 0)
    lax.fori_loop(0, nk // 2, pair, 0)

    @pl.when(nk % 2 == 1)
    def _():
        step(nk - 1, 0)

    for hp in range(ATTN_GROUP // 2):
        pair = jnp.concatenate(
            [acc_ref[h, 0:HEAD_DIM, :] / acc_ref[h, HEAD_DIM:HEAD_DIM + 1, :] for h in (2 * hp, 2 * hp + 1)], axis=0)
        o_ref[0, :, hp * LANES:(hp + 1) * LANES] = pair.T


def _attention(qt, kh, vt, lc):
    nb, _, _, s = qt.shape
    nq = s // ATT_TQ
    return pl.pallas_call(
        functools.partial(_attn_kernel, nctq=lc // ATT_TQ, lc=lc, s_len=s),
        grid=(nb, ATTN_KV_HEADS, nq),
        in_specs=[pl.BlockSpec((1, ATTN_GROUP, HEAD_DIM, ATT_TQ), lambda b, g, i: (b, g, 0, i)),
                  pl.BlockSpec((1, 1, s, HEAD_DIM), lambda b, g, i: (b, g, 0, 0)),
                  pl.BlockSpec((1, 1, s // ATT_TK, VT_ROWS, ATT_TK), lambda b, g, i: (b, g, 0, 0, 0))],
        out_specs=pl.BlockSpec((1, ATT_TQ, ATTN_GROUP * HEAD_DIM), lambda b, g, i: (b, i, g)),
        out_shape=jax.ShapeDtypeStruct((nb, s, ATTN_WIDTH), F32),
        scratch_shapes=[pltpu.VMEM((ATTN_GROUP, 1, ATT_TQ), F32),
                        pltpu.VMEM((ATTN_GROUP, VT_ROWS, ATT_TQ), F32),
                        pltpu.VMEM((2, ATTN_GROUP, ATT_TK, ATT_TQ), F32),
                        pltpu.VMEM((2, ATTN_GROUP, 1, ATT_TQ), F32)],
        compiler_params=_cparams(("parallel", "parallel", "parallel")),
        name="attention",
    )(qt, kh, vt)


def _outproj_kernel(conv_ref, of_ref, ob_ref, z_ref, attn_ref, x_ref, mod_ref, gg_ref, bd_ref, w_ref,
                    n2_ref, wr_ref, br_ref, xo_ref, h2_ref, ri_ref, rw_ref):
    o = of_ref[0] + ob_ref[0]
    ss = _head_sums(o * o, bd_ref[...])
    gdn = o * lax.rsqrt(ss * (1.0 / GDN_DK) + EPS) * gg_ref[...] * _silu(z_ref[0])
    y = _dot(conv_ref[0].astype(BF16), w_ref[0:CONV_CH, :])
    y = y + _dot(gdn.astype(BF16), w_ref[CONV_CH:CONV_CH + GDN_WIDTH, :])
    y = y + _dot(attn_ref[0].astype(BF16), w_ref[CONV_CH + GDN_WIDTH:, :])
    x = x_ref[0] + mod_ref[0, 2:3, :] * y
    xo_ref[0] = x
    h2 = _rms(x, n2_ref[...]) * (1.0 + mod_ref[0, 4:5, :]) + mod_ref[0, 3:4, :]
    h2_ref[0] = h2
    h_hi = h2.astype(BF16)
    h_lo = (h2 - h_hi.astype(F32)).astype(BF16)
    t = _dot(h_hi, wr_ref[...])
    lg = t[:, :LANES] + t[:, LANES:] + _dot(h_lo, wr_ref[:, 0:LANES]) + br_ref[...]
    lane = lax.broadcasted_iota(jnp.int32, lg.shape, 1)
    lane_f = lane.astype(F32)

    def first_max(v):
        mx = jnp.max(v, axis=-1, keepdims=True)
        return mx, jnp.min(jnp.where(v == mx, lane_f, float(LANES)), axis=-1, keepdims=True)

    lgg = jnp.where(lane < N_GROUPS, lg, -jnp.inf)
    gmax, grp = first_max(lgg)
    pg_sel = 1.0 / jnp.sum(jnp.exp(lgg - gmax), axis=-1, keepdims=True)
    in_grp = jnp.logical_and(lane >= N_GROUPS, ((lane - N_GROUPS) >> 3).astype(F32) == grp)
    le = jnp.where(in_grp, lg, -jnp.inf)
    e1max, i1 = first_max(le)
    e2max, i2 = first_max(jnp.where(lane_f == i1, -jnp.inf, le))
    r = jnp.exp(e2max - e1max)
    w1 = pg_sel / (1.0 + r)
    ri_ref[0] = jnp.where(lane == 0, i1, jnp.where(lane == 1, i2, float(N_GROUPS))).astype(jnp.int32) - N_GROUPS
    rw_ref[0] = jnp.where(lane == 0, w1, jnp.where(lane == 1, w1 * r, 0.0))


def _out_proj(conv, o_f, o_b, z, attn, x, mod, gdn_g, bd, w_out, n2, wr, br, nct):
    nb, s, d = x.shape
    nt = s // TM
    tok = lambda w: pl.BlockSpec((1, TM, w), lambda b, t: (b, t, 0))
    full = lambda a: pl.BlockSpec(a.shape, lambda b, t: (0,) * a.ndim)
    return pl.pallas_call(
        _outproj_kernel,
        grid=(nb, nt),
        in_specs=[tok(CONV_CH), tok(GDN_WIDTH), tok(GDN_WIDTH), tok(GDN_WIDTH), tok(ATTN_WIDTH), tok(d),
                  pl.BlockSpec((1, 6, d), lambda b, t: (jnp.where(t < nct, nb, b), 0, 0)),
                  full(gdn_g), full(bd), full(w_out), full(n2), full(wr), full(br)],
        out_specs=[tok(d), tok(d), tok(LANES), tok(LANES)],
        out_shape=[jax.ShapeDtypeStruct((nb, s, d), F32), jax.ShapeDtypeStruct((nb, s, d), F32),
                   jax.ShapeDtypeStruct((nb, s, LANES), jnp.int32), jax.ShapeDtypeStruct((nb, s, LANES), F32)],
        compiler_params=_cparams(("parallel", "parallel")),
        name="out_proj",
    )(conv, o_f, o_b, z, attn, x, mod, gdn_g, bd, w_out, n2, wr, br)


def _dispatch_kernel(slot_ref, h_ref, xe_in_ref, xe_ref, sem):
    del xe_in_ref
    base = pl.program_id(0) * (TM * TOP_K)

    def row_copy(r, k):
        dst = slot_ref[base + TOP_K * r + k]
        return pltpu.make_async_copy(h_ref.at[pl.ds(r, 1)], xe_ref.at[pl.ds(dst, 1)], sem)

    def issue(r, carry):
        for k in range(TOP_K):
            row_copy(r, k).start()
        return carry

    lax.fori_loop(0, TM, issue, 0, unroll=8)
    for _ in range(TOP_K):
        pltpu.make_async_copy(h_ref, xe_ref.at[pl.ds(0, TM)], sem).wait()


def _dispatch(h2, slots, n_slots):
    n_tok, d = h2.shape
    grid_spec = pltpu.PrefetchScalarGridSpec(
        num_scalar_prefetch=1,
        grid=(n_tok // TM,),
        in_specs=[pl.BlockSpec((TM, d), lambda i, sl: (i, 0)),
                  pl.BlockSpec(memory_space=pl.ANY)],
        out_specs=pl.BlockSpec(memory_space=pl.ANY),
        scratch_shapes=[pltpu.SemaphoreType.DMA(())],
    )
    return pl.pallas_call(
        _dispatch_kernel,
        grid_spec=grid_spec,
        out_shape=jax.ShapeDtypeStruct((n_slots, d), F32),
        input_output_aliases={2: 0},
        compiler_params=_cparams(("arbitrary",)),
        name="moe_dispatch",
    )(slots, h2, jnp.zeros((n_slots, d), F32))


def _expert_kernel(be_ref, nu_ref, x_ref, wg_ref, wu_ref, wd_ref, o_ref, wg_s, wu_s, wd_s):
    i = pl.program_id(0)
    used = i < nu_ref[0]
    new_expert = jnp.logical_or(i == 0, be_ref[i] != be_ref[jnp.maximum(i - 1, 0)])

    @pl.when(jnp.logical_and(used, new_expert))
    def _():
        wg_s[...] = wg_ref[0, 0].astype(BF16)
        wu_s[...] = wu_ref[0, 0].astype(BF16)
        wd_s[...] = wd_ref[0, 0].astype(BF16)

    @pl.when(used)
    def _():
        x = x_ref[...].astype(BF16)
        g = _dot(x, wg_s[...])
        u = _dot(x, wu_s[...])
        h = (_silu(g) * u).astype(BF16)
        o_ref[...] = _dot(h, wd_s[...])

    @pl.when(i >= nu_ref[0])
    def _():
        o_ref[...] = jnp.zeros(o_ref.shape, F32)


def _experts(xs, blk_e, n_used, w_gate, w_up, w_down, layer):
    n_slots, d = xs.shape
    nblk = n_slots // MOE_BM
    de = w_gate.shape[-1]
    grid_spec = pltpu.PrefetchScalarGridSpec(
        num_scalar_prefetch=2,
        grid=(nblk,),
        in_specs=[pl.BlockSpec((MOE_BM, d), lambda i, be, nu: (jnp.minimum(i, nu[0] - 1), 0)),
                  pl.BlockSpec((1, 1, d, de), lambda i, be, nu: (layer, be[i], 0, 0)),
                  pl.BlockSpec((1, 1, d, de), lambda i, be, nu: (layer, be[i], 0, 0)),
                  pl.BlockSpec((1, 1, de, d), lambda i, be, nu: (layer, be[i], 0, 0))],
        out_specs=pl.BlockSpec((MOE_BM, d), lambda i, be, nu: (i, 0)),
        scratch_shapes=[pltpu.VMEM((d, de), BF16), pltpu.VMEM((d, de), BF16), pltpu.VMEM((de, d), BF16)],
    )
    return pl.pallas_call(
        _expert_kernel,
        grid_spec=grid_spec,
        out_shape=jax.ShapeDtypeStruct((n_slots, d), F32),
        compiler_params=_cparams(("arbitrary",)),
        name="experts",
    )(blk_e, n_used, xs, w_gate, w_up, w_down)


def _combine_kernel(slot_ref, x_ref, rw_ref, mod_ref, fg_ref, ye_ref, o_ref, buf_ref, sem, *, final, nt):
    base = (pl.program_id(0) * nt + pl.program_id(1)) * (TM * TOP_K)

    def row_copy(r, k):
        src = slot_ref[base + TOP_K * r + k]
        return pltpu.make_async_copy(ye_ref.at[pl.ds(src, 1)], buf_ref.at[k, pl.ds(r, 1)], sem.at[k])

    def issue(r, carry):
        for k in range(TOP_K):
            row_copy(r, k).start()
        return carry

    lax.fori_loop(0, TM, issue, 0, unroll=8)
    for k in range(TOP_K):
        pltpu.make_async_copy(ye_ref.at[pl.ds(0, TM)], buf_ref.at[k], sem.at[k]).wait()
    rw = rw_ref[0]
    y = rw[:, 0:1] * buf_ref[0] + rw[:, 1:2] * buf_ref[1]
    x = x_ref[0] + mod_ref[0, 5:6, :] * y
    if final:
        x = _rms(x, fg_ref[...])
    o_ref[0] = x


def _combine(x, ye, slots, rw, mod, fg, nct, final):
    nb, s, d = x.shape
    nt = s // TM
    tok = lambda w: pl.BlockSpec((1, TM, w), lambda b, t, sl: (b, t, 0))
    grid_spec = pltpu.PrefetchScalarGridSpec(
        num_scalar_prefetch=1,
        grid=(nb, nt),
        in_specs=[tok(d), tok(LANES),
                  pl.BlockSpec((1, 6, d), lambda b, t, sl: (jnp.where(t < nct, nb, b), 0, 0)),
                  pl.BlockSpec((1, d), lambda b, t, sl: (0, 0)),
                  pl.BlockSpec(memory_space=pl.ANY)],
        out_specs=tok(d),
        scratch_shapes=[pltpu.VMEM((TOP_K, TM, d), F32), pltpu.SemaphoreType.DMA((TOP_K,))],
    )
    return pl.pallas_call(
        functools.partial(_combine_kernel, final=final, nt=nt),
        grid_spec=grid_spec,
        out_shape=jax.ShapeDtypeStruct((nb, s, d), F32),
        compiler_params=_cparams(("arbitrary", "arbitrary")),
        name="moe_combine",
    )(slots, x, rw, mod, fg, ye)


def _rank_kernel(ri_ref, rank_ref, cnt_ref, carry_ref):
    @pl.when(pl.program_id(0) == 0)
    def _():
        carry_ref[...] = jnp.zeros(carry_ref.shape, F32)

    ri = ri_ref[0]
    lane = lax.broadcasted_iota(jnp.int32, ri.shape, 1)
    oh1 = lane == ri[:, 0:1]
    oh2 = lane == ri[:, 1:2]
    oh = jnp.logical_or(oh1, oh2).astype(F32)
    row = lax.broadcasted_iota(jnp.int32, (TM, TM), 0)
    col = lax.broadcasted_iota(jnp.int32, (TM, TM), 1)
    before = carry_ref[0:1, :] + _dot((row > col).astype(BF16), oh.astype(BF16))
    r1 = jnp.sum(jnp.where(oh1, before, 0.0), axis=-1, keepdims=True)
    r2 = jnp.sum(jnp.where(oh2, before, 0.0), axis=-1, keepdims=True)
    rank_ref[0] = jnp.where(lane == 0, r1, jnp.where(lane == 1, r2, 0.0)).astype(jnp.int32)
    carry_ref[...] = carry_ref[...] + jnp.sum(oh, axis=0, keepdims=True)
    cnt_ref[...] = carry_ref[...]


def _slot_kernel(ri_ref, rank_ref, ps_ref, slot_ref):
    ri = ri_ref[0]
    rk = rank_ref[0]
    lane = lax.broadcasted_iota(jnp.int32, ri.shape, 1)
    ps = ps_ref[...]
    s1 = jnp.sum(jnp.where(lane == ri[:, 0:1], ps, 0.0), axis=-1, keepdims=True).astype(jnp.int32) + rk[:, 0:1]
    s2 = jnp.sum(jnp.where(lane == ri[:, 1:2], ps, 0.0), axis=-1, keepdims=True).astype(jnp.int32) + rk[:, 1:2]
    slot_ref[0] = jnp.where(lane == 0, s1, jnp.where(lane == 1, s2, 0))


def _route(ri, n_tok):
    ntile = n_tok // TM
    ri = ri.reshape(ntile, TM, LANES)
    tile = pl.BlockSpec((1, TM, LANES), lambda i: (i, 0, 0))
    rank, cnt = pl.pallas_call(
        _rank_kernel,
        grid=(ntile,),
        in_specs=[tile],
        out_specs=[tile, pl.BlockSpec((8, LANES), lambda i: (0, 0))],
        out_shape=[jax.ShapeDtypeStruct((ntile, TM, LANES), jnp.int32), jax.ShapeDtypeStruct((8, LANES), F32)],
        scratch_shapes=[pltpu.VMEM((8, LANES), F32)],
        compiler_params=_cparams(("arbitrary",)),
        name="moe_rank",
    )(ri)
    counts = cnt[0, :N_EXPERTS].astype(jnp.int32)
    pcounts = (counts + MOE_BM - 1) // MOE_BM * MOE_BM
    pends = jnp.cumsum(pcounts)
    pstarts = jnp.zeros((1, LANES), F32).at[0, :N_EXPERTS].set((pends - pcounts).astype(F32))
    slot = pl.pallas_call(
        _slot_kernel,
        grid=(ntile,),
        in_specs=[tile, tile, pl.BlockSpec((1, LANES), lambda i: (0, 0))],
        out_specs=tile,
        out_shape=jax.ShapeDtypeStruct((ntile, TM, LANES), jnp.int32),
        compiler_params=_cparams(("parallel",)),
        name="moe_slot",
    )(ri, rank, pstarts)
    slots = slot.reshape(n_tok, LANES)[:, :TOP_K].reshape(-1)
    n_slots = (-(-n_tok * TOP_K // MOE_BM) + N_EXPERTS) * MOE_BM
    blk_start = jnp.arange(n_slots // MOE_BM, dtype=jnp.int32) * MOE_BM
    blk_e = jnp.minimum(jnp.sum((pends[None, :] <= blk_start[:, None]).astype(jnp.int32), axis=1), N_EXPERTS - 1)
    n_used = (pends[-1] // MOE_BM).astype(jnp.int32).reshape(1)
    return slots, n_slots, blk_e, n_used


def _rope_tables(lc, l):
    rows = l // GRID_W
    row = jnp.repeat(jnp.arange(rows), GRID_W).astype(F32)
    col = (jnp.arange(rows * GRID_W) % GRID_W).astype(F32)
    n_freq = HEAD_DIM // 4
    inv = ROPE_THETA ** (-jnp.arange(n_freq, dtype=F32) / n_freq)
    ar, ac = row[:, None] * inv, col[:, None] * inv
    cos = jnp.concatenate([jnp.cos(ar), jnp.cos(ar), jnp.cos(ac), jnp.cos(ac)], axis=-1)
    sin = jnp.concatenate([-jnp.sin(ar), jnp.sin(ar), -jnp.sin(ac), jnp.sin(ac)], axis=-1)
    cos = jnp.concatenate([jnp.ones((lc, HEAD_DIM), F32), cos], axis=0)
    sin = jnp.concatenate([jnp.zeros((lc, HEAD_DIM), F32), sin], axis=0)
    return jnp.tile(cos, (1, 2)), jnp.tile(sin, (1, 2))


def _block_diag_ones(n, blk):
    i = jnp.arange(n)
    return (i[:, None] // blk == i[None, :] // blk).astype(BF16)


def kernel(x, c, ctx, c_ctx, mod_w, mod_b, norm1_g, norm2_g, w_in, conv_dw_w, conv_dw_b, conv_ln_g, conv_ln_b, conv_pw_w, conv_pw_b, gdn_conv_w, gdn_a_log, gdn_dt_bias, gdn_norm_g, attn_q_norm_g, attn_k_norm_g, w_out, router_group_w, router_group_b, router_expert_w, router_expert_b, expert_w_gate, expert_w_up, expert_w_down, final_norm_g):
    nb, l, d = x.shape
    lc = ctx.shape[1]
    depth = mod_w.shape[0]
    s = lc + l
    assert lc % TM == 0 and l % TM == 0 and lc % GDN_C == 0 and l % GDN_C == 0
    assert lc % ATT_TK == 0 and s % ATT_TK == 0 and lc % ATT_TQ == 0
    nct = lc // TM
    n_tok = nb * s

    nr = -(-(nb + 1) // 8) * 8
    cvec = jnp.zeros((nr, d), F32).at[:nb].set(c).at[nb].set(c_ctx)
    mod_all = _modulation(cvec, mod_w, mod_b).reshape(depth, nr, 6, d)

    cos_t, sin_t = _rope_tables(lc, l)
    bd64 = _block_diag_ones(LANES, HEAD_DIM)
    bd_gdn = _block_diag_ones(GDN_WIDTH, GDN_DK)
    q_scale = (HEAD_DIM ** -0.5) * math.log2(math.e)
    row = lambda v: v.reshape(1, -1).astype(F32)
    pad_lanes = lambda v: jnp.zeros((1, LANES), F32).at[0, :v.size].set(v.reshape(-1))

    xs = jnp.concatenate([ctx, x], axis=1)
    for layer in range(depth):
        mod = mod_all[layer]
        wi = w_in[layer]
        w_in_p = jnp.concatenate(
            [wi[:, :IN_A0 + 16], jnp.zeros((d, IN_PAD), F32), wi[:, IN_A0 + 16:]], axis=1).astype(BF16)
        qkv, ab, kv, gvgg, z, q = _in_proj(xs, mod, row(norm1_g[layer]), w_in_p, nct)

        dww = jnp.concatenate([conv_dw_w[layer], jnp.zeros((1, CONV_CH), F32)], axis=0)
        conv = _conformer(gvgg, dww, row(conv_dw_b[layer]), row(conv_ln_g[layer]), row(conv_ln_b[layer]),
                          conv_pw_w[layer].astype(BF16), row(conv_pw_b[layer]), nct)

        cw = jnp.concatenate([gdn_conv_w[layer], jnp.zeros((8 - SHORT_CONV, GDN_QKV), F32)], axis=0)
        gq, gk, gv, gb = _gdn_features(qkv, ab, cw, pad_lanes(gdn_a_log[layer]), pad_lanes(gdn_dt_bias[layer]),
                                       bd_gdn, nct)
        o_f, o_b = _gdn_scan(gq, gk, gv, gb, lc // GDN_C)

        qt, kh, vt = _attn_prep(q, kv, cos_t, sin_t,
                                row(jnp.tile(attn_q_norm_g[layer], 2)), row(jnp.tile(attn_k_norm_g[layer], 2)),
                                bd64, q_scale)
        attn = _attention(qt, kh, vt, lc)

        wr = jnp.zeros((d, LANES), F32).at[:, :N_GROUPS].set(router_group_w[layer])
        wr = wr.at[:, N_GROUPS:N_GROUPS + N_EXPERTS].set(router_expert_w[layer])
        wr_hi = wr.astype(BF16)
        wr = jnp.concatenate([wr_hi, (wr - wr_hi.astype(F32)).astype(BF16)], axis=1)
        br =jnp.zeros((1, LANES), F32).at[0, :N_GROUPS].set(router_group_b[layer])
        br = br.at[0, N_GROUPS:N_GROUPS + N_EXPERTS].set(router_expert_b[layer])
        xs, h2, ri, rw = _out_proj(conv, o_f, o_b, z, attn, xs, mod,
                                   row(jnp.tile(gdn_norm_g[layer], GDN_HEADS)), bd_gdn,
                                   w_out[layer].astype(BF16), row(norm2_g[layer]), wr, br, nct)

        slots, n_slots, blk_e, n_used = _route(ri, n_tok)
        xe = _dispatch(h2.reshape(n_tok, d), slots, n_slots)
        ye = _experts(xe, blk_e, n_used, expert_w_gate, expert_w_up, expert_w_down, layer)
        xs = _combine(xs, ye, slots, rw, mod, row(final_norm_g), nct, layer == depth - 1)
    return xs[:, lc:, :]
```

```python
import functools
import math

import jax
import jax.numpy as jnp
from jax import lax
from jax.experimental import pallas as pl
from jax.experimental.pallas import tpu as pltpu

F32 = jnp.float32
BF16 = jnp.bfloat16
HIGHEST = lax.Precision.HIGHEST

EPS = 1e-6
GRID_W = 64
CONV_CH = 256
CONV_WIDTH = 31
GDN_HEADS = 4
GDN_DK = 64
GDN_WIDTH = 256
GDN_QKV = 768
SHORT_CONV = 5
HEAD_DIM = 64
ATTN_Q_HEADS = 8
ATTN_KV_HEADS = 2
ATTN_GROUP = 4
ATTN_WIDTH = 512
ROPE_THETA = 10000.0
N_GROUPS = 4
EXPERTS_PER_GROUP = 8
N_EXPERTS = 32
TOP_K = 2
D_EXPERT = 512

LANES = 128
TM = 256
GDN_C = 128
CONV_HALO = 16
SHORT_HALO = 8
ATT_TQ = 256
ATT_TK = 256
VT_ROWS = 80
MOE_BM = 256
VMEM_LIMIT = 56 * 1024 * 1024

IN_A0 = GDN_QKV
IN_PAD = LANES - 16
C_QKV = (0, 768)
C_AB = (768, 896)
C_KV = (896, 1152)
C_GVGG = (1152, 1664)
C_Z = (1664, 1920)
C_Q = (1920, 2432)
IN_COLS = 2432


def _cparams(sem):
    return pltpu.CompilerParams(dimension_semantics=sem, vmem_limit_bytes=VMEM_LIMIT)


def _silu(x):
    return x * jax.nn.sigmoid(x)


def _dot(a, b, **kw):
    return jnp.dot(a, b, preferred_element_type=F32, **kw)


def _dot_nt(a, b):
    return lax.dot_general(a, b, (((1,), (1,)), ((), ())), preferred_element_type=F32)


def _dot_tn(a, b):
    return lax.dot_general(a, b, (((0,), (0,)), ((), ())), preferred_element_type=F32)


def _mod_kernel(c_ref, w_ref, b_ref, o_ref):
    c = c_ref[...]
    o_ref[0] = _dot(_silu(c), w_ref[0], precision=HIGHEST) + b_ref[0]


def _modulation(cvec, mod_w, mod_b):
    depth, d, n = mod_w.shape
    nr = cvec.shape[0]
    tn = 768
    return pl.pallas_call(
        _mod_kernel,
        grid=(depth, n // tn),
        in_specs=[pl.BlockSpec((nr, d), lambda l, j: (0, 0)),
                  pl.BlockSpec((1, d, tn), lambda l, j: (l, 0, j)),
                  pl.BlockSpec((1, 1, tn), lambda l, j: (l, 0, j))],
        out_specs=pl.BlockSpec((1, nr, tn), lambda l, j: (l, 0, j)),
        out_shape=jax.ShapeDtypeStruct((depth, nr, n), F32),
        compiler_params=_cparams(("parallel", "parallel")),
        name="modulation",
    )(cvec, mod_w, mod_b.reshape(depth, 1, n))


def _rms(x, g):
    return x * lax.rsqrt(jnp.mean(x * x, axis=-1, keepdims=True) + EPS) * g


def _head_sums(sq, bd):
    hi = sq.astype(BF16)
    lo = (sq - hi.astype(F32)).astype(BF16)
    return _dot(hi, bd) + _dot(lo, bd)


def _inproj_kernel(x_ref, mod_ref, g_ref, w_ref, qkv_ref, ab_ref, kv_ref, gvgg_ref, z_ref, q_ref):
    x = x_ref[0]
    sh = mod_ref[0, 0:1, :]
    sc = mod_ref[0, 1:2, :]
    h = (_rms(x, g_ref[...]) * (1.0 + sc) + sh).astype(BF16)
    for ref, (c0, c1) in ((qkv_ref, C_QKV), (ab_ref, C_AB), (kv_ref, C_KV),
                          (gvgg_ref, C_GVGG), (z_ref, C_Z), (q_ref, C_Q)):
        ref[0] = _dot(h, w_ref[:, c0:c1])


def _in_proj(x, mod, g1, w_in_p, nct):
    nb, s, d = x.shape
    nt = s // TM
    widths = [c1 - c0 for c0, c1 in (C_QKV, C_AB, C_KV, C_GVGG, C_Z, C_Q)]
    tok = lambda w: pl.BlockSpec((1, TM, w), lambda b, t: (b, t, 0))
    return pl.pallas_call(
        _inproj_kernel,
        grid=(nb, nt),
        in_specs=[tok(d),
                  pl.BlockSpec((1, 6, d), lambda b, t: (jnp.where(t < nct, nb, b), 0, 0)),
                  pl.BlockSpec((1, d), lambda b, t: (0, 0)),
                  pl.BlockSpec((d, IN_COLS), lambda b, t: (0, 0))],
        out_specs=[tok(w) for w in widths],
        out_shape=[jax.ShapeDtypeStruct((nb, s, w), F32) for w in widths],
        compiler_params=_cparams(("parallel", "parallel")),
        name="in_proj",
    )(x, mod, g1, w_in_p)


def _halo_specs(width, halo, s):
    per = TM // halo
    cur = pl.BlockSpec((1, TM, width), lambda b, t: (b, t, 0))
    prev = pl.BlockSpec((1, halo, width), lambda b, t: (b, jnp.maximum(t * per - 1, 0), 0))
    nxt = pl.BlockSpec((1, halo, width), lambda b, t: (b, jnp.minimum((t + 1) * per, s // halo - 1), 0))
    return [cur, prev, nxt]


def _halo_flags(nct, nt):
    t = pl.program_id(1)
    prev_ok = jnp.logical_and(t != 0, t != nct)
    next_ok = jnp.logical_and(t != nct - 1, t != nt - 1)
    return prev_ok, next_ok


def _conformer_kernel(cur_ref, prev_ref, next_ref, dww_ref, dwb_ref, lng_ref, lnb_ref, pww_ref, pwb_ref,
                      o_ref, ext_ref, *, nct, nt):
    prev_ok, next_ok = _halo_flags(nct, nt)

    def glu(v):
        return v[:, :CONV_CH] * jax.nn.sigmoid(v[:, CONV_CH:])

    ext_ref[0:CONV_HALO, :] = jnp.where(prev_ok, glu(prev_ref[0]), 0.0)
    ext_ref[CONV_HALO:CONV_HALO + TM, :] = glu(cur_ref[0])
    ext_ref[CONV_HALO + TM:2 * CONV_HALO + TM, :] = jnp.where(next_ok, glu(next_ref[0]), 0.0)
    rb = 64
    off = CONV_HALO - CONV_WIDTH // 2
    for r in range(TM // rb):
        acc = jnp.zeros((rb, CONV_CH), F32) + dwb_ref[...]
        for j in range(CONV_WIDTH):
            acc = acc + ext_ref[pl.ds(r * rb + off + j, rb), :] * dww_ref[j:j + 1, :]
        mu = jnp.mean(acc, axis=-1, keepdims=True)
        xc = acc - mu
        y = xc * lax.rsqrt(jnp.mean(xc * xc, axis=-1, keepdims=True) + EPS) * lng_ref[...] + lnb_ref[...]
        h = _silu(y).astype(BF16)
        o_ref[0, r * rb:(r + 1) * rb, :] = _dot(h, pww_ref[...]) + pwb_ref[...]


def _conformer(gvgg, dww, dwb, lng, lnb, pww, pwb, nct):
    nb, s, _ = gvgg.shape
    nt = s // TM
    row = lambda w: pl.BlockSpec((1, w), lambda b, t: (0, 0))
    return pl.pallas_call(
        functools.partial(_conformer_kernel, nct=nct, nt=nt),
        grid=(nb, nt),
        in_specs=_halo_specs(2 * CONV_CH, CONV_HALO, s) + [
            pl.BlockSpec((CONV_WIDTH + 1, CONV_CH), lambda b, t: (0, 0)),
            row(CONV_CH), row(CONV_CH), row(CONV_CH),
            pl.BlockSpec((CONV_CH, CONV_CH), lambda b, t: (0, 0)),
            row(CONV_CH)],
        out_specs=pl.BlockSpec((1, TM, CONV_CH), lambda b, t: (b, t, 0)),
        out_shape=jax.ShapeDtypeStruct((nb, s, CONV_CH), F32),
        scratch_shapes=[pltpu.VMEM((TM + 2 * CONV_HALO, CONV_CH), F32)],
        compiler_params=_cparams(("parallel", "parallel")),
        name="conformer",
    )(gvgg, gvgg, gvgg, dww, dwb, lng, lnb, pww, pwb)


def _gdn_feat_kernel(cur_ref, prev_ref, next_ref, cw_ref, ab_ref, alog_ref, dtb_ref,
                     bd_ref, q_ref, k_ref, v_ref, gb_ref, ext_ref, y_ref, *, nct, nt):
    prev_ok, next_ok = _halo_flags(nct, nt)
    ext_ref[0:SHORT_HALO, :] = jnp.where(prev_ok, prev_ref[0], 0.0)
    ext_ref[SHORT_HALO:SHORT_HALO + TM, :] = cur_ref[0]
    ext_ref[SHORT_HALO + TM:2 * SHORT_HALO + TM, :] = jnp.where(next_ok, next_ref[0], 0.0)
    rb = 32
    off = SHORT_HALO - SHORT_CONV // 2
    for r in range(TM // rb):
        acc = jnp.zeros((rb, GDN_QKV), F32)
        for j in range(SHORT_CONV):
            acc = acc + ext_ref[pl.ds(r * rb + off + j, rb), :] * cw_ref[j:j + 1, :]
        y_ref[r * rb:(r + 1) * rb, :] = _silu(acc)

    def l2n(x):
        return x * lax.rsqrt(_head_sums(x * x, bd_ref[...]) + EPS)

    qn = l2n(y_ref[:, 0:GDN_WIDTH]) * (GDN_DK ** -0.5)
    kn = l2n(y_ref[:, GDN_WIDTH:2 * GDN_WIDTH])
    v = y_ref[:, 2 * GDN_WIDTH:]
    for h in range(GDN_HEADS):
        q_ref[0, h] = qn[:, h * GDN_DK:(h + 1) * GDN_DK]
        k_ref[0, h] = kn[:, h * GDN_DK:(h + 1) * GDN_DK]
        v_ref[0, h] = v[:, h * GDN_DK:(h + 1) * GDN_DK]
    ab = ab_ref[0]
    lane = lax.broadcasted_iota(jnp.int32, ab.shape, 1)
    xa = ab + dtb_ref[...]
    softplus = jnp.maximum(xa, 0.0) + jnp.log(1.0 + jnp.exp(-jnp.abs(xa)))
    g = -jnp.exp(alog_ref[...]) * softplus
    beta = jax.nn.sigmoid(ab)
    gb_ref[0] = jnp.where(lane < 2 * GDN_HEADS, g, jnp.where(lane < 4 * GDN_HEADS, beta, 0.0))


def _gdn_features(qkv, ab, cw, alog, dtb, bd, nct):
    nb, s, _ = qkv.shape
    nt = s // TM
    head = pl.BlockSpec((1, GDN_HEADS, TM, GDN_DK), lambda b, t: (b, 0, t, 0))
    hshape = jax.ShapeDtypeStruct((nb, GDN_HEADS, s, GDN_DK), F32)
    return pl.pallas_call(
        functools.partial(_gdn_feat_kernel, nct=nct, nt=nt),
        grid=(nb, nt),
        in_specs=_halo_specs(GDN_QKV, SHORT_HALO, s) + [
            pl.BlockSpec((8, GDN_QKV), lambda b, t: (0, 0)),
            pl.BlockSpec((1, TM, LANES), lambda b, t: (b, t, 0)),
            pl.BlockSpec((1, LANES), lambda b, t: (0, 0)),
            pl.BlockSpec((1, LANES), lambda b, t: (0, 0)),
            pl.BlockSpec((GDN_WIDTH, GDN_WIDTH), lambda b, t: (0, 0))],
        out_specs=[head, head, head, pl.BlockSpec((1, TM, LANES), lambda b, t: (b, t, 0))],
        out_shape=[hshape, hshape, hshape, jax.ShapeDtypeStruct((nb, s, LANES), F32)],
        scratch_shapes=[pltpu.VMEM((TM + 2 * SHORT_HALO, GDN_QKV), F32), pltpu.VMEM((TM, GDN_QKV), F32)],
        compiler_params=_cparams(("parallel", "parallel")),
        name="gdn_features",
    )(qkv, qkv, qkv, cw, ab, alog, dtb, bd)


def _gdn_scan_kernel(qf_ref, kf_ref, vf_ref, gf_ref, qb_ref, kb_ref, vb_ref, gbk_ref,
                     of_ref, ob_ref, s_ref):
    @pl.when(pl.program_id(0) == 0)
    def _():
        s_ref[...] = jnp.zeros(s_ref.shape, F32)

    c = GDN_C
    row = lax.broadcasted_iota(jnp.int32, (c, c), 0)
    col = lax.broadcasted_iota(jnp.int32, (c, c), 1)
    eye = (row == col).astype(F32)
    blk = {2 ** e: (row >> e) == (col >> e) for e in range(1, int(math.log2(c)) + 1)}

    chains = []
    for b, d in ((b, d) for b in range(qf_ref.shape[0]) for d in range(2)):
        q_ref, k_ref, v_ref, g_ref, o_ref = ((qf_ref, kf_ref, vf_ref, gf_ref, of_ref),
                                             (qb_ref, kb_ref, vb_ref, gbk_ref, ob_ref))[d]
        incl = (row >= col) if d == 0 else (row <= col)
        strict = (row > col) if d == 0 else (row < col)
        gb = g_ref[b]
        gam_all = _dot(incl.astype(F32), gb, precision=HIGHEST)
        gam_t = gam_all.T
        last_row = c - 1 if d == 0 else 0
        for h in range(GDN_HEADS):
            cc = d * GDN_HEADS + h
            gam = gam_all[:, cc:cc + 1]
            beta = gb[:, 2 * GDN_HEADS + cc:2 * GDN_HEADS + cc + 1]
            kh = k_ref[b, h]
            chains.append(dict(
                b=b, d=d, h=h, o_ref=o_ref, strict=strict, gam=gam, beta=beta, kh=kh,
                qh=q_ref[b, h], vh=v_ref[b, h],
                last=gam_all[last_row:last_row + 1, cc:cc + 1],
                decay=jnp.exp(jnp.where(incl, gam - gam_t[cc:cc + 1, :], -1e30)),
                kbeta=kh * beta))
    for ch in chains:
        ch["a"] = jnp.where(ch["strict"], _dot_nt(ch["kbeta"], ch["kh"]) * ch["decay"], 0.0)
    for ch in chains:
        ch["t"] = eye - jnp.where(blk[2], ch["a"], 0.0)
    bsz = 2
    while bsz < c:
        off = jnp.logical_and(blk[2 * bsz], jnp.logical_not(blk[bsz]))
        for ch in chains:
            ch["tl"] = _dot(ch["t"], jnp.where(off, ch["a"], 0.0))
        for ch in chains:
            ch["t"] = ch["t"] - _dot(ch["tl"], ch["t"])
        bsz *= 2
    for ch in chains:
        rhs = jnp.concatenate([ch["vh"] * ch["beta"], ch["kbeta"] * jnp.exp(ch["gam"])], axis=1)
        ch["uw"] = _dot(ch["t"], rhs)
    for ch in chains:
        ch["pm"] = _dot_nt(ch["qh"], ch["kh"]) * ch["decay"]
    for ch in chains:
        ch["s"] = s_ref[ch["b"], ch["d"], ch["h"]]
        ch["v_new"] = ch["uw"][:, :GDN_DK] - _dot(ch["uw"][:, GDN_DK:], ch["s"])
    for ch in chains:
        h = ch["h"]
        qg = ch["qh"] * jnp.exp(ch["gam"])
        ch["o_ref"][ch["b"], :, h * GDN_DK:(h + 1) * GDN_DK] = _dot(qg, ch["s"]) + _dot(ch["pm"], ch["v_new"])
    for ch in chains:
        kd = ch["kh"] * jnp.exp(ch["last"] - ch["gam"])
        s_ref[ch["b"], ch["d"], ch["h"]] = ch["s"] * jnp.exp(ch["last"]) + _dot_tn(kd, ch["v_new"])


def _gdn_scan(q, k, v, gb, ncc):
    nb, _, s, _ = q.shape
    nc = s // GDN_C

    def bwd(n):
        return jnp.where(n < ncc, ncc - 1 - n, nc + ncc - 1 - n)

    hf = pl.BlockSpec((nb, GDN_HEADS, GDN_C, GDN_DK), lambda n: (0, 0, n, 0))
    hb = pl.BlockSpec((nb, GDN_HEADS, GDN_C, GDN_DK), lambda n: (0, 0, bwd(n), 0))
    gf = pl.BlockSpec((nb, GDN_C, LANES), lambda n: (0, n, 0))
    gk = pl.BlockSpec((nb, GDN_C, LANES), lambda n: (0, bwd(n), 0))
    of = pl.BlockSpec((nb, GDN_C, GDN_WIDTH), lambda n: (0, n, 0))
    ob = pl.BlockSpec((nb, GDN_C, GDN_WIDTH), lambda n: (0, bwd(n), 0))
    oshape = jax.ShapeDtypeStruct((nb, s, GDN_WIDTH), F32)
    return pl.pallas_call(
        _gdn_scan_kernel,
        grid=(nc,),
        in_specs=[hf, hf, hf, gf, hb, hb, hb, gk],
        out_specs=[of, ob],
        out_shape=[oshape, oshape],
        scratch_shapes=[pltpu.VMEM((nb, 2, GDN_HEADS, GDN_DK, GDN_DK), F32)],
        compiler_params=_cparams(("arbitrary",)),
        name="gdn_scan",
    )(q, k, v, gb, q, k, v, gb)


def _attn_prep_kernel(q_ref, kv_ref, cos_ref, sin_ref, qg_ref, kg_ref, bd_ref,
                      qt_ref, kh_ref, vt_ref, *, q_scale):
    cos = cos_ref[...]
    sin = sin_ref[...]
    bd = bd_ref[...]
    lane = lax.broadcasted_iota(jnp.int32, cos.shape, 1)
    first = (lane % 32) < 16

    def norm_rope(x, g):
        ss = _head_sums(x * x, bd)
        y = x * lax.rsqrt(ss * (1.0 / HEAD_DIM) + EPS) * g
        swapped = jnp.where(first, pltpu.roll(y, LANES - 16, 1), pltpu.roll(y, 16, 1))
        return y * cos + swapped * sin

    q = q_ref[0]
    for j in range(ATTN_WIDTH // LANES):
        yt = (norm_rope(q[:, j * LANES:(j + 1) * LANES], qg_ref[...]) * q_scale).T
        qt_ref[0, 2 * j] = yt[:HEAD_DIM].astype(BF16)
        qt_ref[0, 2 * j + 1] = yt[HEAD_DIM:].astype(BF16)
    kv = kv_ref[0]
    y = norm_rope(kv[:, :LANES], kg_ref[...])
    kh_ref[0, 0] = y[:, :HEAD_DIM].astype(BF16)
    kh_ref[0, 1] = y[:, HEAD_DIM:].astype(BF16)
    vt = kv[:, LANES:].T.astype(BF16)
    ones = jnp.ones((VT_ROWS - HEAD_DIM, ATT_TK), BF16)
    for g in range(ATTN_KV_HEADS):
        for cidx in range(TM // ATT_TK):
            vt_ref[0, g, cidx, 0:HEAD_DIM, :] = vt[g * HEAD_DIM:(g + 1) * HEAD_DIM,
                                                   cidx * ATT_TK:(cidx + 1) * ATT_TK]
            vt_ref[0, g, cidx, HEAD_DIM:VT_ROWS, :] = ones


def _attn_prep(q, kv, cos_t, sin_t, qg, kg, bd, q_scale):
    nb, s, _ = q.shape
    nt = s // TM
    tab = pl.BlockSpec((TM, LANES), lambda b, t: (t, 0))
    row = pl.BlockSpec((1, LANES), lambda b, t: (0, 0))
    per = TM // ATT_TK
    return pl.pallas_call(
        functools.partial(_attn_prep_kernel, q_scale=q_scale),
        grid=(nb, nt),
        in_specs=[pl.BlockSpec((1, TM, ATTN_WIDTH), lambda b, t: (b, t, 0)),
                  pl.BlockSpec((1, TM, 2 * LANES), lambda b, t: (b, t, 0)),
                  tab, tab, row, row,
                  pl.BlockSpec((LANES, LANES), lambda b, t: (0, 0))],
        out_specs=[pl.BlockSpec((1, ATTN_Q_HEADS, HEAD_DIM, TM), lambda b, t: (b, 0, 0, t)),
                   pl.BlockSpec((1, ATTN_KV_HEADS, TM, HEAD_DIM), lambda b, t: (b, 0, t, 0)),
                   pl.BlockSpec((1, ATTN_KV_HEADS, per, VT_ROWS, ATT_TK), lambda b, t: (b, 0, t, 0, 0))],
        out_shape=[jax.ShapeDtypeStruct((nb, ATTN_Q_HEADS, HEAD_DIM, s), BF16),
                   jax.ShapeDtypeStruct((nb, ATTN_KV_HEADS, s, HEAD_DIM), BF16),
                   jax.ShapeDtypeStruct((nb, ATTN_KV_HEADS, s // ATT_TK, VT_ROWS, ATT_TK), BF16)],
        compiler_params=_cparams(("parallel", "parallel")),
        name="attn_prep",
    )(q, kv, cos_t, sin_t, qg, kg, bd)


def _attn_kernel(qt_ref, k_ref, vt_ref, o_ref, m_ref, acc_ref, s_ref, mx_ref, *, nctq, lc, s_len):
    qi = pl.program_id(2)
    m_ref[...] = jnp.full(m_ref.shape, -jnp.inf, F32)
    acc_ref[...] = jnp.zeros(acc_ref.shape, F32)
    nk = jnp.where(qi < nctq, lc // ATT_TK, s_len // ATT_TK)

    def score(j, slot):
        ks = pl.multiple_of(j * ATT_TK, ATT_TK)
        k = k_ref[0, 0, pl.ds(ks, ATT_TK), :]
        for h in range(ATTN_GROUP):
            s = _dot(k, qt_ref[0, h])
            s_ref[slot, h] = s
            mx_ref[slot, h] = jnp.max(s, axis=0, keepdims=True)

    def step(j, slot):
        score(jnp.minimum(j + 1, nk - 1), 1 - slot)
        vt = vt_ref[0, 0, j]
        for h in range(ATTN_GROUP):
            s = s_ref[slot, h]
            m_old = m_ref[h]
            m_new = jnp.maximum(m_old, mx_ref[slot, h])
            alpha = jnp.exp2(m_old - m_new)
            p = jnp.exp2(s - m_new)
            acc_ref[h] = alpha * acc_ref[h] + _dot(vt, p.astype(BF16))
            m_ref[h] = m_new

    def quad(i, carry):
        for u in range(4):
            step(4 * i + u, u % 2)
        return carry

    score(0, 0)
    lax.fori_loop(0, nk // 4, quad, 0)
    done = (nk // 4) * 4

    @pl.when(nk - done >= 2)
    def _():
        step(done, 0)
        step(done + 1, 1)

    @pl.when(nk % 2 == 1)
    def _():
        step(nk - 1, 0)

    for hp in range(ATTN_GROUP // 2):
        pair = jnp.concatenate(
            [acc_ref[h, 0:HEAD_DIM, :] / acc_ref[h, HEAD_DIM:HEAD_DIM + 1, :] for h in (2 * hp, 2 * hp + 1)], axis=0)
        o_ref[0, :, hp * LANES:(hp + 1) * LANES] = pair.T


def _attention(qt, kh, vt, lc):
    nb, _, _, s = qt.shape
    nq = s // ATT_TQ
    return pl.pallas_call(
        functools.partial(_attn_kernel, nctq=lc // ATT_TQ, lc=lc, s_len=s),
        grid=(nb, ATTN_KV_HEADS, nq),
        in_specs=[pl.BlockSpec((1, ATTN_GROUP, HEAD_DIM, ATT_TQ), lambda b, g, i: (b, g, 0, i)),
                  pl.BlockSpec((1, 1, s, HEAD_DIM), lambda b, g, i: (b, g, 0, 0)),
                  pl.BlockSpec((1, 1, s // ATT_TK, VT_ROWS, ATT_TK), lambda b, g, i: (b, g, 0, 0, 0))],
        out_specs=pl.BlockSpec((1, ATT_TQ, ATTN_GROUP * HEAD_DIM), lambda b, g, i: (b, i, g)),
        out_shape=jax.ShapeDtypeStruct((nb, s, ATTN_WIDTH), F32),
        scratch_shapes=[pltpu.VMEM((ATTN_GROUP, 1, ATT_TQ), F32),
                        pltpu.VMEM((ATTN_GROUP, VT_ROWS, ATT_TQ), F32),
                        pltpu.VMEM((2, ATTN_GROUP, ATT_TK, ATT_TQ), F32),
                        pltpu.VMEM((2, ATTN_GROUP, 1, ATT_TQ), F32)],
        compiler_params=_cparams(("parallel", "parallel", "parallel")),
        name="attention",
    )(qt, kh, vt)


def _outproj_kernel(conv_ref, of_ref, ob_ref, z_ref, attn_ref, x_ref, mod_ref, gg_ref, bd_ref, w_ref,
                    n2_ref, wr_ref, br_ref, xo_ref, h2_ref, ri_ref, rw_ref):
    o = of_ref[0] + ob_ref[0]
    ss = _head_sums(o * o, bd_ref[...])
    gdn = o * lax.rsqrt(ss * (1.0 / GDN_DK) + EPS) * gg_ref[...] * _silu(z_ref[0])
    y = _dot(conv_ref[0].astype(BF16), w_ref[0:CONV_CH, :])
    y = y + _dot(gdn.astype(BF16), w_ref[CONV_CH:CONV_CH + GDN_WIDTH, :])
    y = y + _dot(attn_ref[0].astype(BF16), w_ref[CONV_CH + GDN_WIDTH:, :])
    x = x_ref[0] + mod_ref[0, 2:3, :] * y
    xo_ref[0] = x
    h2 = _rms(x, n2_ref[...]) * (1.0 + mod_ref[0, 4:5, :]) + mod_ref[0, 3:4, :]
    h2_ref[0] = h2
    h_hi = h2.astype(BF16)
    h_lo = (h2 - h_hi.astype(F32)).astype(BF16)
    t = _dot(h_hi, wr_ref[...])
    lg = t[:, :LANES] + t[:, LANES:] + _dot(h_lo, wr_ref[:, 0:LANES]) + br_ref[...]
    lane = lax.broadcasted_iota(jnp.int32, lg.shape, 1)
    lane_f = lane.astype(F32)

    def first_max(v):
        mx = jnp.max(v, axis=-1, keepdims=True)
        return mx, jnp.min(jnp.where(v == mx, lane_f, float(LANES)), axis=-1, keepdims=True)

    lgg = jnp.where(lane < N_GROUPS, lg, -jnp.inf)
    gmax, grp = first_max(lgg)
    pg_sel = 1.0 / jnp.sum(jnp.exp(lgg - gmax), axis=-1, keepdims=True)
    in_grp = jnp.logical_and(lane >= N_GROUPS, ((lane - N_GROUPS) >> 3).astype(F32) == grp)
    le = jnp.where(in_grp, lg, -jnp.inf)
    e1max, i1 = first_max(le)
    e2max, i2 = first_max(jnp.where(lane_f == i1, -jnp.inf, le))
    r = jnp.exp(e2max - e1max)
    w1 = pg_sel / (1.0 + r)
    ri_ref[0] = jnp.where(lane == 0, i1, jnp.where(lane == 1, i2, float(N_GROUPS))).astype(jnp.int32) - N_GROUPS
    rw_ref[0] = jnp.where(lane == 0, w1, jnp.where(lane == 1, w1 * r, 0.0))


def _out_proj(conv, o_f, o_b, z, attn, x, mod, gdn_g, bd, w_out, n2, wr, br, nct):
    nb, s, d = x.shape
    nt = s // TM
    tok = lambda w: pl.BlockSpec((1, TM, w), lambda b, t: (b, t, 0))
    full = lambda a: pl.BlockSpec(a.shape, lambda b, t: (0,) * a.ndim)
    return pl.pallas_call(
        _outproj_kernel,
        grid=(nb, nt),
        in_specs=[tok(CONV_CH), tok(GDN_WIDTH), tok(GDN_WIDTH), tok(GDN_WIDTH), tok(ATTN_WIDTH), tok(d),
                  pl.BlockSpec((1, 6, d), lambda b, t: (jnp.where(t < nct, nb, b), 0, 0)),
                  full(gdn_g), full(bd), full(w_out), full(n2), full(wr), full(br)],
        out_specs=[tok(d), tok(d), tok(LANES), tok(LANES)],
        out_shape=[jax.ShapeDtypeStruct((nb, s, d), F32), jax.ShapeDtypeStruct((nb, s, d), F32),
                   jax.ShapeDtypeStruct((nb, s, LANES), jnp.int32), jax.ShapeDtypeStruct((nb, s, LANES), F32)],
        compiler_params=_cparams(("parallel", "parallel")),
        name="out_proj",
    )(conv, o_f, o_b, z, attn, x, mod, gdn_g, bd, w_out, n2, wr, br)


def _dispatch_kernel(slot_ref, h_ref, xe_in_ref, xe_ref, sem, *, ntile):
    del xe_in_ref
    g = pl.program_id(0)
    base = g * (TM * TOP_K)

    def row_copy(r, k, par):
        row = g * TM + r
        dst = slot_ref[base + TOP_K * r + k]
        return pltpu.make_async_copy(h_ref.at[pl.ds(row, 1)], xe_ref.at[pl.ds(dst, 1)], sem.at[par])

    def wait_step(par):
        for _ in range(TOP_K):
            pltpu.make_async_copy(h_ref.at[pl.ds(0, TM)], xe_ref.at[pl.ds(0, TM)], sem.at[par]).wait()

    def issue_all(par):
        def issue(r, carry):
            for k in range(TOP_K):
                row_copy(r, k, par).start()
            return carry
        lax.fori_loop(0, TM, issue, 0, unroll=8)

    for par in range(2):
        @pl.when(g % 2 == par)
        def _(par=par):
            issue_all(par)

            @pl.when(g > 0)
            def _():
                wait_step(1 - par)

            @pl.when(g == ntile - 1)
            def _():
                wait_step(par)


def _dispatch(h2, slots, n_slots):
    n_tok, d = h2.shape
    ntile = n_tok // TM
    grid_spec = pltpu.PrefetchScalarGridSpec(
        num_scalar_prefetch=1,
        grid=(ntile,),
        in_specs=[pl.BlockSpec(memory_space=pl.ANY),
                  pl.BlockSpec(memory_space=pl.ANY)],
        out_specs=pl.BlockSpec(memory_space=pl.ANY),
        scratch_shapes=[pltpu.SemaphoreType.DMA((2,))],
    )
    return pl.pallas_call(
        functools.partial(_dispatch_kernel, ntile=ntile),
        grid_spec=grid_spec,
        out_shape=jax.ShapeDtypeStruct((n_slots, d), F32),
        input_output_aliases={2: 0},
        compiler_params=_cparams(("arbitrary",)),
        name="moe_dispatch",
    )(slots, h2, jnp.zeros((n_slots, d), F32))


def _expert_kernel(be_ref, nu_ref, x_ref, wg_ref, wu_ref, wd_ref, o_ref, wg_s, wu_s, wd_s):
    i = pl.program_id(0)
    used = i < nu_ref[0]
    new_expert = jnp.logical_or(i == 0, be_ref[i] != be_ref[jnp.maximum(i - 1, 0)])

    @pl.when(jnp.logical_and(used, new_expert))
    def _():
        wg_s[...] = wg_ref[0, 0].astype(BF16)
        wu_s[...] = wu_ref[0, 0].astype(BF16)
        wd_s[...] = wd_ref[0, 0].astype(BF16)

    @pl.when(used)
    def _():
        x = x_ref[...].astype(BF16)
        g = _dot(x, wg_s[...])
        u = _dot(x, wu_s[...])
        h = (_silu(g) * u).astype(BF16)
        o_ref[...] = _dot(h, wd_s[...])

    @pl.when(i >= nu_ref[0])
    def _():
        o_ref[...] = jnp.zeros(o_ref.shape, F32)


def _experts(xs, blk_e, n_used, w_gate, w_up, w_down, layer):
    n_slots, d = xs.shape
    nblk = n_slots // MOE_BM
    de = w_gate.shape[-1]
    grid_spec = pltpu.PrefetchScalarGridSpec(
        num_scalar_prefetch=2,
        grid=(nblk,),
        in_specs=[pl.BlockSpec((MOE_BM, d), lambda i, be, nu: (jnp.minimum(i, nu[0] - 1), 0)),
                  pl.BlockSpec((1, 1, d, de), lambda i, be, nu: (layer, be[i], 0, 0)),
                  pl.BlockSpec((1, 1, d, de), lambda i, be, nu: (layer, be[i], 0, 0)),
                  pl.BlockSpec((1, 1, de, d), lambda i, be, nu: (layer, be[i], 0, 0))],
        out_specs=pl.BlockSpec((MOE_BM, d), lambda i, be, nu: (i, 0)),
        scratch_shapes=[pltpu.VMEM((d, de), BF16), pltpu.VMEM((d, de), BF16), pltpu.VMEM((de, d), BF16)],
    )
    return pl.pallas_call(
        _expert_kernel,
        grid_spec=grid_spec,
        out_shape=jax.ShapeDtypeStruct((n_slots, d), F32),
        compiler_params=_cparams(("arbitrary",)),
        name="experts",
    )(blk_e, n_used, xs, w_gate, w_up, w_down)


def _combine_kernel(slot_ref, x_ref, rw_ref, mod_ref, fg_ref, ye_ref, o_ref, buf_ref, sem, *, final, nt, ntile):
    g = pl.program_id(0) * nt + pl.program_id(1)

    def row_copy(tile, r, k, par):
        src = slot_ref[tile * (TM * TOP_K) + TOP_K * r + k]
        return pltpu.make_async_copy(ye_ref.at[pl.ds(src, 1)], buf_ref.at[par, k, pl.ds(r, 1)], sem.at[par, k])

    def issue_tile(tile, par):
        def issue(r, carry):
            for k in range(TOP_K):
                row_copy(tile, r, k, par).start()
            return carry
        lax.fori_loop(0, TM, issue, 0, unroll=8)

    def finish(par):
        for k in range(TOP_K):
            pltpu.make_async_copy(ye_ref.at[pl.ds(0, TM)], buf_ref.at[par, k], sem.at[par, k]).wait()
        rw = rw_ref[0]
        y = rw[:, 0:1] * buf_ref[par, 0] + rw[:, 1:2] * buf_ref[par, 1]
        x = x_ref[0] + mod_ref[0, 5:6, :] * y
        if final:
            x = _rms(x, fg_ref[...])
        o_ref[0] = x

    @pl.when(g == 0)
    def _():
        issue_tile(0, 0)

    for par in range(2):
        @pl.when(g % 2 == par)
        def _(par=par):
            @pl.when(g + 1 < ntile)
            def _():
                issue_tile(g + 1, 1 - par)

            finish(par)


def _combine(x, ye, slots, rw, mod, fg, nct, final):
    nb, s, d = x.shape
    nt = s // TM
    tok = lambda w: pl.BlockSpec((1, TM, w), lambda b, t, sl: (b, t, 0))
    grid_spec = pltpu.PrefetchScalarGridSpec(
        num_scalar_prefetch=1,
        grid=(nb, nt),
        in_specs=[tok(d), tok(LANES),
                  pl.BlockSpec((1, 6, d), lambda b, t, sl: (jnp.where(t < nct, nb, b), 0, 0)),
                  pl.BlockSpec((1, d), lambda b, t, sl: (0, 0)),
                  pl.BlockSpec(memory_space=pl.ANY)],
        out_specs=tok(d),
        scratch_shapes=[pltpu.VMEM((2, TOP_K, TM, d), F32), pltpu.SemaphoreType.DMA((2, TOP_K))],
    )
    return pl.pallas_call(
        functools.partial(_combine_kernel, final=final, nt=nt, ntile=nb * nt),
        grid_spec=grid_spec,
        out_shape=jax.ShapeDtypeStruct((nb, s, d), F32),
        compiler_params=_cparams(("arbitrary", "arbitrary")),
        name="moe_combine",
    )(slots, x, rw, mod, fg, ye)


def _rank_kernel(ri_ref, rank_ref, cnt_ref, carry_ref):
    @pl.when(pl.program_id(0) == 0)
    def _():
        carry_ref[...] = jnp.zeros(carry_ref.shape, F32)

    ri = ri_ref[0]
    lane = lax.broadcasted_iota(jnp.int32, ri.shape, 1)
    oh1 = lane == ri[:, 0:1]
    oh2 = lane == ri[:, 1:2]
    oh = jnp.logical_or(oh1, oh2).astype(F32)
    row = lax.broadcasted_iota(jnp.int32, (TM, TM), 0)
    col = lax.broadcasted_iota(jnp.int32, (TM, TM), 1)
    before = carry_ref[0:1, :] + _dot((row > col).astype(BF16), oh.astype(BF16))
    r1 = jnp.sum(jnp.where(oh1, before, 0.0), axis=-1, keepdims=True)
    r2 = jnp.sum(jnp.where(oh2, before, 0.0), axis=-1, keepdims=True)
    rank_ref[0] = jnp.where(lane == 0, r1, jnp.where(lane == 1, r2, 0.0)).astype(jnp.int32)
    carry_ref[...] = carry_ref[...] + jnp.sum(oh, axis=0, keepdims=True)
    cnt_ref[...] = carry_ref[...]


def _slot_kernel(ri_ref, rank_ref, ps_ref, slot_ref):
    ri = ri_ref[0]
    rk = rank_ref[0]
    lane = lax.broadcasted_iota(jnp.int32, ri.shape, 1)
    ps = ps_ref[...]
    s1 = jnp.sum(jnp.where(lane == ri[:, 0:1], ps, 0.0), axis=-1, keepdims=True).astype(jnp.int32) + rk[:, 0:1]
    s2 = jnp.sum(jnp.where(lane == ri[:, 1:2], ps, 0.0), axis=-1, keepdims=True).astype(jnp.int32) + rk[:, 1:2]
    slot_ref[0] = jnp.where(lane == 0, s1, jnp.where(lane == 1, s2, 0))


def _route(ri, n_tok):
    ntile = n_tok // TM
    ri = ri.reshape(ntile, TM, LANES)
    tile = pl.BlockSpec((1, TM, LANES), lambda i: (i, 0, 0))
    rank, cnt = pl.pallas_call(
        _rank_kernel,
        grid=(ntile,),
        in_specs=[tile],
        out_specs=[tile, pl.BlockSpec((8, LANES), lambda i: (0, 0))],
        out_shape=[jax.ShapeDtypeStruct((ntile, TM, LANES), jnp.int32), jax.ShapeDtypeStruct((8, LANES), F32)],
        scratch_shapes=[pltpu.VMEM((8, LANES), F32)],
        compiler_params=_cparams(("arbitrary",)),
        name="moe_rank",
    )(ri)
    counts = cnt[0, :N_EXPERTS].astype(jnp.int32)
    pcounts = (counts + MOE_BM - 1) // MOE_BM * MOE_BM
    pends = jnp.cumsum(pcounts)
    pstarts = jnp.zeros((1, LANES), F32).at[0, :N_EXPERTS].set((pends - pcounts).astype(F32))
    slot = pl.pallas_call(
        _slot_kernel,
        grid=(ntile,),
        in_specs=[tile, tile, pl.BlockSpec((1, LANES), lambda i: (0, 0))],
        out_specs=tile,
        out_shape=jax.ShapeDtypeStruct((ntile, TM, LANES), jnp.int32),
        compiler_params=_cparams(("parallel",)),
        name="moe_slot",
    )(ri, rank, pstarts)
    slots = slot.reshape(n_tok, LANES)[:, :TOP_K].reshape(-1)
    n_slots = (-(-n_tok * TOP_K // MOE_BM) + N_EXPERTS) * MOE_BM
    blk_start = jnp.arange(n_slots // MOE_BM, dtype=jnp.int32) * MOE_BM
    blk_e = jnp.minimum(jnp.sum((pends[None, :] <= blk_start[:, None]).astype(jnp.int32), axis=1), N_EXPERTS - 1)
    n_used = (pends[-1] // MOE_BM).astype(jnp.int32).reshape(1)
    return slots, n_slots, blk_e, n_used


def _rope_tables(lc, l):
    rows = l // GRID_W
    row = jnp.repeat(jnp.arange(rows), GRID_W).astype(F32)
    col = (jnp.arange(rows * GRID_W) % GRID_W).astype(F32)
    n_freq = HEAD_DIM // 4
    inv = ROPE_THETA ** (-jnp.arange(n_freq, dtype=F32) / n_freq)
    ar, ac = row[:, None] * inv, col[:, None] * inv
    cos = jnp.concatenate([jnp.cos(ar), jnp.cos(ar), jnp.cos(ac), jnp.cos(ac)], axis=-1)
    sin = jnp.concatenate([-jnp.sin(ar), jnp.sin(ar), -jnp.sin(ac), jnp.sin(ac)], axis=-1)
    cos = jnp.concatenate([jnp.ones((lc, HEAD_DIM), F32), cos], axis=0)
    sin = jnp.concatenate([jnp.zeros((lc, HEAD_DIM), F32), sin], axis=0)
    return jnp.tile(cos, (1, 2)), jnp.tile(sin, (1, 2))


def _block_diag_ones(n, blk):
    i = jnp.arange(n)
    return (i[:, None] // blk == i[None, :] // blk).astype(BF16)


def kernel(x, c, ctx, c_ctx, mod_w, mod_b, norm1_g, norm2_g, w_in, conv_dw_w, conv_dw_b, conv_ln_g, conv_ln_b, conv_pw_w, conv_pw_b, gdn_conv_w, gdn_a_log, gdn_dt_bias, gdn_norm_g, attn_q_norm_g, attn_k_norm_g, w_out, router_group_w, router_group_b, router_expert_w, router_expert_b, expert_w_gate, expert_w_up, expert_w_down, final_norm_g):
    nb, l, d = x.shape
    lc = ctx.shape[1]
    depth = mod_w.shape[0]
    s = lc + l
    assert lc % TM == 0 and l % TM == 0 and lc % GDN_C == 0 and l % GDN_C == 0
    assert lc % ATT_TK == 0 and s % ATT_TK == 0 and lc % ATT_TQ == 0
    nct = lc // TM
    n_tok = nb * s

    nr = -(-(nb + 1) // 8) * 8
    cvec = jnp.zeros((nr, d), F32).at[:nb].set(c).at[nb].set(c_ctx)
    mod_all = _modulation(cvec, mod_w, mod_b).reshape(depth, nr, 6, d)

    cos_t, sin_t = _rope_tables(lc, l)
    bd64 = _block_diag_ones(LANES, HEAD_DIM)
    bd_gdn = _block_diag_ones(GDN_WIDTH, GDN_DK)
    q_scale = (HEAD_DIM ** -0.5) * math.log2(math.e)
    row = lambda v: v.reshape(1, -1).astype(F32)
    pad_lanes = lambda v: jnp.zeros((1, LANES), F32).at[0, :v.size].set(v.reshape(-1))

    xs = jnp.concatenate([ctx, x], axis=1)
    for layer in range(depth):
        mod = mod_all[layer]
        wi = w_in[layer]
        w_in_p = jnp.concatenate(
            [wi[:, :IN_A0 + 16], jnp.zeros((d, IN_PAD), F32), wi[:, IN_A0 + 16:]], axis=1).astype(BF16)
        qkv, ab, kv, gvgg, z, q = _in_proj(xs, mod, row(norm1_g[layer]), w_in_p, nct)

        dww = jnp.concatenate([conv_dw_w[layer], jnp.zeros((1, CONV_CH), F32)], axis=0)
        conv = _conformer(gvgg, dww, row(conv_dw_b[layer]), row(conv_ln_g[layer]), row(conv_ln_b[layer]),
                          conv_pw_w[layer].astype(BF16), row(conv_pw_b[layer]), nct)

        cw = jnp.concatenate([gdn_conv_w[layer], jnp.zeros((8 - SHORT_CONV, GDN_QKV), F32)], axis=0)
        gq, gk, gv, gb = _gdn_features(qkv, ab, cw, pad_lanes(gdn_a_log[layer]), pad_lanes(gdn_dt_bias[layer]),
                                       bd_gdn, nct)
        o_f, o_b = _gdn_scan(gq, gk, gv, gb, lc // GDN_C)

        qt, kh, vt = _attn_prep(q, kv, cos_t, sin_t,
                                row(jnp.tile(attn_q_norm_g[layer], 2)), row(jnp.tile(attn_k_norm_g[layer], 2)),
                                bd64, q_scale)
        attn = _attention(qt, kh, vt, lc)

        wr = jnp.zeros((d, LANES), F32).at[:, :N_GROUPS].set(router_group_w[layer])
        wr = wr.at[:, N_GROUPS:N_GROUPS + N_EXPERTS].set(router_expert_w[layer])
        wr_hi = wr.astype(BF16)
        wr = jnp.concatenate([wr_hi, (wr - wr_hi.astype(F32)).astype(BF16)], axis=1)
        br =jnp.zeros((1, LANES), F32).at[0, :N_GROUPS].set(router_group_b[layer])
        br = br.at[0, N_GROUPS:N_GROUPS + N_EXPERTS].set(router_expert_b[layer])
        xs, h2, ri, rw = _out_proj(conv, o_f, o_b, z, attn, xs, mod,
                                   row(jnp.tile(gdn_norm_g[layer], GDN_HEADS)), bd_gdn,
                                   w_out[layer].astype(BF16), row(norm2_g[layer]), wr, br, nct)

        slots, n_slots, blk_e, n_used = _route(ri, n_tok)
        xe = _dispatch(h2.reshape(n_tok, d), slots, n_slots)
        ye = _experts(xe, blk_e, n_used, expert_w_gate, expert_w_up, expert_w_down, layer)
        xs = _combine(xs, ye, slots, rw, mod, row(final_norm_g), nct, layer == depth - 1)
    return xs[:, lc:, :]
```

```python
import functools
import math

import jax
import jax.numpy as jnp
from jax import lax
from jax.experimental import pallas as pl
from jax.experimental.pallas import tpu as pltpu

F32 = jnp.float32
BF16 = jnp.bfloat16
HIGHEST = lax.Precision.HIGHEST

EPS = 1e-6
GRID_W = 64
CONV_CH = 256
CONV_WIDTH = 31
GDN_HEADS = 4
GDN_DK = 64
GDN_WIDTH = 256
GDN_QKV = 768
SHORT_CONV = 5
HEAD_DIM = 64
ATTN_Q_HEADS = 8
ATTN_KV_HEADS = 2
ATTN_GROUP = 4
ATTN_WIDTH = 512
ROPE_THETA = 10000.0
N_GROUPS = 4
EXPERTS_PER_GROUP = 8
N_EXPERTS = 32
TOP_K = 2
D_EXPERT = 512

LANES = 128
TM = 256
GDN_C = 128
CONV_HALO = 16
SHORT_HALO = 8
ATT_TQ = 256
ATT_TK = 256
VT_ROWS = 80
MOE_BM = 256
VMEM_LIMIT = 56 * 1024 * 1024

IN_A0 = GDN_QKV
IN_PAD = LANES - 16
C_QKV = (0, 768)
C_AB = (768, 896)
C_KV = (896, 1152)
C_GVGG = (1152, 1664)
C_Z = (1664, 1920)
C_Q = (1920, 2432)
IN_COLS = 2432


def _cparams(sem):
    return pltpu.CompilerParams(dimension_semantics=sem, vmem_limit_bytes=VMEM_LIMIT)


def _silu(x):
    return x * jax.nn.sigmoid(x)


def _dot(a, b, **kw):
    return jnp.dot(a, b, preferred_element_type=F32, **kw)


def _dot_nt(a, b):
    return lax.dot_general(a, b, (((1,), (1,)), ((), ())), preferred_element_type=F32)


def _dot_tn(a, b):
    return lax.dot_general(a, b, (((0,), (0,)), ((), ())), preferred_element_type=F32)


def _mod_kernel(c_ref, w_ref, b_ref, o_ref):
    c = c_ref[...]
    o_ref[0] = _dot(_silu(c), w_ref[0], precision=HIGHEST) + b_ref[0]


def _modulation(cvec, mod_w, mod_b):
    depth, d, n = mod_w.shape
    nr = cvec.shape[0]
    tn = 768
    return pl.pallas_call(
        _mod_kernel,
        grid=(depth, n // tn),
        in_specs=[pl.BlockSpec((nr, d), lambda l, j: (0, 0)),
                  pl.BlockSpec((1, d, tn), lambda l, j: (l, 0, j)),
                  pl.BlockSpec((1, 1, tn), lambda l, j: (l, 0, j))],
        out_specs=pl.BlockSpec((1, nr, tn), lambda l, j: (l, 0, j)),
        out_shape=jax.ShapeDtypeStruct((depth, nr, n), F32),
        compiler_params=_cparams(("parallel", "parallel")),
        name="modulation",
    )(cvec, mod_w, mod_b.reshape(depth, 1, n))


def _rms(x, g):
    return x * lax.rsqrt(jnp.mean(x * x, axis=-1, keepdims=True) + EPS) * g


def _head_sums(sq, bd):
    hi = sq.astype(BF16)
    lo = (sq - hi.astype(F32)).astype(BF16)
    return _dot(hi, bd) + _dot(lo, bd)


def _inproj_kernel(x_ref, mod_ref, g_ref, w_ref, qkv_ref, ab_ref, kv_ref, gvgg_ref, z_ref, q_ref):
    x = x_ref[0]
    sh = mod_ref[0, 0:1, :]
    sc = mod_ref[0, 1:2, :]
    h = (_rms(x, g_ref[...]) * (1.0 + sc) + sh).astype(BF16)
    for ref, (c0, c1) in ((qkv_ref, C_QKV), (ab_ref, C_AB), (kv_ref, C_KV),
                          (gvgg_ref, C_GVGG), (z_ref, C_Z), (q_ref, C_Q)):
        ref[0] = _dot(h, w_ref[:, c0:c1])


def _in_proj(x, mod, g1, w_in_p, nct):
    nb, s, d = x.shape
    nt = s // TM
    widths = [c1 - c0 for c0, c1 in (C_QKV, C_AB, C_KV, C_GVGG, C_Z, C_Q)]
    tok = lambda w: pl.BlockSpec((1, TM, w), lambda b, t: (b, t, 0))
    return pl.pallas_call(
        _inproj_kernel,
        grid=(nb, nt),
        in_specs=[tok(d),
                  pl.BlockSpec((1, 6, d), lambda b, t: (jnp.where(t < nct, nb, b), 0, 0)),
                  pl.BlockSpec((1, d), lambda b, t: (0, 0)),
                  pl.BlockSpec((d, IN_COLS), lambda b, t: (0, 0))],
        out_specs=[tok(w) for w in widths],
        out_shape=[jax.ShapeDtypeStruct((nb, s, w), F32) for w in widths],
        compiler_params=_cparams(("parallel", "parallel")),
        name="in_proj",
    )(x, mod, g1, w_in_p)


def _halo_specs(width, halo, s):
    per = TM // halo
    cur = pl.BlockSpec((1, TM, width), lambda b, t: (b, t, 0))
    prev = pl.BlockSpec((1, halo, width), lambda b, t: (b, jnp.maximum(t * per - 1, 0), 0))
    nxt = pl.BlockSpec((1, halo, width), lambda b, t: (b, jnp.minimum((t + 1) * per, s // halo - 1), 0))
    return [cur, prev, nxt]


def _halo_flags(nct, nt):
    t = pl.program_id(1)
    prev_ok = jnp.logical_and(t != 0, t != nct)
    next_ok = jnp.logical_and(t != nct - 1, t != nt - 1)
    return prev_ok, next_ok


def _conformer_kernel(cur_ref, prev_ref, next_ref, dww_ref, dwb_ref, lng_ref, lnb_ref, pww_ref, pwb_ref,
                      o_ref, ext_ref, *, nct, nt):
    prev_ok, next_ok = _halo_flags(nct, nt)

    def glu(v):
        return v[:, :CONV_CH] * jax.nn.sigmoid(v[:, CONV_CH:])

    ext_ref[0:CONV_HALO, :] = jnp.where(prev_ok, glu(prev_ref[0]), 0.0)
    ext_ref[CONV_HALO:CONV_HALO + TM, :] = glu(cur_ref[0])
    ext_ref[CONV_HALO + TM:2 * CONV_HALO + TM, :] = jnp.where(next_ok, glu(next_ref[0]), 0.0)
    rb = 64
    off = CONV_HALO - CONV_WIDTH // 2
    for r in range(TM // rb):
        acc = jnp.zeros((rb, CONV_CH), F32) + dwb_ref[...]
        for j in range(CONV_WIDTH):
            acc = acc + ext_ref[pl.ds(r * rb + off + j, rb), :] * dww_ref[j:j + 1, :]
        mu = jnp.mean(acc, axis=-1, keepdims=True)
        xc = acc - mu
        y = xc * lax.rsqrt(jnp.mean(xc * xc, axis=-1, keepdims=True) + EPS) * lng_ref[...] + lnb_ref[...]
        h = _silu(y).astype(BF16)
        o_ref[0, r * rb:(r + 1) * rb, :] = _dot(h, pww_ref[...]) + pwb_ref[...]


def _conformer(gvgg, dww, dwb, lng, lnb, pww, pwb, nct):
    nb, s, _ = gvgg.shape
    nt = s // TM
    row = lambda w: pl.BlockSpec((1, w), lambda b, t: (0, 0))
    return pl.pallas_call(
        functools.partial(_conformer_kernel, nct=nct, nt=nt),
        grid=(nb, nt),
        in_specs=_halo_specs(2 * CONV_CH, CONV_HALO, s) + [
            pl.BlockSpec((CONV_WIDTH + 1, CONV_CH), lambda b, t: (0, 0)),
            row(CONV_CH), row(CONV_CH), row(CONV_CH),
            pl.BlockSpec((CONV_CH, CONV_CH), lambda b, t: (0, 0)),
            row(CONV_CH)],
        out_specs=pl.BlockSpec((1, TM, CONV_CH), lambda b, t: (b, t, 0)),
        out_shape=jax.ShapeDtypeStruct((nb, s, CONV_CH), F32),
        scratch_shapes=[pltpu.VMEM((TM + 2 * CONV_HALO, CONV_CH), F32)],
        compiler_params=_cparams(("parallel", "parallel")),
        name="conformer",
    )(gvgg, gvgg, gvgg, dww, dwb, lng, lnb, pww, pwb)


def _gdn_feat_kernel(cur_ref, prev_ref, next_ref, cw_ref, ab_ref, alog_ref, dtb_ref,
                     bd_ref, q_ref, k_ref, v_ref, gb_ref, ext_ref, y_ref, *, nct, nt):
    prev_ok, next_ok = _halo_flags(nct, nt)
    ext_ref[0:SHORT_HALO, :] = jnp.where(prev_ok, prev_ref[0], 0.0)
    ext_ref[SHORT_HALO:SHORT_HALO + TM, :] = cur_ref[0]
    ext_ref[SHORT_HALO + TM:2 * SHORT_HALO + TM, :] = jnp.where(next_ok, next_ref[0], 0.0)
    rb = 32
    off = SHORT_HALO - SHORT_CONV // 2
    for r in range(TM // rb):
        acc = jnp.zeros((rb, GDN_QKV), F32)
        for j in range(SHORT_CONV):
            acc = acc + ext_ref[pl.ds(r * rb + off + j, rb), :] * cw_ref[j:j + 1, :]
        y_ref[r * rb:(r + 1) * rb, :] = _silu(acc)

    def l2n(x):
        return x * lax.rsqrt(_head_sums(x * x, bd_ref[...]) + EPS)

    qn = l2n(y_ref[:, 0:GDN_WIDTH]) * (GDN_DK ** -0.5)
    kn = l2n(y_ref[:, GDN_WIDTH:2 * GDN_WIDTH])
    v = y_ref[:, 2 * GDN_WIDTH:]
    for h in range(GDN_HEADS):
        q_ref[0, h] = qn[:, h * GDN_DK:(h + 1) * GDN_DK]
        k_ref[0, h] = kn[:, h * GDN_DK:(h + 1) * GDN_DK]
        v_ref[0, h] = v[:, h * GDN_DK:(h + 1) * GDN_DK]
    ab = ab_ref[0]
    lane = lax.broadcasted_iota(jnp.int32, ab.shape, 1)
    xa = ab + dtb_ref[...]
    softplus = jnp.maximum(xa, 0.0) + jnp.log(1.0 + jnp.exp(-jnp.abs(xa)))
    g = -jnp.exp(alog_ref[...]) * softplus
    beta = jax.nn.sigmoid(ab)
    gb_ref[0] = jnp.where(lane < 2 * GDN_HEADS, g, jnp.where(lane < 4 * GDN_HEADS, beta, 0.0))


def _gdn_features(qkv, ab, cw, alog, dtb, bd, nct):
    nb, s, _ = qkv.shape
    nt = s // TM
    head = pl.BlockSpec((1, GDN_HEADS, TM, GDN_DK), lambda b, t: (b, 0, t, 0))
    hshape = jax.ShapeDtypeStruct((nb, GDN_HEADS, s, GDN_DK), F32)
    return pl.pallas_call(
        functools.partial(_gdn_feat_kernel, nct=nct, nt=nt),
        grid=(nb, nt),
        in_specs=_halo_specs(GDN_QKV, SHORT_HALO, s) + [
            pl.BlockSpec((8, GDN_QKV), lambda b, t: (0, 0)),
            pl.BlockSpec((1, TM, LANES), lambda b, t: (b, t, 0)),
            pl.BlockSpec((1, LANES), lambda b, t: (0, 0)),
            pl.BlockSpec((1, LANES), lambda b, t: (0, 0)),
            pl.BlockSpec((GDN_WIDTH, GDN_WIDTH), lambda b, t: (0, 0))],
        out_specs=[head, head, head, pl.BlockSpec((1, TM, LANES), lambda b, t: (b, t, 0))],
        out_shape=[hshape, hshape, hshape, jax.ShapeDtypeStruct((nb, s, LANES), F32)],
        scratch_shapes=[pltpu.VMEM((TM + 2 * SHORT_HALO, GDN_QKV), F32), pltpu.VMEM((TM, GDN_QKV), F32)],
        compiler_params=_cparams(("parallel", "parallel")),
        name="gdn_features",
    )(qkv, qkv, qkv, cw, ab, alog, dtb, bd)


def _gdn_scan_kernel(qf_ref, kf_ref, vf_ref, gf_ref, qb_ref, kb_ref, vb_ref, gbk_ref,
                     of_ref, ob_ref, s_ref):
    @pl.when(pl.program_id(0) == 0)
    def _():
        s_ref[...] = jnp.zeros(s_ref.shape, F32)

    c = GDN_C
    row = lax.broadcasted_iota(jnp.int32, (c, c), 0)
    col = lax.broadcasted_iota(jnp.int32, (c, c), 1)
    eye = (row == col).astype(F32)
    blk = {2 ** e: (row >> e) == (col >> e) for e in range(1, int(math.log2(c)) + 1)}

    chains = []
    for b, d in ((b, d) for b in range(qf_ref.shape[0]) for d in range(2)):
        q_ref, k_ref, v_ref, g_ref, o_ref = ((qf_ref, kf_ref, vf_ref, gf_ref, of_ref),
                                             (qb_ref, kb_ref, vb_ref, gbk_ref, ob_ref))[d]
        incl = (row >= col) if d == 0 else (row <= col)
        strict = (row > col) if d == 0 else (row < col)
        gb = g_ref[b]
        gam_all = _dot(incl.astype(F32), gb, precision=HIGHEST)
        gam_t = gam_all.T
        last_row = c - 1 if d == 0 else 0
        for h in range(GDN_HEADS):
            cc = d * GDN_HEADS + h
            gam = gam_all[:, cc:cc + 1]
            beta = gb[:, 2 * GDN_HEADS + cc:2 * GDN_HEADS + cc + 1]
            kh = k_ref[b, h]
            chains.append(dict(
                b=b, d=d, h=h, o_ref=o_ref, strict=strict, gam=gam, beta=beta, kh=kh,
                qh=q_ref[b, h], vh=v_ref[b, h],
                last=gam_all[last_row:last_row + 1, cc:cc + 1],
                decay=jnp.exp(jnp.where(incl, gam - gam_t[cc:cc + 1, :], -1e30)),
                kbeta=kh * beta))
    for ch in chains:
        ch["a"] = jnp.where(ch["strict"], _dot_nt(ch["kbeta"], ch["kh"]) * ch["decay"], 0.0)
    for ch in chains:
        ch["t"] = eye - jnp.where(blk[2], ch["a"], 0.0)
    bsz = 2
    while bsz < c:
        off = jnp.logical_and(blk[2 * bsz], jnp.logical_not(blk[bsz]))
        for ch in chains:
            ch["tl"] = _dot(ch["t"], jnp.where(off, ch["a"], 0.0))
        for ch in chains:
            ch["t"] = ch["t"] - _dot(ch["tl"], ch["t"])
        bsz *= 2
    for ch in chains:
        rhs = jnp.concatenate([ch["vh"] * ch["beta"], ch["kbeta"] * jnp.exp(ch["gam"])], axis=1)
        ch["uw"] = _dot(ch["t"], rhs)
    for ch in chains:
        ch["pm"] = _dot_nt(ch["qh"], ch["kh"]) * ch["decay"]
    for ch in chains:
        ch["s"] = s_ref[ch["b"], ch["d"], ch["h"]]
        ch["v_new"] = ch["uw"][:, :GDN_DK] - _dot(ch["uw"][:, GDN_DK:], ch["s"])
    for ch in chains:
        h = ch["h"]
        qg = ch["qh"] * jnp.exp(ch["gam"])
        ch["o_ref"][ch["b"], :, h * GDN_DK:(h + 1) * GDN_DK] = _dot(qg, ch["s"]) + _dot(ch["pm"], ch["v_new"])
    for ch in chains:
        kd = ch["kh"] * jnp.exp(ch["last"] - ch["gam"])
        s_ref[ch["b"], ch["d"], ch["h"]] = ch["s"] * jnp.exp(ch["last"]) + _dot_tn(kd, ch["v_new"])


def _gdn_scan(q, k, v, gb, ncc):
    nb, _, s, _ = q.shape
    nc = s // GDN_C

    def bwd(n):
        return jnp.where(n < ncc, ncc - 1 - n, nc + ncc - 1 - n)

    hf = pl.BlockSpec((nb, GDN_HEADS, GDN_C, GDN_DK), lambda n: (0, 0, n, 0))
    hb = pl.BlockSpec((nb, GDN_HEADS, GDN_C, GDN_DK), lambda n: (0, 0, bwd(n), 0))
    gf = pl.BlockSpec((nb, GDN_C, LANES), lambda n: (0, n, 0))
    gk = pl.BlockSpec((nb, GDN_C, LANES), lambda n: (0, bwd(n), 0))
    of = pl.BlockSpec((nb, GDN_C, GDN_WIDTH), lambda n: (0, n, 0))
    ob = pl.BlockSpec((nb, GDN_C, GDN_WIDTH), lambda n: (0, bwd(n), 0))
    oshape = jax.ShapeDtypeStruct((nb, s, GDN_WIDTH), F32)
    return pl.pallas_call(
        _gdn_scan_kernel,
        grid=(nc,),
        in_specs=[hf, hf, hf, gf, hb, hb, hb, gk],
        out_specs=[of, ob],
        out_shape=[oshape, oshape],
        scratch_shapes=[pltpu.VMEM((nb, 2, GDN_HEADS, GDN_DK, GDN_DK), F32)],
        compiler_params=_cparams(("arbitrary",)),
        name="gdn_scan",
    )(q, k, v, gb, q, k, v, gb)


def _attn_prep_kernel(q_ref, kv_ref, cos_ref, sin_ref, qg_ref, kg_ref, bd_ref,
                      qt_ref, kh_ref, vt_ref, *, q_scale):
    cos = cos_ref[...]
    sin = sin_ref[...]
    bd = bd_ref[...]
    lane = lax.broadcasted_iota(jnp.int32, cos.shape, 1)
    first = (lane % 32) < 16

    def norm_rope(x, g):
        ss = _head_sums(x * x, bd)
        y = x * lax.rsqrt(ss * (1.0 / HEAD_DIM) + EPS) * g
        swapped = jnp.where(first, pltpu.roll(y, LANES - 16, 1), pltpu.roll(y, 16, 1))
        return y * cos + swapped * sin

    q = q_ref[0]
    for j in range(ATTN_WIDTH // LANES):
        yt = (norm_rope(q[:, j * LANES:(j + 1) * LANES], qg_ref[...]) * q_scale).T
        qt_ref[0, 2 * j] = yt[:HEAD_DIM].astype(BF16)
        qt_ref[0, 2 * j + 1] = yt[HEAD_DIM:].astype(BF16)
    kv = kv_ref[0]
    y = norm_rope(kv[:, :LANES], kg_ref[...])
    kh_ref[0, 0] = y[:, :HEAD_DIM].astype(BF16)
    kh_ref[0, 1] = y[:, HEAD_DIM:].astype(BF16)
    vt = kv[:, LANES:].T.astype(BF16)
    ones = jnp.ones((VT_ROWS - HEAD_DIM, ATT_TK), BF16)
    for g in range(ATTN_KV_HEADS):
        for cidx in range(TM // ATT_TK):
            vt_ref[0, g, cidx, 0:HEAD_DIM, :] = vt[g * HEAD_DIM:(g + 1) * HEAD_DIM,
                                                   cidx * ATT_TK:(cidx + 1) * ATT_TK]
            vt_ref[0, g, cidx, HEAD_DIM:VT_ROWS, :] = ones


def _attn_prep(q, kv, cos_t, sin_t, qg, kg, bd, q_scale):
    nb, s, _ = q.shape
    nt = s // TM
    tab = pl.BlockSpec((TM, LANES), lambda b, t: (t, 0))
    row = pl.BlockSpec((1, LANES), lambda b, t: (0, 0))
    per = TM // ATT_TK
    return pl.pallas_call(
        functools.partial(_attn_prep_kernel, q_scale=q_scale),
        grid=(nb, nt),
        in_specs=[pl.BlockSpec((1, TM, ATTN_WIDTH), lambda b, t: (b, t, 0)),
                  pl.BlockSpec((1, TM, 2 * LANES), lambda b, t: (b, t, 0)),
                  tab, tab, row, row,
                  pl.BlockSpec((LANES, LANES), lambda b, t: (0, 0))],
        out_specs=[pl.BlockSpec((1, ATTN_Q_HEADS, HEAD_DIM, TM), lambda b, t: (b, 0, 0, t)),
                   pl.BlockSpec((1, ATTN_KV_HEADS, TM, HEAD_DIM), lambda b, t: (b, 0, t, 0)),
                   pl.BlockSpec((1, ATTN_KV_HEADS, per, VT_ROWS, ATT_TK), lambda b, t: (b, 0, t, 0, 0))],
        out_shape=[jax.ShapeDtypeStruct((nb, ATTN_Q_HEADS, HEAD_DIM, s), BF16),
                   jax.ShapeDtypeStruct((nb, ATTN_KV_HEADS, s, HEAD_DIM), BF16),
                   jax.ShapeDtypeStruct((nb, ATTN_KV_HEADS, s // ATT_TK, VT_ROWS, ATT_TK), BF16)],
        compiler_params=_cparams(("parallel", "parallel")),
        name="attn_prep",
    )(q, kv, cos_t, sin_t, qg, kg, bd)


def _attn_kernel(qt_ref, k_ref, vt_ref, o_ref, m_ref, acc_ref, s_ref, mx_ref, *, nctq, lc, s_len):
    qi = pl.program_id(2)
    m_ref[...] = jnp.full(m_ref.shape, -jnp.inf, F32)
    acc_ref[...] = jnp.zeros(acc_ref.shape, F32)
    nk = jnp.where(qi < nctq, lc // ATT_TK, s_len // ATT_TK)

    def score(j, slot):
        ks = pl.multiple_of(j * ATT_TK, ATT_TK)
        k = k_ref[0, 0, pl.ds(ks, ATT_TK), :]
        for h in range(ATTN_GROUP):
            s = _dot(k, qt_ref[0, h])
            s_ref[slot, h] = s
            mx_ref[slot, h] = jnp.max(s, axis=0, keepdims=True)

    def step(j, slot):
        score(jnp.minimum(j + 1, nk - 1), 1 - slot)
        vt = vt_ref[0, 0, j]
        for h in range(ATTN_GROUP):
            s = s_ref[slot, h]
            m_old = m_ref[h]
            m_new = jnp.maximum(m_old, mx_ref[slot, h])
            alpha = jnp.exp2(m_old - m_new)
            p = jnp.exp2(s - m_new)
            acc_ref[h] = alpha * acc_ref[h] + _dot(vt, p.astype(BF16))
            m_ref[h] = m_new

    def quad(i, carry):
        for u in range(4):
            step(4 * i + u, u % 2)
        return carry

    score(0, 0)
    lax.fori_loop(0, nk // 4, quad, 0)
    done = (nk // 4) * 4

    @pl.when(nk - done >= 2)
    def _():
        step(done, 0)
        step(done + 1, 1)

    @pl.when(nk % 2 == 1)
    def _():
        step(nk - 1, 0)

    for hp in range(ATTN_GROUP // 2):
        pair = jnp.concatenate(
            [acc_ref[h, 0:HEAD_DIM, :] / acc_ref[h, HEAD_DIM:HEAD_DIM + 1, :] for h in (2 * hp, 2 * hp + 1)], axis=0)
        o_ref[0, :, hp * LANES:(hp + 1) * LANES] = pair.T


def _attention(qt, kh, vt, lc):
    nb, _, _, s = qt.shape
    nq = s // ATT_TQ
    return pl.pallas_call(
        functools.partial(_attn_kernel, nctq=lc // ATT_TQ, lc=lc, s_len=s),
        grid=(nb, ATTN_KV_HEADS, nq),
        in_specs=[pl.BlockSpec((1, ATTN_GROUP, HEAD_DIM, ATT_TQ), lambda b, g, i: (b, g, 0, i)),
                  pl.BlockSpec((1, 1, s, HEAD_DIM), lambda b, g, i: (b, g, 0, 0)),
                  pl.BlockSpec((1, 1, s // ATT_TK, VT_ROWS, ATT_TK), lambda b, g, i: (b, g, 0, 0, 0))],
        out_specs=pl.BlockSpec((1, ATT_TQ, ATTN_GROUP * HEAD_DIM), lambda b, g, i: (b, i, g)),
        out_shape=jax.ShapeDtypeStruct((nb, s, ATTN_WIDTH), F32),
        scratch_shapes=[pltpu.VMEM((ATTN_GROUP, 1, ATT_TQ), F32),
                        pltpu.VMEM((ATTN_GROUP, VT_ROWS, ATT_TQ), F32),
                        pltpu.VMEM((2, ATTN_GROUP, ATT_TK, ATT_TQ), F32),
                        pltpu.VMEM((2, ATTN_GROUP, 1, ATT_TQ), F32)],
        compiler_params=_cparams(("parallel", "parallel", "parallel")),
        name="attention",
    )(qt, kh, vt)


def _outproj_kernel(conv_ref, of_ref, ob_ref, z_ref, attn_ref, x_ref, mod_ref, gg_ref, bd_ref, w_ref,
                    n2_ref, wr_ref, br_ref, xo_ref, h2_ref, ri_ref, rw_ref):
    o = of_ref[0] + ob_ref[0]
    ss = _head_sums(o * o, bd_ref[...])
    gdn = o * lax.rsqrt(ss * (1.0 / GDN_DK) + EPS) * gg_ref[...] * _silu(z_ref[0])
    mix = jnp.concatenate([conv_ref[0].astype(BF16), gdn.astype(BF16), attn_ref[0].astype(BF16)], axis=1)
    y = _dot(mix, w_ref[...])
    x = x_ref[0] + mod_ref[0, 2:3, :] * y
    xo_ref[0] = x
    h2 = _rms(x, n2_ref[...]) * (1.0 + mod_ref[0, 4:5, :]) + mod_ref[0, 3:4, :]
    h2_ref[0] = h2
    h_hi = h2.astype(BF16)
    h_lo = (h2 - h_hi.astype(F32)).astype(BF16)
    t = _dot(h_hi, wr_ref[...])
    lg = t[:, :LANES] + t[:, LANES:] + _dot(h_lo, wr_ref[:, 0:LANES]) + br_ref[...]
    lane = lax.broadcasted_iota(jnp.int32, lg.shape, 1)
    lane_f = lane.astype(F32)

    def first_max(v):
        mx = jnp.max(v, axis=-1, keepdims=True)
        return mx, jnp.min(jnp.where(v == mx, lane_f, float(LANES)), axis=-1, keepdims=True)

    lgg = jnp.where(lane < N_GROUPS, lg, -jnp.inf)
    gmax, grp = first_max(lgg)
    pg_sel = 1.0 / jnp.sum(jnp.exp(lgg - gmax), axis=-1, keepdims=True)
    in_grp = jnp.logical_and(lane >= N_GROUPS, ((lane - N_GROUPS) >> 3).astype(F32) == grp)
    le = jnp.where(in_grp, lg, -jnp.inf)
    e1max, i1 = first_max(le)
    e2max, i2 = first_max(jnp.where(lane_f == i1, -jnp.inf, le))
    r = jnp.exp(e2max - e1max)
    w1 = pg_sel / (1.0 + r)
    ri_ref[0] = jnp.where(lane == 0, i1, jnp.where(lane == 1, i2, float(N_GROUPS))).astype(jnp.int32) - N_GROUPS
    rw_ref[0] = jnp.where(lane == 0, w1, jnp.where(lane == 1, w1 * r, 0.0))


def _out_proj(conv, o_f, o_b, z, attn, x, mod, gdn_g, bd, w_out, n2, wr, br, nct):
    nb, s, d = x.shape
    nt = s // TM
    tok = lambda w: pl.BlockSpec((1, TM, w), lambda b, t: (b, t, 0))
    full = lambda a: pl.BlockSpec(a.shape, lambda b, t: (0,) * a.ndim)
    return pl.pallas_call(
        _outproj_kernel,
        grid=(nb, nt),
        in_specs=[tok(CONV_CH), tok(GDN_WIDTH), tok(GDN_WIDTH), tok(GDN_WIDTH), tok(ATTN_WIDTH), tok(d),
                  pl.BlockSpec((1, 6, d), lambda b, t: (jnp.where(t < nct, nb, b), 0, 0)),
                  full(gdn_g), full(bd), full(w_out), full(n2), full(wr), full(br)],
        out_specs=[tok(d), tok(d), tok(LANES), tok(LANES)],
        out_shape=[jax.ShapeDtypeStruct((nb, s, d), F32), jax.ShapeDtypeStruct((nb, s, d), F32),
                   jax.ShapeDtypeStruct((nb, s, LANES), jnp.int32), jax.ShapeDtypeStruct((nb, s, LANES), F32)],
        compiler_params=_cparams(("parallel", "parallel")),
        name="out_proj",
    )(conv, o_f, o_b, z, attn, x, mod, gdn_g, bd, w_out, n2, wr, br)


def _dispatch_kernel(slot_ref, h_ref, xe_in_ref, xe_ref, sem):
    del xe_in_ref
    base = pl.program_id(0) * (TM * TOP_K)

    def row_copy(r, k):
        dst = slot_ref[base + TOP_K * r + k]
        return pltpu.make_async_copy(h_ref.at[pl.ds(r, 1)], xe_ref.at[pl.ds(dst, 1)], sem)

    def issue(r, carry):
        for k in range(TOP_K):
            row_copy(r, k).start()
        return carry

    lax.fori_loop(0, TM, issue, 0, unroll=8)
    for _ in range(TOP_K):
        pltpu.make_async_copy(h_ref, xe_ref.at[pl.ds(0, TM)], sem).wait()


def _dispatch(h2, slots, n_slots):
    n_tok, d = h2.shape
    grid_spec = pltpu.PrefetchScalarGridSpec(
        num_scalar_prefetch=1,
        grid=(n_tok // TM,),
        in_specs=[pl.BlockSpec((TM, d), lambda i, sl: (i, 0)),
                  pl.BlockSpec(memory_space=pl.ANY)],
        out_specs=pl.BlockSpec(memory_space=pl.ANY),
        scratch_shapes=[pltpu.SemaphoreType.DMA(())],
    )
    return pl.pallas_call(
        _dispatch_kernel,
        grid_spec=grid_spec,
        out_shape=jax.ShapeDtypeStruct((n_slots, d), F32),
        input_output_aliases={2: 0},
        compiler_params=_cparams(("arbitrary",)),
        name="moe_dispatch",
    )(slots, h2, jnp.zeros((n_slots, d), F32))


def _expert_kernel(be_ref, nu_ref, x_ref, wg_ref, wu_ref, wd_ref, o_ref, wg_s, wu_s, wd_s):
    i = pl.program_id(0)
    used = i < nu_ref[0]
    new_expert = jnp.logical_or(i == 0, be_ref[i] != be_ref[jnp.maximum(i - 1, 0)])

    @pl.when(jnp.logical_and(used, new_expert))
    def _():
        wg_s[...] = wg_ref[0, 0].astype(BF16)
        wu_s[...] = wu_ref[0, 0].astype(BF16)
        wd_s[...] = wd_ref[0, 0].astype(BF16)

    @pl.when(used)
    def _():
        x = x_ref[...].astype(BF16)
        g = _dot(x, wg_s[...])
        u = _dot(x, wu_s[...])
        h = (_silu(g) * u).astype(BF16)
        o_ref[...] = _dot(h, wd_s[...])

    @pl.when(i >= nu_ref[0])
    def _():
        o_ref[...] = jnp.zeros(o_ref.shape, F32)


def _experts(xs, blk_e, n_used, w_gate, w_up, w_down, layer):
    n_slots, d = xs.shape
    nblk = n_slots // MOE_BM
    de = w_gate.shape[-1]
    grid_spec = pltpu.PrefetchScalarGridSpec(
        num_scalar_prefetch=2,
        grid=(nblk,),
        in_specs=[pl.BlockSpec((MOE_BM, d), lambda i, be, nu: (jnp.minimum(i, nu[0] - 1), 0)),
                  pl.BlockSpec((1, 1, d, de), lambda i, be, nu: (layer, be[i], 0, 0)),
                  pl.BlockSpec((1, 1, d, de), lambda i, be, nu: (layer, be[i], 0, 0)),
                  pl.BlockSpec((1, 1, de, d), lambda i, be, nu: (layer, be[i], 0, 0))],
        out_specs=pl.BlockSpec((MOE_BM, d), lambda i, be, nu: (i, 0)),
        scratch_shapes=[pltpu.VMEM((d, de), BF16), pltpu.VMEM((d, de), BF16), pltpu.VMEM((de, d), BF16)],
    )
    return pl.pallas_call(
        _expert_kernel,
        grid_spec=grid_spec,
        out_shape=jax.ShapeDtypeStruct((n_slots, d), F32),
        compiler_params=_cparams(("arbitrary",)),
        name="experts",
    )(blk_e, n_used, xs, w_gate, w_up, w_down)


def _combine_kernel(slot_ref, x_ref, rw_ref, mod_ref, fg_ref, ye_ref, o_ref, buf_ref, sem, *, final, nt, ntile):
    g = pl.program_id(0) * nt + pl.program_id(1)

    def row_copy(tile, r, k, par):
        src = slot_ref[tile * (TM * TOP_K) + TOP_K * r + k]
        return pltpu.make_async_copy(ye_ref.at[pl.ds(src, 1)], buf_ref.at[par, k, pl.ds(r, 1)], sem.at[par, k])

    def issue_tile(tile, par):
        def issue(r, carry):
            for k in range(TOP_K):
                row_copy(tile, r, k, par).start()
            return carry
        lax.fori_loop(0, TM, issue, 0, unroll=8)

    def finish(par):
        for k in range(TOP_K):
            pltpu.make_async_copy(ye_ref.at[pl.ds(0, TM)], buf_ref.at[par, k], sem.at[par, k]).wait()
        rw = rw_ref[0]
        y = rw[:, 0:1] * buf_ref[par, 0] + rw[:, 1:2] * buf_ref[par, 1]
        x = x_ref[0] + mod_ref[0, 5:6, :] * y
        if final:
            x = _rms(x, fg_ref[...])
        o_ref[0] = x

    @pl.when(g == 0)
    def _():
        issue_tile(0, 0)

    for par in range(2):
        @pl.when(g % 2 == par)
        def _(par=par):
            @pl.when(g + 1 < ntile)
            def _():
                issue_tile(g + 1, 1 - par)

            finish(par)


def _combine(x, ye, slots, rw, mod, fg, nct, final):
    nb, s, d = x.shape
    nt = s // TM
    tok = lambda w: pl.BlockSpec((1, TM, w), lambda b, t, sl: (b, t, 0))
    grid_spec = pltpu.PrefetchScalarGridSpec(
        num_scalar_prefetch=1,
        grid=(nb, nt),
        in_specs=[tok(d), tok(LANES),
                  pl.BlockSpec((1, 6, d), lambda b, t, sl: (jnp.where(t < nct, nb, b), 0, 0)),
                  pl.BlockSpec((1, d), lambda b, t, sl: (0, 0)),
                  pl.BlockSpec(memory_space=pl.ANY)],
        out_specs=tok(d),
        scratch_shapes=[pltpu.VMEM((2, TOP_K, TM, d), F32), pltpu.SemaphoreType.DMA((2, TOP_K))],
    )
    return pl.pallas_call(
        functools.partial(_combine_kernel, final=final, nt=nt, ntile=nb * nt),
        grid_spec=grid_spec,
        out_shape=jax.ShapeDtypeStruct((nb, s, d), F32),
        compiler_params=_cparams(("arbitrary", "arbitrary")),
        name="moe_combine",
    )(slots, x, rw, mod, fg, ye)


def _rank_kernel(ri_ref, rank_ref, cnt_ref, carry_ref):
    @pl.when(pl.program_id(0) == 0)
    def _():
        carry_ref[...] = jnp.zeros(carry_ref.shape, F32)

    ri = ri_ref[0]
    lane = lax.broadcasted_iota(jnp.int32, ri.shape, 1)
    oh1 = lane == ri[:, 0:1]
    oh2 = lane == ri[:, 1:2]
    oh = jnp.logical_or(oh1, oh2).astype(F32)
    row = lax.broadcasted_iota(jnp.int32, (TM, TM), 0)
    col = lax.broadcasted_iota(jnp.int32, (TM, TM), 1)
    before = carry_ref[0:1, :] + _dot((row > col).astype(BF16), oh.astype(BF16))
    r1 = jnp.sum(jnp.where(oh1, before, 0.0), axis=-1, keepdims=True)
    r2 = jnp.sum(jnp.where(oh2, before, 0.0), axis=-1, keepdims=True)
    rank_ref[0] = jnp.where(lane == 0, r1, jnp.where(lane == 1, r2, 0.0)).astype(jnp.int32)
    carry_ref[...] = carry_ref[...] + jnp.sum(oh, axis=0, keepdims=True)
    cnt_ref[...] = carry_ref[...]


def _slot_kernel(ri_ref, rank_ref, ps_ref, slot_ref):
    ri = ri_ref[0]
    rk = rank_ref[0]
    lane = lax.broadcasted_iota(jnp.int32, ri.shape, 1)
    ps = ps_ref[...]
    s1 = jnp.sum(jnp.where(lane == ri[:, 0:1], ps, 0.0), axis=-1, keepdims=True).astype(jnp.int32) + rk[:, 0:1]
    s2 = jnp.sum(jnp.where(lane == ri[:, 1:2], ps, 0.0), axis=-1, keepdims=True).astype(jnp.int32) + rk[:, 1:2]
    slot_ref[0] = jnp.where(lane == 0, s1, jnp.where(lane == 1, s2, 0))


def _route(ri, n_tok):
    ntile = n_tok // TM
    ri = ri.reshape(ntile, TM, LANES)
    tile = pl.BlockSpec((1, TM, LANES), lambda i: (i, 0, 0))
    rank, cnt = pl.pallas_call(
        _rank_kernel,
        grid=(ntile,),
        in_specs=[tile],
        out_specs=[tile, pl.BlockSpec((8, LANES), lambda i: (0, 0))],
        out_shape=[jax.ShapeDtypeStruct((ntile, TM, LANES), jnp.int32), jax.ShapeDtypeStruct((8, LANES), F32)],
        scratch_shapes=[pltpu.VMEM((8, LANES), F32)],
        compiler_params=_cparams(("arbitrary",)),
        name="moe_rank",
    )(ri)
    counts = cnt[0, :N_EXPERTS].astype(jnp.int32)
    pcounts = (counts + MOE_BM - 1) // MOE_BM * MOE_BM
    pends = jnp.cumsum(pcounts)
    pstarts = jnp.zeros((1, LANES), F32).at[0, :N_EXPERTS].set((pends - pcounts).astype(F32))
    slot = pl.pallas_call(
        _slot_kernel,
        grid=(ntile,),
        in_specs=[tile, tile, pl.BlockSpec((1, LANES), lambda i: (0, 0))],
        out_specs=tile,
        out_shape=jax.ShapeDtypeStruct((ntile, TM, LANES), jnp.int32),
        compiler_params=_cparams(("parallel",)),
        name="moe_slot",
    )(ri, rank, pstarts)
    slots = slot.reshape(n_tok, LANES)[:, :TOP_K].reshape(-1)
    n_slots = (-(-n_tok * TOP_K // MOE_BM) + N_EXPERTS) * MOE_BM
    blk_start = jnp.arange(n_slots // MOE_BM, dtype=jnp.int32) * MOE_BM
    blk_e = jnp.minimum(jnp.sum((pends[None, :] <= blk_start[:, None]).astype(jnp.int32), axis=1), N_EXPERTS - 1)
    n_used = (pends[-1] // MOE_BM).astype(jnp.int32).reshape(1)
    return slots, n_slots, blk_e, n_used


def _rope_tables(lc, l):
    rows = l // GRID_W
    row = jnp.repeat(jnp.arange(rows), GRID_W).astype(F32)
    col = (jnp.arange(rows * GRID_W) % GRID_W).astype(F32)
    n_freq = HEAD_DIM // 4
    inv = ROPE_THETA ** (-jnp.arange(n_freq, dtype=F32) / n_freq)
    ar, ac = row[:, None] * inv, col[:, None] * inv
    cos = jnp.concatenate([jnp.cos(ar), jnp.cos(ar), jnp.cos(ac), jnp.cos(ac)], axis=-1)
    sin = jnp.concatenate([-jnp.sin(ar), jnp.sin(ar), -jnp.sin(ac), jnp.sin(ac)], axis=-1)
    cos = jnp.concatenate([jnp.ones((lc, HEAD_DIM), F32), cos], axis=0)
    sin = jnp.concatenate([jnp.zeros((lc, HEAD_DIM), F32), sin], axis=0)
    return jnp.tile(cos, (1, 2)), jnp.tile(sin, (1, 2))


def _block_diag_ones(n, blk):
    i = jnp.arange(n)
    return (i[:, None] // blk == i[None, :] // blk).astype(BF16)


def kernel(x, c, ctx, c_ctx, mod_w, mod_b, norm1_g, norm2_g, w_in, conv_dw_w, conv_dw_b, conv_ln_g, conv_ln_b, conv_pw_w, conv_pw_b, gdn_conv_w, gdn_a_log, gdn_dt_bias, gdn_norm_g, attn_q_norm_g, attn_k_norm_g, w_out, router_group_w, router_group_b, router_expert_w, router_expert_b, expert_w_gate, expert_w_up, expert_w_down, final_norm_g):
    nb, l, d = x.shape
    lc = ctx.shape[1]
    depth = mod_w.shape[0]
    s = lc + l
    assert lc % TM == 0 and l % TM == 0 and lc % GDN_C == 0 and l % GDN_C == 0
    assert lc % ATT_TK == 0 and s % ATT_TK == 0 and lc % ATT_TQ == 0
    nct = lc // TM
    n_tok = nb * s

    nr = -(-(nb + 1) // 8) * 8
    cvec = jnp.zeros((nr, d), F32).at[:nb].set(c).at[nb].set(c_ctx)
    mod_all = _modulation(cvec, mod_w, mod_b).reshape(depth, nr, 6, d)

    cos_t, sin_t = _rope_tables(lc, l)
    bd64 = _block_diag_ones(LANES, HEAD_DIM)
    bd_gdn = _block_diag_ones(GDN_WIDTH, GDN_DK)
    q_scale = (HEAD_DIM ** -0.5) * math.log2(math.e)
    row = lambda v: v.reshape(1, -1).astype(F32)
    pad_lanes = lambda v: jnp.zeros((1, LANES), F32).at[0, :v.size].set(v.reshape(-1))

    xs = jnp.concatenate([ctx, x], axis=1)
    for layer in range(depth):
        mod = mod_all[layer]
        wi = w_in[layer]
        w_in_p = jnp.concatenate(
            [wi[:, :IN_A0 + 16], jnp.zeros((d, IN_PAD), F32), wi[:, IN_A0 + 16:]], axis=1).astype(BF16)
        qkv, ab, kv, gvgg, z, q = _in_proj(xs, mod, row(norm1_g[layer]), w_in_p, nct)

        dww = jnp.concatenate([conv_dw_w[layer], jnp.zeros((1, CONV_CH), F32)], axis=0)
        conv = _conformer(gvgg, dww, row(conv_dw_b[layer]), row(conv_ln_g[layer]), row(conv_ln_b[layer]),
                          conv_pw_w[layer].astype(BF16), row(conv_pw_b[layer]), nct)

        cw = jnp.concatenate([gdn_conv_w[layer], jnp.zeros((8 - SHORT_CONV, GDN_QKV), F32)], axis=0)
        gq, gk, gv, gb = _gdn_features(qkv, ab, cw, pad_lanes(gdn_a_log[layer]), pad_lanes(gdn_dt_bias[layer]),
                                       bd_gdn, nct)
        o_f, o_b = _gdn_scan(gq, gk, gv, gb, lc // GDN_C)

        qt, kh, vt = _attn_prep(q, kv, cos_t, sin_t,
                                row(jnp.tile(attn_q_norm_g[layer], 2)), row(jnp.tile(attn_k_norm_g[layer], 2)),
                                bd64, q_scale)
        attn = _attention(qt, kh, vt, lc)

        wr = jnp.zeros((d, LANES), F32).at[:, :N_GROUPS].set(router_group_w[layer])
        wr = wr.at[:, N_GROUPS:N_GROUPS + N_EXPERTS].set(router_expert_w[layer])
        wr_hi = wr.astype(BF16)
        wr = jnp.concatenate([wr_hi, (wr - wr_hi.astype(F32)).astype(BF16)], axis=1)
        br =jnp.zeros((1, LANES), F32).at[0, :N_GROUPS].set(router_group_b[layer])
        br = br.at[0, N_GROUPS:N_GROUPS + N_EXPERTS].set(router_expert_b[layer])
        xs, h2, ri, rw = _out_proj(conv, o_f, o_b, z, attn, xs, mod,
                                   row(jnp.tile(gdn_norm_g[layer], GDN_HEADS)), bd_gdn,
                                   w_out[layer].astype(BF16), row(norm2_g[layer]), wr, br, nct)

        slots, n_slots, blk_e, n_used = _route(ri, n_tok)
        xe = _dispatch(h2.reshape(n_tok, d), slots, n_slots)
        ye = _experts(xe, blk_e, n_used, expert_w_gate, expert_w_up, expert_w_down, layer)
        xs = _combine(xs, ye, slots, rw, mod, row(final_norm_g), nct, layer == depth - 1)
    return xs[:, lc:, :]
```

```python
import functools
import math

import jax
import jax.numpy as jnp
from jax import lax
from jax.experimental import pallas as pl
from jax.experimental.pallas import tpu as pltpu

F32 = jnp.float32
BF16 = jnp.bfloat16
HIGHEST = lax.Precision.HIGHEST

EPS = 1e-6
GRID_W = 64
CONV_CH = 256
CONV_WIDTH = 31
GDN_HEADS = 4
GDN_DK = 64
GDN_WIDTH = 256
GDN_QKV = 768
SHORT_CONV = 5
HEAD_DIM = 64
ATTN_Q_HEADS = 8
ATTN_KV_HEADS = 2
ATTN_GROUP = 4
ATTN_WIDTH = 512
ROPE_THETA = 10000.0
N_GROUPS = 4
EXPERTS_PER_GROUP = 8
N_EXPERTS = 32
TOP_K = 2
D_EXPERT = 512

LANES = 128
TM = 256
GDN_C = 128
CONV_HALO = 16
SHORT_HALO = 8
ATT_TQ = 256
ATT_TK = 256
VT_ROWS = 80
MOE_BM = 256
VMEM_LIMIT = 56 * 1024 * 1024

IN_A0 = GDN_QKV
IN_PAD = LANES - 16
C_QKV = (0, 768)
C_AB = (768, 896)
C_KV = (896, 1152)
C_GVGG = (1152, 1664)
C_Z = (1664, 1920)
C_Q = (1920, 2432)
IN_COLS = 2432


def _cparams(sem):
    return pltpu.CompilerParams(dimension_semantics=sem, vmem_limit_bytes=VMEM_LIMIT)


def _silu(x):
    return x * jax.nn.sigmoid(x)


def _dot(a, b, **kw):
    return jnp.dot(a, b, preferred_element_type=F32, **kw)


def _dot_nt(a, b):
    return lax.dot_general(a, b, (((1,), (1,)), ((), ())), preferred_element_type=F32)


def _dot_tn(a, b):
    return lax.dot_general(a, b, (((0,), (0,)), ((), ())), preferred_element_type=F32)


def _mod_kernel(c_ref, w_ref, b_ref, o_ref):
    c = c_ref[...]
    o_ref[0] = _dot(_silu(c), w_ref[0], precision=HIGHEST) + b_ref[0]


def _modulation(cvec, mod_w, mod_b):
    depth, d, n = mod_w.shape
    nr = cvec.shape[0]
    tn = 768
    return pl.pallas_call(
        _mod_kernel,
        grid=(depth, n // tn),
        in_specs=[pl.BlockSpec((nr, d), lambda l, j: (0, 0)),
                  pl.BlockSpec((1, d, tn), lambda l, j: (l, 0, j)),
                  pl.BlockSpec((1, 1, tn), lambda l, j: (l, 0, j))],
        out_specs=pl.BlockSpec((1, nr, tn), lambda l, j: (l, 0, j)),
        out_shape=jax.ShapeDtypeStruct((depth, nr, n), F32),
        compiler_params=_cparams(("parallel", "parallel")),
        name="modulation",
    )(cvec, mod_w, mod_b.reshape(depth, 1, n))


def _rms(x, g):
    return x * lax.rsqrt(jnp.mean(x * x, axis=-1, keepdims=True) + EPS) * g


def _head_sums(sq, bd):
    hi = sq.astype(BF16)
    lo = (sq - hi.astype(F32)).astype(BF16)
    return _dot(hi, bd) + _dot(lo, bd)


def _inproj_kernel(x_ref, mod_ref, g_ref, w_ref, qkv_ref, ab_ref, kv_ref, gvgg_ref, z_ref, q_ref):
    x = x_ref[0]
    sh = mod_ref[0, 0:1, :]
    sc = mod_ref[0, 1:2, :]
    h = (_rms(x, g_ref[...]) * (1.0 + sc) + sh).astype(BF16)
    for ref, (c0, c1) in ((qkv_ref, C_QKV), (ab_ref, C_AB), (kv_ref, C_KV),
                          (gvgg_ref, C_GVGG), (z_ref, C_Z), (q_ref, C_Q)):
        ref[0] = _dot(h, w_ref[:, c0:c1])


def _in_proj(x, mod, g1, w_in_p, nct):
    nb, s, d = x.shape
    nt = s // TM
    widths = [c1 - c0 for c0, c1 in (C_QKV, C_AB, C_KV, C_GVGG, C_Z, C_Q)]
    tok = lambda w: pl.BlockSpec((1, TM, w), lambda b, t: (b, t, 0))
    return pl.pallas_call(
        _inproj_kernel,
        grid=(nb, nt),
        in_specs=[tok(d),
                  pl.BlockSpec((1, 6, d), lambda b, t: (jnp.where(t < nct, nb, b), 0, 0)),
                  pl.BlockSpec((1, d), lambda b, t: (0, 0)),
                  pl.BlockSpec((d, IN_COLS), lambda b, t: (0, 0))],
        out_specs=[tok(w) for w in widths],
        out_shape=[jax.ShapeDtypeStruct((nb, s, w), F32) for w in widths],
        compiler_params=_cparams(("parallel", "parallel")),
        name="in_proj",
    )(x, mod, g1, w_in_p)


def _halo_specs(width, halo, s):
    per = TM // halo
    cur = pl.BlockSpec((1, TM, width), lambda b, t: (b, t, 0))
    prev = pl.BlockSpec((1, halo, width), lambda b, t: (b, jnp.maximum(t * per - 1, 0), 0))
    nxt = pl.BlockSpec((1, halo, width), lambda b, t: (b, jnp.minimum((t + 1) * per, s // halo - 1), 0))
    return [cur, prev, nxt]


def _halo_flags(nct, nt):
    t = pl.program_id(1)
    prev_ok = jnp.logical_and(t != 0, t != nct)
    next_ok = jnp.logical_and(t != nct - 1, t != nt - 1)
    return prev_ok, next_ok


def _conformer_kernel(cur_ref, prev_ref, next_ref, dww_ref, dwb_ref, lng_ref, lnb_ref, pww_ref, pwb_ref,
                      o_ref, ext_ref, *, nct, nt):
    prev_ok, next_ok = _halo_flags(nct, nt)

    def glu(v):
        return v[:, :CONV_CH] * jax.nn.sigmoid(v[:, CONV_CH:])

    ext_ref[0:CONV_HALO, :] = jnp.where(prev_ok, glu(prev_ref[0]), 0.0)
    ext_ref[CONV_HALO:CONV_HALO + TM, :] = glu(cur_ref[0])
    ext_ref[CONV_HALO + TM:2 * CONV_HALO + TM, :] = jnp.where(next_ok, glu(next_ref[0]), 0.0)
    rb = 64
    off = CONV_HALO - CONV_WIDTH // 2
    for r in range(TM // rb):
        acc = jnp.zeros((rb, CONV_CH), F32) + dwb_ref[...]
        for j in range(CONV_WIDTH):
            acc = acc + ext_ref[pl.ds(r * rb + off + j, rb), :] * dww_ref[j:j + 1, :]
        mu = jnp.mean(acc, axis=-1, keepdims=True)
        xc = acc - mu
        y = xc * lax.rsqrt(jnp.mean(xc * xc, axis=-1, keepdims=True) + EPS) * lng_ref[...] + lnb_ref[...]
        h = _silu(y).astype(BF16)
        o_ref[0, r * rb:(r + 1) * rb, :] = _dot(h, pww_ref[...]) + pwb_ref[...]


def _conformer(gvgg, dww, dwb, lng, lnb, pww, pwb, nct):
    nb, s, _ = gvgg.shape
    nt = s // TM
    row = lambda w: pl.BlockSpec((1, w), lambda b, t: (0, 0))
    return pl.pallas_call(
        functools.partial(_conformer_kernel, nct=nct, nt=nt),
        grid=(nb, nt),
        in_specs=_halo_specs(2 * CONV_CH, CONV_HALO, s) + [
            pl.BlockSpec((CONV_WIDTH + 1, CONV_CH), lambda b, t: (0, 0)),
            row(CONV_CH), row(CONV_CH), row(CONV_CH),
            pl.BlockSpec((CONV_CH, CONV_CH), lambda b, t: (0, 0)),
            row(CONV_CH)],
        out_specs=pl.BlockSpec((1, TM, CONV_CH), lambda b, t: (b, t, 0)),
        out_shape=jax.ShapeDtypeStruct((nb, s, CONV_CH), F32),
        scratch_shapes=[pltpu.VMEM((TM + 2 * CONV_HALO, CONV_CH), F32)],
        compiler_params=_cparams(("parallel", "parallel")),
        name="conformer",
    )(gvgg, gvgg, gvgg, dww, dwb, lng, lnb, pww, pwb)


def _gdn_feat_kernel(cur_ref, prev_ref, next_ref, cw_ref, ab_ref, alog_ref, dtb_ref,
                     bd_ref, q_ref, k_ref, v_ref, gb_ref, ext_ref, y_ref, *, nct, nt):
    prev_ok, next_ok = _halo_flags(nct, nt)
    ext_ref[0:SHORT_HALO, :] = jnp.where(prev_ok, prev_ref[0], 0.0)
    ext_ref[SHORT_HALO:SHORT_HALO + TM, :] = cur_ref[0]
    ext_ref[SHORT_HALO + TM:2 * SHORT_HALO + TM, :] = jnp.where(next_ok, next_ref[0], 0.0)
    rb = 32
    off = SHORT_HALO - SHORT_CONV // 2
    for r in range(TM // rb):
        acc = jnp.zeros((rb, GDN_QKV), F32)
        for j in range(SHORT_CONV):
            acc = acc + ext_ref[pl.ds(r * rb + off + j, rb), :] * cw_ref[j:j + 1, :]
        y_ref[r * rb:(r + 1) * rb, :] = _silu(acc)

    def l2n(x):
        return x * lax.rsqrt(_head_sums(x * x, bd_ref[...]) + EPS)

    qn = l2n(y_ref[:, 0:GDN_WIDTH]) * (GDN_DK ** -0.5)
    kn = l2n(y_ref[:, GDN_WIDTH:2 * GDN_WIDTH])
    v = y_ref[:, 2 * GDN_WIDTH:]
    for h in range(GDN_HEADS):
        q_ref[0, h] = qn[:, h * GDN_DK:(h + 1) * GDN_DK]
        k_ref[0, h] = kn[:, h * GDN_DK:(h + 1) * GDN_DK]
        v_ref[0, h] = v[:, h * GDN_DK:(h + 1) * GDN_DK]
    ab = ab_ref[0]
    lane = lax.broadcasted_iota(jnp.int32, ab.shape, 1)
    xa = ab + dtb_ref[...]
    softplus = jnp.maximum(xa, 0.0) + jnp.log(1.0 + jnp.exp(-jnp.abs(xa)))
    g = -jnp.exp(alog_ref[...]) * softplus
    beta = jax.nn.sigmoid(ab)
    gb_ref[0] = jnp.where(lane < 2 * GDN_HEADS, g, jnp.where(lane < 4 * GDN_HEADS, beta, 0.0))


def _gdn_features(qkv, ab, cw, alog, dtb, bd, nct):
    nb, s, _ = qkv.shape
    nt = s // TM
    head = pl.BlockSpec((1, GDN_HEADS, TM, GDN_DK), lambda b, t: (b, 0, t, 0))
    hshape = jax.ShapeDtypeStruct((nb, GDN_HEADS, s, GDN_DK), F32)
    return pl.pallas_call(
        functools.partial(_gdn_feat_kernel, nct=nct, nt=nt),
        grid=(nb, nt),
        in_specs=_halo_specs(GDN_QKV, SHORT_HALO, s) + [
            pl.BlockSpec((8, GDN_QKV), lambda b, t: (0, 0)),
            pl.BlockSpec((1, TM, LANES), lambda b, t: (b, t, 0)),
            pl.BlockSpec((1, LANES), lambda b, t: (0, 0)),
            pl.BlockSpec((1, LANES), lambda b, t: (0, 0)),
            pl.BlockSpec((GDN_WIDTH, GDN_WIDTH), lambda b, t: (0, 0))],
        out_specs=[head, head, head, pl.BlockSpec((1, TM, LANES), lambda b, t: (b, t, 0))],
        out_shape=[hshape, hshape, hshape, jax.ShapeDtypeStruct((nb, s, LANES), F32)],
        scratch_shapes=[pltpu.VMEM((TM + 2 * SHORT_HALO, GDN_QKV), F32), pltpu.VMEM((TM, GDN_QKV), F32)],
        compiler_params=_cparams(("parallel", "parallel")),
        name="gdn_features",
    )(qkv, qkv, qkv, cw, ab, alog, dtb, bd)


def _gdn_scan_kernel(qf_ref, kf_ref, vf_ref, gf_ref, qb_ref, kb_ref, vb_ref, gbk_ref,
                     of_ref, ob_ref, s_ref):
    @pl.when(pl.program_id(0) == 0)
    def _():
        s_ref[...] = jnp.zeros(s_ref.shape, F32)

    c = GDN_C
    row = lax.broadcasted_iota(jnp.int32, (c, c), 0)
    col = lax.broadcasted_iota(jnp.int32, (c, c), 1)
    eye = (row == col).astype(F32)
    blk = {2 ** e: (row >> e) == (col >> e) for e in range(1, int(math.log2(c)) + 1)}

    chains = []
    for b, d in ((b, d) for b in range(qf_ref.shape[0]) for d in range(2)):
        q_ref, k_ref, v_ref, g_ref, o_ref = ((qf_ref, kf_ref, vf_ref, gf_ref, of_ref),
                                             (qb_ref, kb_ref, vb_ref, gbk_ref, ob_ref))[d]
        incl = (row >= col) if d == 0 else (row <= col)
        strict = (row > col) if d == 0 else (row < col)
        gb = g_ref[b]
        gam_all = _dot(incl.astype(F32), gb, precision=HIGHEST)
        gam_t = gam_all.T
        last_row = c - 1 if d == 0 else 0
        for h in range(GDN_HEADS):
            cc = d * GDN_HEADS + h
            gam = gam_all[:, cc:cc + 1]
            beta = gb[:, 2 * GDN_HEADS + cc:2 * GDN_HEADS + cc + 1]
            kh = k_ref[b, h]
            chains.append(dict(
                b=b, d=d, h=h, o_ref=o_ref, strict=strict, gam=gam, beta=beta, kh=kh,
                qh=q_ref[b, h], vh=v_ref[b, h],
                last=gam_all[last_row:last_row + 1, cc:cc + 1],
                decay=jnp.exp(jnp.where(incl, gam - gam_t[cc:cc + 1, :], -1e30)),
                kbeta=kh * beta))
    for ch in chains:
        ch["a"] = jnp.where(ch["strict"], _dot_nt(ch["kbeta"], ch["kh"]) * ch["decay"], 0.0)
    for ch in chains:
        ch["t"] = eye - jnp.where(blk[2], ch["a"], 0.0)
    bsz = 2
    while bsz < c:
        off = jnp.logical_and(blk[2 * bsz], jnp.logical_not(blk[bsz]))
        for ch in chains:
            ch["tl"] = _dot(ch["t"], jnp.where(off, ch["a"], 0.0))
        for ch in chains:
            ch["t"] = ch["t"] - _dot(ch["tl"], ch["t"])
        bsz *= 2
    for ch in chains:
        rhs = jnp.concatenate([ch["vh"] * ch["beta"], ch["kbeta"] * jnp.exp(ch["gam"])], axis=1)
        ch["uw"] = _dot(ch["t"], rhs)
    for ch in chains:
        ch["pm"] = _dot_nt(ch["qh"], ch["kh"]) * ch["decay"]
    for ch in chains:
        ch["s"] = s_ref[ch["b"], ch["d"], ch["h"]]
        ch["v_new"] = ch["uw"][:, :GDN_DK] - _dot(ch["uw"][:, GDN_DK:], ch["s"])
    for ch in chains:
        h = ch["h"]
        qg = ch["qh"] * jnp.exp(ch["gam"])
        ch["o_ref"][ch["b"], :, h * GDN_DK:(h + 1) * GDN_DK] = _dot(qg, ch["s"]) + _dot(ch["pm"], ch["v_new"])
    for ch in chains:
        kd = ch["kh"] * jnp.exp(ch["last"] - ch["gam"])
        s_ref[ch["b"], ch["d"], ch["h"]] = ch["s"] * jnp.exp(ch["last"]) + _dot_tn(kd, ch["v_new"])


def _gdn_scan(q, k, v, gb, ncc):
    nb, _, s, _ = q.shape
    nc = s // GDN_C

    def bwd(n):
        return jnp.where(n < ncc, ncc - 1 - n, nc + ncc - 1 - n)

    hf = pl.BlockSpec((nb, GDN_HEADS, GDN_C, GDN_DK), lambda n: (0, 0, n, 0))
    hb = pl.BlockSpec((nb, GDN_HEADS, GDN_C, GDN_DK), lambda n: (0, 0, bwd(n), 0))
    gf = pl.BlockSpec((nb, GDN_C, LANES), lambda n: (0, n, 0))
    gk = pl.BlockSpec((nb, GDN_C, LANES), lambda n: (0, bwd(n), 0))
    of = pl.BlockSpec((nb, GDN_C, GDN_WIDTH), lambda n: (0, n, 0))
    ob = pl.BlockSpec((nb, GDN_C, GDN_WIDTH), lambda n: (0, bwd(n), 0))
    oshape = jax.ShapeDtypeStruct((nb, s, GDN_WIDTH), F32)
    return pl.pallas_call(
        _gdn_scan_kernel,
        grid=(nc,),
        in_specs=[hf, hf, hf, gf, hb, hb, hb, gk],
        out_specs=[of, ob],
        out_shape=[oshape, oshape],
        scratch_shapes=[pltpu.VMEM((nb, 2, GDN_HEADS, GDN_DK, GDN_DK), F32)],
        compiler_params=_cparams(("arbitrary",)),
        name="gdn_scan",
    )(q, k, v, gb, q, k, v, gb)


def _attn_prep_kernel(q_ref, kv_ref, cos_ref, sin_ref, qg_ref, kg_ref, bd_ref,
                      qt_ref, kh_ref, vt_ref, *, q_scale):
    cos = cos_ref[...]
    sin = sin_ref[...]
    bd = bd_ref[...]
    lane = lax.broadcasted_iota(jnp.int32, cos.shape, 1)
    first = (lane % 32) < 16

    def norm_rope(x, g):
        ss = _head_sums(x * x, bd)
        y = x * lax.rsqrt(ss * (1.0 / HEAD_DIM) + EPS) * g
        swapped = jnp.where(first, pltpu.roll(y, LANES - 16, 1), pltpu.roll(y, 16, 1))
        return y * cos + swapped * sin

    q = q_ref[0]
    for j in range(ATTN_WIDTH // LANES):
        yt = (norm_rope(q[:, j * LANES:(j + 1) * LANES], qg_ref[...]) * q_scale).T
        qt_ref[0, 2 * j] = yt[:HEAD_DIM].astype(BF16)
        qt_ref[0, 2 * j + 1] = yt[HEAD_DIM:].astype(BF16)
    kv = kv_ref[0]
    y = norm_rope(kv[:, :LANES], kg_ref[...])
    kh_ref[0, 0] = y[:, :HEAD_DIM].astype(BF16)
    kh_ref[0, 1] = y[:, HEAD_DIM:].astype(BF16)
    vt = kv[:, LANES:].T.astype(BF16)
    ones = jnp.ones((VT_ROWS - HEAD_DIM, ATT_TK), BF16)
    for g in range(ATTN_KV_HEADS):
        for cidx in range(TM // ATT_TK):
            vt_ref[0, g, cidx, 0:HEAD_DIM, :] = vt[g * HEAD_DIM:(g + 1) * HEAD_DIM,
                                                   cidx * ATT_TK:(cidx + 1) * ATT_TK]
            vt_ref[0, g, cidx, HEAD_DIM:VT_ROWS, :] = ones


def _attn_prep(q, kv, cos_t, sin_t, qg, kg, bd, q_scale):
    nb, s, _ = q.shape
    nt = s // TM
    tab = pl.BlockSpec((TM, LANES), lambda b, t: (t, 0))
    row = pl.BlockSpec((1, LANES), lambda b, t: (0, 0))
    per = TM // ATT_TK
    return pl.pallas_call(
        functools.partial(_attn_prep_kernel, q_scale=q_scale),
        grid=(nb, nt),
        in_specs=[pl.BlockSpec((1, TM, ATTN_WIDTH), lambda b, t: (b, t, 0)),
                  pl.BlockSpec((1, TM, 2 * LANES), lambda b, t: (b, t, 0)),
                  tab, tab, row, row,
                  pl.BlockSpec((LANES, LANES), lambda b, t: (0, 0))],
        out_specs=[pl.BlockSpec((1, ATTN_Q_HEADS, HEAD_DIM, TM), lambda b, t: (b, 0, 0, t)),
                   pl.BlockSpec((1, ATTN_KV_HEADS, TM, HEAD_DIM), lambda b, t: (b, 0, t, 0)),
                   pl.BlockSpec((1, ATTN_KV_HEADS, per, VT_ROWS, ATT_TK), lambda b, t: (b, 0, t, 0, 0))],
        out_shape=[jax.ShapeDtypeStruct((nb, ATTN_Q_HEADS, HEAD_DIM, s), BF16),
                   jax.ShapeDtypeStruct((nb, ATTN_KV_HEADS, s, HEAD_DIM), BF16),
                   jax.ShapeDtypeStruct((nb, ATTN_KV_HEADS, s // ATT_TK, VT_ROWS, ATT_TK), BF16)],
        compiler_params=_cparams(("parallel", "parallel")),
        name="attn_prep",
    )(q, kv, cos_t, sin_t, qg, kg, bd)


def _attn_kernel(qt_ref, k_ref, vt_ref, o_ref, m_ref, acc_ref, s_ref, mx_ref, *, nctq, lc, s_len):
    qi = pl.program_id(2)
    m_ref[...] = jnp.full(m_ref.shape, -jnp.inf, F32)
    acc_ref[...] = jnp.zeros(acc_ref.shape, F32)
    nk = jnp.where(qi < nctq, lc // ATT_TK, s_len // ATT_TK)

    def score(j, slot):
        ks = pl.multiple_of(j * ATT_TK, ATT_TK)
        k = k_ref[0, 0, pl.ds(ks, ATT_TK), :]
        for h in range(ATTN_GROUP):
            s = _dot(k, qt_ref[0, h])
            s_ref[slot, h] = s
            mx_ref[slot, h] = jnp.max(s, axis=0, keepdims=True)

    def step(j, slot):
        score(jnp.minimum(j + 1, nk - 1), 1 - slot)
        vt = vt_ref[0, 0, j]
        for h in range(ATTN_GROUP):
            s = s_ref[slot, h]
            m_old = m_ref[h]
            m_new = jnp.maximum(m_old, mx_ref[slot, h])
            alpha = jnp.exp2(m_old - m_new)
            p = jnp.exp2(s - m_new)
            acc_ref[h] = alpha * acc_ref[h] + _dot(vt, p.astype(BF16))
            m_ref[h] = m_new

    def quad(i, carry):
        for u in range(4):
            step(1 + 4 * i + u, (1 + u) % 2)
        return carry

    score(0, 0)
    step(0, 0)
    lax.fori_loop(0, (nk - 1) // 4, quad, 0)

    for hp in range(ATTN_GROUP // 2):
        pair = jnp.concatenate(
            [acc_ref[h, 0:HEAD_DIM, :] / acc_ref[h, HEAD_DIM:HEAD_DIM + 1, :] for h in (2 * hp, 2 * hp + 1)], axis=0)
        o_ref[0, :, hp * LANES:(hp + 1) * LANES] = pair.T


def _attention(qt, kh, vt, lc):
    nb, _, _, s = qt.shape
    nq = s // ATT_TQ
    return pl.pallas_call(
        functools.partial(_attn_kernel, nctq=lc // ATT_TQ, lc=lc, s_len=s),
        grid=(nb, ATTN_KV_HEADS, nq),
        in_specs=[pl.BlockSpec((1, ATTN_GROUP, HEAD_DIM, ATT_TQ), lambda b, g, i: (b, g, 0, i)),
                  pl.BlockSpec((1, 1, s, HEAD_DIM), lambda b, g, i: (b, g, 0, 0)),
                  pl.BlockSpec((1, 1, s // ATT_TK, VT_ROWS, ATT_TK), lambda b, g, i: (b, g, 0, 0, 0))],
        out_specs=pl.BlockSpec((1, ATT_TQ, ATTN_GROUP * HEAD_DIM), lambda b, g, i: (b, i, g)),
        out_shape=jax.ShapeDtypeStruct((nb, s, ATTN_WIDTH), F32),
        scratch_shapes=[pltpu.VMEM((ATTN_GROUP, 1, ATT_TQ), F32),
                        pltpu.VMEM((ATTN_GROUP, VT_ROWS, ATT_TQ), F32),
                        pltpu.VMEM((2, ATTN_GROUP, ATT_TK, ATT_TQ), F32),
                        pltpu.VMEM((2, ATTN_GROUP, 1, ATT_TQ), F32)],
        compiler_params=_cparams(("parallel", "parallel", "parallel")),
        name="attention",
    )(qt, kh, vt)


def _outproj_kernel(conv_ref, of_ref, ob_ref, z_ref, attn_ref, x_ref, mod_ref, gg_ref, bd_ref, w_ref,
                    n2_ref, wr_ref, br_ref, xo_ref, h2_ref, ri_ref, rw_ref):
    o = of_ref[0] + ob_ref[0]
    ss = _head_sums(o * o, bd_ref[...])
    gdn = o * lax.rsqrt(ss * (1.0 / GDN_DK) + EPS) * gg_ref[...] * _silu(z_ref[0])
    mix = jnp.concatenate([conv_ref[0].astype(BF16), gdn.astype(BF16), attn_ref[0].astype(BF16)], axis=1)
    y = _dot(mix, w_ref[...])
    x = x_ref[0] + mod_ref[0, 2:3, :] * y
    xo_ref[0] = x
    h2 = _rms(x, n2_ref[...]) * (1.0 + mod_ref[0, 4:5, :]) + mod_ref[0, 3:4, :]
    h2_ref[0] = h2
    h_hi = h2.astype(BF16)
    h_lo = (h2 - h_hi.astype(F32)).astype(BF16)
    t = _dot(h_hi, wr_ref[...])
    lg = t[:, :LANES] + t[:, LANES:] + _dot(h_lo, wr_ref[:, 0:LANES]) + br_ref[...]
    lane = lax.broadcasted_iota(jnp.int32, lg.shape, 1)
    lane_f = lane.astype(F32)

    def first_max(v):
        mx = jnp.max(v, axis=-1, keepdims=True)
        return mx, jnp.min(jnp.where(v == mx, lane_f, float(LANES)), axis=-1, keepdims=True)

    lgg = jnp.where(lane < N_GROUPS, lg, -jnp.inf)
    gmax, grp = first_max(lgg)
    pg_sel = 1.0 / jnp.sum(jnp.exp(lgg - gmax), axis=-1, keepdims=True)
    in_grp = jnp.logical_and(lane >= N_GROUPS, ((lane - N_GROUPS) >> 3).astype(F32) == grp)
    le = jnp.where(in_grp, lg, -jnp.inf)
    e1max, i1 = first_max(le)
    e2max, i2 = first_max(jnp.where(lane_f == i1, -jnp.inf, le))
    r = jnp.exp(e2max - e1max)
    w1 = pg_sel / (1.0 + r)
    ri_ref[0] = jnp.where(lane == 0, i1, jnp.where(lane == 1, i2, float(N_GROUPS))).astype(jnp.int32) - N_GROUPS
    rw_ref[0] = jnp.where(lane == 0, w1, jnp.where(lane == 1, w1 * r, 0.0))


def _out_proj(conv, o_f, o_b, z, attn, x, mod, gdn_g, bd, w_out, n2, wr, br, nct):
    nb, s, d = x.shape
    nt = s // TM
    tok = lambda w: pl.BlockSpec((1, TM, w), lambda b, t: (b, t, 0))
    full = lambda a: pl.BlockSpec(a.shape, lambda b, t: (0,) * a.ndim)
    return pl.pallas_call(
        _outproj_kernel,
        grid=(nb, nt),
        in_specs=[tok(CONV_CH), tok(GDN_WIDTH), tok(GDN_WIDTH), tok(GDN_WIDTH), tok(ATTN_WIDTH), tok(d),
                  pl.BlockSpec((1, 6, d), lambda b, t: (jnp.where(t < nct, nb, b), 0, 0)),
                  full(gdn_g), full(bd), full(w_out), full(n2), full(wr), full(br)],
        out_specs=[tok(d), tok(d), tok(LANES), tok(LANES)],
        out_shape=[jax.ShapeDtypeStruct((nb, s, d), F32), jax.ShapeDtypeStruct((nb, s, d), F32),
                   jax.ShapeDtypeStruct((nb, s, LANES), jnp.int32), jax.ShapeDtypeStruct((nb, s, LANES), F32)],
        compiler_params=_cparams(("parallel", "parallel")),
        name="out_proj",
    )(conv, o_f, o_b, z, attn, x, mod, gdn_g, bd, w_out, n2, wr, br)


def _dispatch_kernel(slot_ref, h_ref, xe_in_ref, xe_ref, sem):
    del xe_in_ref
    base = pl.program_id(0) * (TM * TOP_K)

    def row_copy(r, k):
        dst = slot_ref[base + TOP_K * r + k]
        return pltpu.make_async_copy(h_ref.at[pl.ds(r, 1)], xe_ref.at[pl.ds(dst, 1)], sem)

    def issue(r, carry):
        for k in range(TOP_K):
            row_copy(r, k).start()
        return carry

    lax.fori_loop(0, TM, issue, 0, unroll=8)
    for _ in range(TOP_K):
        pltpu.make_async_copy(h_ref, xe_ref.at[pl.ds(0, TM)], sem).wait()


def _dispatch(h2, slots, n_slots):
    n_tok, d = h2.shape
    grid_spec = pltpu.PrefetchScalarGridSpec(
        num_scalar_prefetch=1,
        grid=(n_tok // TM,),
        in_specs=[pl.BlockSpec((TM, d), lambda i, sl: (i, 0)),
                  pl.BlockSpec(memory_space=pl.ANY)],
        out_specs=pl.BlockSpec(memory_space=pl.ANY),
        scratch_shapes=[pltpu.SemaphoreType.DMA(())],
    )
    return pl.pallas_call(
        _dispatch_kernel,
        grid_spec=grid_spec,
        out_shape=jax.ShapeDtypeStruct((n_slots, d), F32),
        input_output_aliases={2: 0},
        compiler_params=_cparams(("arbitrary",)),
        name="moe_dispatch",
    )(slots, h2, jnp.zeros((n_slots, d), F32))


def _expert_kernel(be_ref, nu_ref, x_ref, wg_ref, wu_ref, wd_ref, o_ref, wg_s, wu_s, wd_s):
    i = pl.program_id(0)
    used = i < nu_ref[0]
    new_expert = jnp.logical_or(i == 0, be_ref[i] != be_ref[jnp.maximum(i - 1, 0)])

    @pl.when(jnp.logical_and(used, new_expert))
    def _():
        wg_s[...] = wg_ref[0, 0].astype(BF16)
        wu_s[...] = wu_ref[0, 0].astype(BF16)
        wd_s[...] = wd_ref[0, 0].astype(BF16)

    @pl.when(used)
    def _():
        x = x_ref[...].astype(BF16)
        g = _dot(x, wg_s[...])
        u = _dot(x, wu_s[...])
        h = (_silu(g) * u).astype(BF16)
        o_ref[...] = _dot(h, wd_s[...])

    @pl.when(i >= nu_ref[0])
    def _():
        o_ref[...] = jnp.zeros(o_ref.shape, F32)


def _experts(xs, blk_e, n_used, w_gate, w_up, w_down, layer):
    n_slots, d = xs.shape
    nblk = n_slots // MOE_BM
    de = w_gate.shape[-1]
    grid_spec = pltpu.PrefetchScalarGridSpec(
        num_scalar_prefetch=2,
        grid=(nblk,),
        in_specs=[pl.BlockSpec((MOE_BM, d), lambda i, be, nu: (jnp.minimum(i, nu[0] - 1), 0)),
                  pl.BlockSpec((1, 1, d, de), lambda i, be, nu: (layer, be[i], 0, 0)),
                  pl.BlockSpec((1, 1, d, de), lambda i, be, nu: (layer, be[i], 0, 0)),
                  pl.BlockSpec((1, 1, de, d), lambda i, be, nu: (layer, be[i], 0, 0))],
        out_specs=pl.BlockSpec((MOE_BM, d), lambda i, be, nu: (i, 0)),
        scratch_shapes=[pltpu.VMEM((d, de), BF16), pltpu.VMEM((d, de), BF16), pltpu.VMEM((de, d), BF16)],
    )
    return pl.pallas_call(
        _expert_kernel,
        grid_spec=grid_spec,
        out_shape=jax.ShapeDtypeStruct((n_slots, d), F32),
        compiler_params=_cparams(("arbitrary",)),
        name="experts",
    )(blk_e, n_used, xs, w_gate, w_up, w_down)


def _combine_kernel(slot_ref, x_ref, rw_ref, mod_ref, fg_ref, ye_ref, o_ref, buf_ref, sem, *, final, nt, ntile):
    g = pl.program_id(0) * nt + pl.program_id(1)

    def row_copy(tile, r, k, par):
        src = slot_ref[tile * (TM * TOP_K) + TOP_K * r + k]
        return pltpu.make_async_copy(ye_ref.at[pl.ds(src, 1)], buf_ref.at[par, k, pl.ds(r, 1)], sem.at[par, k])

    def issue_tile(tile, par):
        def issue(r, carry):
            for k in range(TOP_K):
                row_copy(tile, r, k, par).start()
            return carry
        lax.fori_loop(0, TM, issue, 0, unroll=8)

    def finish(par):
        for k in range(TOP_K):
            pltpu.make_async_copy(ye_ref.at[pl.ds(0, TM)], buf_ref.at[par, k], sem.at[par, k]).wait()
        rw = rw_ref[0]
        y = rw[:, 0:1] * buf_ref[par, 0] + rw[:, 1:2] * buf_ref[par, 1]
        x = x_ref[0] + mod_ref[0, 5:6, :] * y
        if final:
            x = _rms(x, fg_ref[...])
        o_ref[0] = x

    @pl.when(g == 0)
    def _():
        issue_tile(0, 0)

    for par in range(2):
        @pl.when(g % 2 == par)
        def _(par=par):
            @pl.when(g + 1 < ntile)
            def _():
                issue_tile(g + 1, 1 - par)

            finish(par)


def _combine(x, ye, slots, rw, mod, fg, nct, final):
    nb, s, d = x.shape
    nt = s // TM
    tok = lambda w: pl.BlockSpec((1, TM, w), lambda b, t, sl: (b, t, 0))
    grid_spec = pltpu.PrefetchScalarGridSpec(
        num_scalar_prefetch=1,
        grid=(nb, nt),
        in_specs=[tok(d), tok(LANES),
                  pl.BlockSpec((1, 6, d), lambda b, t, sl: (jnp.where(t < nct, nb, b), 0, 0)),
                  pl.BlockSpec((1, d), lambda b, t, sl: (0, 0)),
                  pl.BlockSpec(memory_space=pl.ANY)],
        out_specs=tok(d),
        scratch_shapes=[pltpu.VMEM((2, TOP_K, TM, d), F32), pltpu.SemaphoreType.DMA((2, TOP_K))],
    )
    return pl.pallas_call(
        functools.partial(_combine_kernel, final=final, nt=nt, ntile=nb * nt),
        grid_spec=grid_spec,
        out_shape=jax.ShapeDtypeStruct((nb, s, d), F32),
        compiler_params=_cparams(("arbitrary", "arbitrary")),
        name="moe_combine",
    )(slots, x, rw, mod, fg, ye)


def _rank_kernel(ri_ref, rank_ref, cnt_ref, carry_ref):
    @pl.when(pl.program_id(0) == 0)
    def _():
        carry_ref[...] = jnp.zeros(carry_ref.shape, F32)

    ri = ri_ref[0]
    lane = lax.broadcasted_iota(jnp.int32, ri.shape, 1)
    oh1 = lane == ri[:, 0:1]
    oh2 = lane == ri[:, 1:2]
    oh = jnp.logical_or(oh1, oh2).astype(F32)
    row = lax.broadcasted_iota(jnp.int32, (TM, TM), 0)
    col = lax.broadcasted_iota(jnp.int32, (TM, TM), 1)
    before = carry_ref[0:1, :] + _dot((row > col).astype(BF16), oh.astype(BF16))
    r1 = jnp.sum(jnp.where(oh1, before, 0.0), axis=-1, keepdims=True)
    r2 = jnp.sum(jnp.where(oh2, before, 0.0), axis=-1, keepdims=True)
    rank_ref[0] = jnp.where(lane == 0, r1, jnp.where(lane == 1, r2, 0.0)).astype(jnp.int32)
    carry_ref[...] = carry_ref[...] + jnp.sum(oh, axis=0, keepdims=True)
    cnt_ref[...] = carry_ref[...]


def _slot_kernel(ri_ref, rank_ref, ps_ref, slot_ref):
    ri = ri_ref[0]
    rk = rank_ref[0]
    lane = lax.broadcasted_iota(jnp.int32, ri.shape, 1)
    ps = ps_ref[...]
    s1 = jnp.sum(jnp.where(lane == ri[:, 0:1], ps, 0.0), axis=-1, keepdims=True).astype(jnp.int32) + rk[:, 0:1]
    s2 = jnp.sum(jnp.where(lane == ri[:, 1:2], ps, 0.0), axis=-1, keepdims=True).astype(jnp.int32) + rk[:, 1:2]
    slot_ref[0] = jnp.where(lane == 0, s1, jnp.where(lane == 1, s2, 0))


def _route(ri, n_tok):
    ntile = n_tok // TM
    ri = ri.reshape(ntile, TM, LANES)
    tile = pl.BlockSpec((1, TM, LANES), lambda i: (i, 0, 0))
    rank, cnt = pl.pallas_call(
        _rank_kernel,
        grid=(ntile,),
        in_specs=[tile],
        out_specs=[tile, pl.BlockSpec((8, LANES), lambda i: (0, 0))],
        out_shape=[jax.ShapeDtypeStruct((ntile, TM, LANES), jnp.int32), jax.ShapeDtypeStruct((8, LANES), F32)],
        scratch_shapes=[pltpu.VMEM((8, LANES), F32)],
        compiler_params=_cparams(("arbitrary",)),
        name="moe_rank",
    )(ri)
    counts = cnt[0, :N_EXPERTS].astype(jnp.int32)
    pcounts = (counts + MOE_BM - 1) // MOE_BM * MOE_BM
    pends = jnp.cumsum(pcounts)
    pstarts = jnp.zeros((1, LANES), F32).at[0, :N_EXPERTS].set((pends - pcounts).astype(F32))
    slot = pl.pallas_call(
        _slot_kernel,
        grid=(ntile,),
        in_specs=[tile, tile, pl.BlockSpec((1, LANES), lambda i: (0, 0))],
        out_specs=tile,
        out_shape=jax.ShapeDtypeStruct((ntile, TM, LANES), jnp.int32),
        compiler_params=_cparams(("parallel",)),
        name="moe_slot",
    )(ri, rank, pstarts)
    slots = slot.reshape(n_tok, LANES)[:, :TOP_K].reshape(-1)
    n_slots = (-(-n_tok * TOP_K // MOE_BM) + N_EXPERTS) * MOE_BM
    blk_start = jnp.arange(n_slots // MOE_BM, dtype=jnp.int32) * MOE_BM
    blk_e = jnp.minimum(jnp.sum((pends[None, :] <= blk_start[:, None]).astype(jnp.int32), axis=1), N_EXPERTS - 1)
    n_used = (pends[-1] // MOE_BM).astype(jnp.int32).reshape(1)
    return slots, n_slots, blk_e, n_used


def _rope_tables(lc, l):
    rows = l // GRID_W
    row = jnp.repeat(jnp.arange(rows), GRID_W).astype(F32)
    col = (jnp.arange(rows * GRID_W) % GRID_W).astype(F32)
    n_freq = HEAD_DIM // 4
    inv = ROPE_THETA ** (-jnp.arange(n_freq, dtype=F32) / n_freq)
    ar, ac = row[:, None] * inv, col[:, None] * inv
    cos = jnp.concatenate([jnp.cos(ar), jnp.cos(ar), jnp.cos(ac), jnp.cos(ac)], axis=-1)
    sin = jnp.concatenate([-jnp.sin(ar), jnp.sin(ar), -jnp.sin(ac), jnp.sin(ac)], axis=-1)
    cos = jnp.concatenate([jnp.ones((lc, HEAD_DIM), F32), cos], axis=0)
    sin = jnp.concatenate([jnp.zeros((lc, HEAD_DIM), F32), sin], axis=0)
    return jnp.tile(cos, (1, 2)), jnp.tile(sin, (1, 2))


def _block_diag_ones(n, blk):
    i = jnp.arange(n)
    return (i[:, None] // blk == i[None, :] // blk).astype(BF16)


def kernel(x, c, ctx, c_ctx, mod_w, mod_b, norm1_g, norm2_g, w_in, conv_dw_w, conv_dw_b, conv_ln_g, conv_ln_b, conv_pw_w, conv_pw_b, gdn_conv_w, gdn_a_log, gdn_dt_bias, gdn_norm_g, attn_q_norm_g, attn_k_norm_g, w_out, router_group_w, router_group_b, router_expert_w, router_expert_b, expert_w_gate, expert_w_up, expert_w_down, final_norm_g):
    nb, l, d = x.shape
    lc = ctx.shape[1]
    depth = mod_w.shape[0]
    s = lc + l
    assert lc % TM == 0 and l % TM == 0 and lc % GDN_C == 0 and l % GDN_C == 0
    assert lc % ATT_TK == 0 and s % ATT_TK == 0 and lc % ATT_TQ == 0
    assert (lc // ATT_TK - 1) % 4 == 0 and (s // ATT_TK - 1) % 4 == 0
    nct = lc // TM
    n_tok = nb * s

    nr = -(-(nb + 1) // 8) * 8
    cvec = jnp.zeros((nr, d), F32).at[:nb].set(c).at[nb].set(c_ctx)
    mod_all = _modulation(cvec, mod_w, mod_b).reshape(depth, nr, 6, d)

    cos_t, sin_t = _rope_tables(lc, l)
    bd64 = _block_diag_ones(LANES, HEAD_DIM)
    bd_gdn = _block_diag_ones(GDN_WIDTH, GDN_DK)
    q_scale = (HEAD_DIM ** -0.5) * math.log2(math.e)
    row = lambda v: v.reshape(1, -1).astype(F32)
    pad_lanes = lambda v: jnp.zeros((1, LANES), F32).at[0, :v.size].set(v.reshape(-1))

    xs = jnp.concatenate([ctx, x], axis=1)
    for layer in range(depth):
        mod = mod_all[layer]
        wi = w_in[layer]
        w_in_p = jnp.concatenate(
            [wi[:, :IN_A0 + 16], jnp.zeros((d, IN_PAD), F32), wi[:, IN_A0 + 16:]], axis=1).astype(BF16)
        qkv, ab, kv, gvgg, z, q = _in_proj(xs, mod, row(norm1_g[layer]), w_in_p, nct)

        dww = jnp.concatenate([conv_dw_w[layer], jnp.zeros((1, CONV_CH), F32)], axis=0)
        conv = _conformer(gvgg, dww, row(conv_dw_b[layer]), row(conv_ln_g[layer]), row(conv_ln_b[layer]),
                          conv_pw_w[layer].astype(BF16), row(conv_pw_b[layer]), nct)

        cw = jnp.concatenate([gdn_conv_w[layer], jnp.zeros((8 - SHORT_CONV, GDN_QKV), F32)], axis=0)
        gq, gk, gv, gb = _gdn_features(qkv, ab, cw, pad_lanes(gdn_a_log[layer]), pad_lanes(gdn_dt_bias[layer]),
                                       bd_gdn, nct)
        o_f, o_b = _gdn_scan(gq, gk, gv, gb, lc // GDN_C)

        qt, kh, vt = _attn_prep(q, kv, cos_t, sin_t,
                                row(jnp.tile(attn_q_norm_g[layer], 2)), row(jnp.tile(attn_k_norm_g[layer], 2)),
                                bd64, q_scale)
        attn = _attention(qt, kh, vt, lc)

        wr = jnp.zeros((d, LANES), F32).at[:, :N_GROUPS].set(router_group_w[layer])
        wr = wr.at[:, N_GROUPS:N_GROUPS + N_EXPERTS].set(router_expert_w[layer])
        wr_hi = wr.astype(BF16)
        wr = jnp.concatenate([wr_hi, (wr - wr_hi.astype(F32)).astype(BF16)], axis=1)
        br =jnp.zeros((1, LANES), F32).at[0, :N_GROUPS].set(router_group_b[layer])
        br = br.at[0, N_GROUPS:N_GROUPS + N_EXPERTS].set(router_expert_b[layer])
        xs, h2, ri, rw = _out_proj(conv, o_f, o_b, z, attn, xs, mod,
                                   row(jnp.tile(gdn_norm_g[layer], GDN_HEADS)), bd_gdn,
                                   w_out[layer].astype(BF16), row(norm2_g[layer]), wr, br, nct)

        slots, n_slots, blk_e, n_used = _route(ri, n_tok)
        xe = _dispatch(h2.reshape(n_tok, d), slots, n_slots)
        ye = _experts(xe, blk_e, n_used, expert_w_gate, expert_w_up, expert_w_down, layer)
        xs = _combine(xs, ye, slots, rw, mod, row(final_norm_g), nct, layer == depth - 1)
    return xs[:, lc:, :]
```

```python
import functools
import math

import jax
import jax.numpy as jnp
from jax import lax
from jax.experimental import pallas as pl
from jax.experimental.pallas import tpu as pltpu

F32 = jnp.float32
BF16 = jnp.bfloat16
HIGHEST = lax.Precision.HIGHEST

EPS = 1e-6
GRID_W = 64
CONV_CH = 256
CONV_WIDTH = 31
GDN_HEADS = 4
GDN_DK = 64
GDN_WIDTH = 256
GDN_QKV = 768
SHORT_CONV = 5
HEAD_DIM = 64
ATTN_Q_HEADS = 8
ATTN_KV_HEADS = 2
ATTN_GROUP = 4
ATTN_WIDTH = 512
ROPE_THETA = 10000.0
N_GROUPS = 4
EXPERTS_PER_GROUP = 8
N_EXPERTS = 32
TOP_K = 2
D_EXPERT = 512

LANES = 128
TM = 256
GDN_C = 128
CONV_HALO = 16
SHORT_HALO = 8
ATT_TQ = 256
ATT_TK = 256
VT_ROWS = 80
MOE_BM = 256
VMEM_LIMIT = 56 * 1024 * 1024

IN_A0 = GDN_QKV
IN_PAD = LANES - 16
C_QKV = (0, 768)
C_AB = (768, 896)
C_KV = (896, 1152)
C_GVGG = (1152, 1664)
C_Z = (1664, 1920)
C_Q = (1920, 2432)
IN_COLS = 2432


def _cparams(sem):
    return pltpu.CompilerParams(dimension_semantics=sem, vmem_limit_bytes=VMEM_LIMIT)


def _silu(x):
    return x * jax.nn.sigmoid(x)


def _dot(a, b, **kw):
    return jnp.dot(a, b, preferred_element_type=F32, **kw)


def _dot_nt(a, b):
    return lax.dot_general(a, b, (((1,), (1,)), ((), ())), preferred_element_type=F32)


def _dot_tn(a, b):
    return lax.dot_general(a, b, (((0,), (0,)), ((), ())), preferred_element_type=F32)


def _mod_kernel(c_ref, w_ref, b_ref, o_ref):
    c = c_ref[...]
    o_ref[0] = _dot(_silu(c), w_ref[0], precision=HIGHEST) + b_ref[0]


def _modulation(cvec, mod_w, mod_b):
    depth, d, n = mod_w.shape
    nr = cvec.shape[0]
    tn = 768
    return pl.pallas_call(
        _mod_kernel,
        grid=(depth, n // tn),
        in_specs=[pl.BlockSpec((nr, d), lambda l, j: (0, 0)),
                  pl.BlockSpec((1, d, tn), lambda l, j: (l, 0, j)),
                  pl.BlockSpec((1, 1, tn), lambda l, j: (l, 0, j))],
        out_specs=pl.BlockSpec((1, nr, tn), lambda l, j: (l, 0, j)),
        out_shape=jax.ShapeDtypeStruct((depth, nr, n), F32),
        compiler_params=_cparams(("parallel", "parallel")),
        name="modulation",
    )(cvec, mod_w, mod_b.reshape(depth, 1, n))


def _rms(x, g):
    return x * lax.rsqrt(jnp.mean(x * x, axis=-1, keepdims=True) + EPS) * g


def _head_sums(sq, bd):
    hi = sq.astype(BF16)
    lo = (sq - hi.astype(F32)).astype(BF16)
    return _dot(hi, bd) + _dot(lo, bd)


def _inproj_kernel(x_ref, mod_ref, g_ref, w_ref, qkv_ref, ab_ref, kv_ref, gvgg_ref, z_ref, q_ref):
    x = x_ref[0]
    sh = mod_ref[0, 0:1, :]
    sc = mod_ref[0, 1:2, :]
    h = (_rms(x, g_ref[...]) * (1.0 + sc) + sh).astype(BF16)
    for ref, (c0, c1) in ((qkv_ref, C_QKV), (ab_ref, C_AB), (kv_ref, C_KV),
                          (gvgg_ref, C_GVGG), (z_ref, C_Z), (q_ref, C_Q)):
        ref[0] = _dot(h, w_ref[:, c0:c1])


def _in_proj(x, mod, g1, w_in_p, nct):
    nb, s, d = x.shape
    nt = s // TM
    widths = [c1 - c0 for c0, c1 in (C_QKV, C_AB, C_KV, C_GVGG, C_Z, C_Q)]
    tok = lambda w: pl.BlockSpec((1, TM, w), lambda b, t: (b, t, 0))
    return pl.pallas_call(
        _inproj_kernel,
        grid=(nb, nt),
        in_specs=[tok(d),
                  pl.BlockSpec((1, 6, d), lambda b, t: (jnp.where(t < nct, nb, b), 0, 0)),
                  pl.BlockSpec((1, d), lambda b, t: (0, 0)),
                  pl.BlockSpec((d, IN_COLS), lambda b, t: (0, 0))],
        out_specs=[tok(w) for w in widths],
        out_shape=[jax.ShapeDtypeStruct((nb, s, w), F32) for w in widths],
        compiler_params=_cparams(("parallel", "parallel")),
        name="in_proj",
    )(x, mod, g1, w_in_p)


def _halo_specs(width, halo, s):
    per = TM // halo
    cur = pl.BlockSpec((1, TM, width), lambda b, t: (b, t, 0))
    prev = pl.BlockSpec((1, halo, width), lambda b, t: (b, jnp.maximum(t * per - 1, 0), 0))
    nxt = pl.BlockSpec((1, halo, width), lambda b, t: (b, jnp.minimum((t + 1) * per, s // halo - 1), 0))
    return [cur, prev, nxt]


def _halo_flags(nct, nt):
    t = pl.program_id(1)
    prev_ok = jnp.logical_and(t != 0, t != nct)
    next_ok = jnp.logical_and(t != nct - 1, t != nt - 1)
    return prev_ok, next_ok


def _conformer_kernel(cur_ref, prev_ref, next_ref, dww_ref, dwb_ref, lng_ref, lnb_ref, pww_ref, pwb_ref,
                      o_ref, ext_ref, *, nct, nt):
    prev_ok, next_ok = _halo_flags(nct, nt)

    def glu(v):
        return v[:, :CONV_CH] * jax.nn.sigmoid(v[:, CONV_CH:])

    ext_ref[0:CONV_HALO, :] = jnp.where(prev_ok, glu(prev_ref[0]), 0.0)
    ext_ref[CONV_HALO:CONV_HALO + TM, :] = glu(cur_ref[0])
    ext_ref[CONV_HALO + TM:2 * CONV_HALO + TM, :] = jnp.where(next_ok, glu(next_ref[0]), 0.0)
    rb = 64
    off = CONV_HALO - CONV_WIDTH // 2
    for r in range(TM // rb):
        acc = jnp.zeros((rb, CONV_CH), F32) + dwb_ref[...]
        for j in range(CONV_WIDTH):
            acc = acc + ext_ref[pl.ds(r * rb + off + j, rb), :] * dww_ref[j:j + 1, :]
        mu = jnp.mean(acc, axis=-1, keepdims=True)
        xc = acc - mu
        y = xc * lax.rsqrt(jnp.mean(xc * xc, axis=-1, keepdims=True) + EPS) * lng_ref[...] + lnb_ref[...]
        h = _silu(y).astype(BF16)
        o_ref[0, r * rb:(r + 1) * rb, :] = _dot(h, pww_ref[...]) + pwb_ref[...]


def _conformer(gvgg, dww, dwb, lng, lnb, pww, pwb, nct):
    nb, s, _ = gvgg.shape
    nt = s // TM
    row = lambda w: pl.BlockSpec((1, w), lambda b, t: (0, 0))
    return pl.pallas_call(
        functools.partial(_conformer_kernel, nct=nct, nt=nt),
        grid=(nb, nt),
        in_specs=_halo_specs(2 * CONV_CH, CONV_HALO, s) + [
            pl.BlockSpec((CONV_WIDTH + 1, CONV_CH), lambda b, t: (0, 0)),
            row(CONV_CH), row(CONV_CH), row(CONV_CH),
            pl.BlockSpec((CONV_CH, CONV_CH), lambda b, t: (0, 0)),
            row(CONV_CH)],
        out_specs=pl.BlockSpec((1, TM, CONV_CH), lambda b, t: (b, t, 0)),
        out_shape=jax.ShapeDtypeStruct((nb, s, CONV_CH), F32),
        scratch_shapes=[pltpu.VMEM((TM + 2 * CONV_HALO, CONV_CH), F32)],
        compiler_params=_cparams(("parallel", "parallel")),
        name="conformer",
    )(gvgg, gvgg, gvgg, dww, dwb, lng, lnb, pww, pwb)


def _gdn_feat_kernel(cur_ref, prev_ref, next_ref, cw_ref, ab_ref, alog_ref, dtb_ref,
                     bd_ref, q_ref, k_ref, v_ref, gb_ref, ext_ref, y_ref, *, nct, nt):
    prev_ok, next_ok = _halo_flags(nct, nt)
    ext_ref[0:SHORT_HALO, :] = jnp.where(prev_ok, prev_ref[0], 0.0)
    ext_ref[SHORT_HALO:SHORT_HALO + TM, :] = cur_ref[0]
    ext_ref[SHORT_HALO + TM:2 * SHORT_HALO + TM, :] = jnp.where(next_ok, next_ref[0], 0.0)
    rb = 32
    off = SHORT_HALO - SHORT_CONV // 2
    for r in range(TM // rb):
        acc = jnp.zeros((rb, GDN_QKV), F32)
        for j in range(SHORT_CONV):
            acc = acc + ext_ref[pl.ds(r * rb + off + j, rb), :] * cw_ref[j:j + 1, :]
        y_ref[r * rb:(r + 1) * rb, :] = _silu(acc)

    def l2n(x):
        return x * lax.rsqrt(_head_sums(x * x, bd_ref[...]) + EPS)

    qn = l2n(y_ref[:, 0:GDN_WIDTH]) * (GDN_DK ** -0.5)
    kn = l2n(y_ref[:, GDN_WIDTH:2 * GDN_WIDTH])
    v = y_ref[:, 2 * GDN_WIDTH:]
    for h in range(GDN_HEADS):
        q_ref[0, h] = qn[:, h * GDN_DK:(h + 1) * GDN_DK]
        k_ref[0, h] = kn[:, h * GDN_DK:(h + 1) * GDN_DK]
        v_ref[0, h] = v[:, h * GDN_DK:(h + 1) * GDN_DK]
    ab = ab_ref[0]
    lane = lax.broadcasted_iota(jnp.int32, ab.shape, 1)
    xa = ab + dtb_ref[...]
    softplus = jnp.maximum(xa, 0.0) + jnp.log(1.0 + jnp.exp(-jnp.abs(xa)))
    g = -jnp.exp(alog_ref[...]) * softplus
    beta = jax.nn.sigmoid(ab)
    gb_ref[0] = jnp.where(lane < 2 * GDN_HEADS, g, jnp.where(lane < 4 * GDN_HEADS, beta, 0.0))


def _gdn_features(qkv, ab, cw, alog, dtb, bd, nct):
    nb, s, _ = qkv.shape
    nt = s // TM
    head = pl.BlockSpec((1, GDN_HEADS, TM, GDN_DK), lambda b, t: (b, 0, t, 0))
    hshape = jax.ShapeDtypeStruct((nb, GDN_HEADS, s, GDN_DK), F32)
    return pl.pallas_call(
        functools.partial(_gdn_feat_kernel, nct=nct, nt=nt),
        grid=(nb, nt),
        in_specs=_halo_specs(GDN_QKV, SHORT_HALO, s) + [
            pl.BlockSpec((8, GDN_QKV), lambda b, t: (0, 0)),
            pl.BlockSpec((1, TM, LANES), lambda b, t: (b, t, 0)),
            pl.BlockSpec((1, LANES), lambda b, t: (0, 0)),
            pl.BlockSpec((1, LANES), lambda b, t: (0, 0)),
            pl.BlockSpec((GDN_WIDTH, GDN_WIDTH), lambda b, t: (0, 0))],
        out_specs=[head, head, head, pl.BlockSpec((1, TM, LANES), lambda b, t: (b, t, 0))],
        out_shape=[hshape, hshape, hshape, jax.ShapeDtypeStruct((nb, s, LANES), F32)],
        scratch_shapes=[pltpu.VMEM((TM + 2 * SHORT_HALO, GDN_QKV), F32), pltpu.VMEM((TM, GDN_QKV), F32)],
        compiler_params=_cparams(("parallel", "parallel")),
        name="gdn_features",
    )(qkv, qkv, qkv, cw, ab, alog, dtb, bd)


def _gdn_scan_kernel(qf_ref, kf_ref, vf_ref, gf_ref, qb_ref, kb_ref, vb_ref, gbk_ref,
                     of_ref, ob_ref, s_ref):
    @pl.when(pl.program_id(0) == 0)
    def _():
        s_ref[...] = jnp.zeros(s_ref.shape, F32)

    c = GDN_C
    row = lax.broadcasted_iota(jnp.int32, (c, c), 0)
    col = lax.broadcasted_iota(jnp.int32, (c, c), 1)
    eye = (row == col).astype(F32)
    blk = {2 ** e: (row >> e) == (col >> e) for e in range(1, int(math.log2(c)) + 1)}

    chains = []
    for b, d in ((b, d) for b in range(qf_ref.shape[0]) for d in range(2)):
        q_ref, k_ref, v_ref, g_ref, o_ref = ((qf_ref, kf_ref, vf_ref, gf_ref, of_ref),
                                             (qb_ref, kb_ref, vb_ref, gbk_ref, ob_ref))[d]
        incl = (row >= col) if d == 0 else (row <= col)
        strict = (row > col) if d == 0 else (row < col)
        gb = g_ref[b]
        gam_all = _dot(incl.astype(F32), gb, precision=HIGHEST)
        gam_t = gam_all.T
        last_row = c - 1 if d == 0 else 0
        for h in range(GDN_HEADS):
            cc = d * GDN_HEADS + h
            gam = gam_all[:, cc:cc + 1]
            beta = gb[:, 2 * GDN_HEADS + cc:2 * GDN_HEADS + cc + 1]
            kh = k_ref[b, h]
            chains.append(dict(
                b=b, d=d, h=h, o_ref=o_ref, strict=strict, gam=gam, beta=beta, kh=kh,
                qh=q_ref[b, h], vh=v_ref[b, h],
                last=gam_all[last_row:last_row + 1, cc:cc + 1],
                decay=jnp.exp(jnp.where(incl, gam - gam_t[cc:cc + 1, :], -1e30)),
                kbeta=kh * beta))
    for ch in chains:
        ch["a"] = jnp.where(ch["strict"], _dot_nt(ch["kbeta"], ch["kh"]) * ch["decay"], 0.0)
    for ch in chains:
        ch["t"] = eye - jnp.where(blk[2], ch["a"], 0.0)
    bsz = 2
    while bsz < c:
        off = jnp.logical_and(blk[2 * bsz], jnp.logical_not(blk[bsz]))
        for ch in chains:
            ch["tl"] = _dot(ch["t"], jnp.where(off, ch["a"], 0.0))
        for ch in chains:
            ch["t"] = ch["t"] - _dot(ch["tl"], ch["t"])
        bsz *= 2
    for ch in chains:
        rhs = jnp.concatenate([ch["vh"] * ch["beta"], ch["kbeta"] * jnp.exp(ch["gam"])], axis=1)
        ch["uw"] = _dot(ch["t"], rhs)
    for ch in chains:
        ch["pm"] = _dot_nt(ch["qh"], ch["kh"]) * ch["decay"]
    for ch in chains:
        ch["s"] = s_ref[ch["b"], ch["d"], ch["h"]]
        ch["v_new"] = ch["uw"][:, :GDN_DK] - _dot(ch["uw"][:, GDN_DK:], ch["s"])
    for ch in chains:
        h = ch["h"]
        qg = ch["qh"] * jnp.exp(ch["gam"])
        ch["o_ref"][ch["b"], :, h * GDN_DK:(h + 1) * GDN_DK] = _dot(qg, ch["s"]) + _dot(ch["pm"], ch["v_new"])
    for ch in chains:
        kd = ch["kh"] * jnp.exp(ch["last"] - ch["gam"])
        s_ref[ch["b"], ch["d"], ch["h"]] = ch["s"] * jnp.exp(ch["last"]) + _dot_tn(kd, ch["v_new"])


def _gdn_scan(q, k, v, gb, ncc):
    nb, _, s, _ = q.shape
    nc = s // GDN_C

    def bwd(n):
        return jnp.where(n < ncc, ncc - 1 - n, nc + ncc - 1 - n)

    hf = pl.BlockSpec((nb, GDN_HEADS, GDN_C, GDN_DK), lambda n: (0, 0, n, 0))
    hb = pl.BlockSpec((nb, GDN_HEADS, GDN_C, GDN_DK), lambda n: (0, 0, bwd(n), 0))
    gf = pl.BlockSpec((nb, GDN_C, LANES), lambda n: (0, n, 0))
    gk = pl.BlockSpec((nb, GDN_C, LANES), lambda n: (0, bwd(n), 0))
    of = pl.BlockSpec((nb, GDN_C, GDN_WIDTH), lambda n: (0, n, 0))
    ob = pl.BlockSpec((nb, GDN_C, GDN_WIDTH), lambda n: (0, bwd(n), 0))
    oshape = jax.ShapeDtypeStruct((nb, s, GDN_WIDTH), F32)
    return pl.pallas_call(
        _gdn_scan_kernel,
        grid=(nc,),
        in_specs=[hf, hf, hf, gf, hb, hb, hb, gk],
        out_specs=[of, ob],
        out_shape=[oshape, oshape],
        scratch_shapes=[pltpu.VMEM((nb, 2, GDN_HEADS, GDN_DK, GDN_DK), F32)],
        compiler_params=_cparams(("arbitrary",)),
        name="gdn_scan",
    )(q, k, v, gb, q, k, v, gb)


def _attn_prep_kernel(q_ref, kv_ref, cos_ref, sin_ref, qg_ref, kg_ref, bd_ref,
                      qt_ref, kh_ref, vt_ref, *, q_scale):
    cos = cos_ref[...]
    sin = sin_ref[...]
    bd = bd_ref[...]
    lane = lax.broadcasted_iota(jnp.int32, cos.shape, 1)
    first = (lane % 32) < 16

    def norm_rope(x, g):
        ss = _head_sums(x * x, bd)
        y = x * lax.rsqrt(ss * (1.0 / HEAD_DIM) + EPS) * g
        swapped = jnp.where(first, pltpu.roll(y, LANES - 16, 1), pltpu.roll(y, 16, 1))
        return y * cos + swapped * sin

    q = q_ref[0]
    for j in range(ATTN_WIDTH // LANES):
        yt = (norm_rope(q[:, j * LANES:(j + 1) * LANES], qg_ref[...]) * q_scale).T
        qt_ref[0, 2 * j] = yt[:HEAD_DIM].astype(BF16)
        qt_ref[0, 2 * j + 1] = yt[HEAD_DIM:].astype(BF16)
    kv = kv_ref[0]
    y = norm_rope(kv[:, :LANES], kg_ref[...])
    kh_ref[0, 0] = y[:, :HEAD_DIM].astype(BF16)
    kh_ref[0, 1] = y[:, HEAD_DIM:].astype(BF16)
    vt = kv[:, LANES:].T.astype(BF16)
    ones = jnp.ones((VT_ROWS - HEAD_DIM, ATT_TK), BF16)
    for g in range(ATTN_KV_HEADS):
        for cidx in range(TM // ATT_TK):
            vt_ref[0, g, cidx, 0:HEAD_DIM, :] = vt[g * HEAD_DIM:(g + 1) * HEAD_DIM,
                                                   cidx * ATT_TK:(cidx + 1) * ATT_TK]
            vt_ref[0, g, cidx, HEAD_DIM:VT_ROWS, :] = ones


def _attn_prep(q, kv, cos_t, sin_t, qg, kg, bd, q_scale):
    nb, s, _ = q.shape
    nt = s // TM
    tab = pl.BlockSpec((TM, LANES), lambda b, t: (t, 0))
    row = pl.BlockSpec((1, LANES), lambda b, t: (0, 0))
    per = TM // ATT_TK
    return pl.pallas_call(
        functools.partial(_attn_prep_kernel, q_scale=q_scale),
        grid=(nb, nt),
        in_specs=[pl.BlockSpec((1, TM, ATTN_WIDTH), lambda b, t: (b, t, 0)),
                  pl.BlockSpec((1, TM, 2 * LANES), lambda b, t: (b, t, 0)),
                  tab, tab, row, row,
                  pl.BlockSpec((LANES, LANES), lambda b, t: (0, 0))],
        out_specs=[pl.BlockSpec((1, ATTN_Q_HEADS, HEAD_DIM, TM), lambda b, t: (b, 0, 0, t)),
                   pl.BlockSpec((1, ATTN_KV_HEADS, TM, HEAD_DIM), lambda b, t: (b, 0, t, 0)),
                   pl.BlockSpec((1, ATTN_KV_HEADS, per, VT_ROWS, ATT_TK), lambda b, t: (b, 0, t, 0, 0))],
        out_shape=[jax.ShapeDtypeStruct((nb, ATTN_Q_HEADS, HEAD_DIM, s), BF16),
                   jax.ShapeDtypeStruct((nb, ATTN_KV_HEADS, s, HEAD_DIM), BF16),
                   jax.ShapeDtypeStruct((nb, ATTN_KV_HEADS, s // ATT_TK, VT_ROWS, ATT_TK), BF16)],
        compiler_params=_cparams(("parallel", "parallel")),
        name="attn_prep",
    )(q, kv, cos_t, sin_t, qg, kg, bd)


def _attn_kernel(qt_ref, k_ref, vt_ref, o_ref, m_ref, acc_ref, s_ref, mx_ref, *, nctq, lc, s_len):
    qi = pl.program_id(2)
    m_ref[...] = jnp.full(m_ref.shape, -jnp.inf, F32)
    acc_ref[...] = jnp.zeros(acc_ref.shape, F32)
    nk = jnp.where(qi < nctq, lc // ATT_TK, s_len // ATT_TK)

    def score(j, slot):
        ks = pl.multiple_of(j * ATT_TK, ATT_TK)
        k = k_ref[0, 0, pl.ds(ks, ATT_TK), :]
        for h in range(ATTN_GROUP):
            s = _dot(k, qt_ref[0, h])
            s_ref[slot, h] = s
            mx_ref[slot, h] = jnp.max(s, axis=0, keepdims=True)

    def step(j, slot):
        score(jnp.minimum(j + 1, nk - 1), 1 - slot)
        vt = vt_ref[0, 0, j]
        for h in range(ATTN_GROUP):
            s = s_ref[slot, h]
            m_old = m_ref[h]
            m_new = jnp.maximum(m_old, mx_ref[slot, h])
            alpha = jnp.exp2(m_old - m_new)
            p = jnp.exp2(s - m_new)
            acc_ref[h] = alpha * acc_ref[h] + _dot(vt, p.astype(BF16))
            m_ref[h] = m_new

    def quad(i, carry):
        for u in range(4):
            step(1 + 4 * i + u, (1 + u) % 2)
        return carry

    score(0, 0)
    step(0, 0)
    lax.fori_loop(0, (nk - 1) // 4, quad, 0)

    for hp in range(ATTN_GROUP // 2):
        pair = jnp.concatenate(
            [acc_ref[h, 0:HEAD_DIM, :] / acc_ref[h, HEAD_DIM:HEAD_DIM + 1, :] for h in (2 * hp, 2 * hp + 1)], axis=0)
        o_ref[0, :, hp * LANES:(hp + 1) * LANES] = pair.T


def _attention(qt, kh, vt, lc):
    nb, _, _, s = qt.shape
    nq = s // ATT_TQ
    return pl.pallas_call(
        functools.partial(_attn_kernel, nctq=lc // ATT_TQ, lc=lc, s_len=s),
        grid=(nb, ATTN_KV_HEADS, nq),
        in_specs=[pl.BlockSpec((1, ATTN_GROUP, HEAD_DIM, ATT_TQ), lambda b, g, i: (b, g, 0, i)),
                  pl.BlockSpec((1, 1, s, HEAD_DIM), lambda b, g, i: (b, g, 0, 0)),
                  pl.BlockSpec((1, 1, s // ATT_TK, VT_ROWS, ATT_TK), lambda b, g, i: (b, g, 0, 0, 0))],
        out_specs=pl.BlockSpec((1, ATT_TQ, ATTN_GROUP * HEAD_DIM), lambda b, g, i: (b, i, g)),
        out_shape=jax.ShapeDtypeStruct((nb, s, ATTN_WIDTH), F32),
        scratch_shapes=[pltpu.VMEM((ATTN_GROUP, 1, ATT_TQ), F32),
                        pltpu.VMEM((ATTN_GROUP, VT_ROWS, ATT_TQ), F32),
                        pltpu.VMEM((2, ATTN_GROUP, ATT_TK, ATT_TQ), F32),
                        pltpu.VMEM((2, ATTN_GROUP, 1, ATT_TQ), F32)],
        compiler_params=_cparams(("parallel", "parallel", "parallel")),
        name="attention",
    )(qt, kh, vt)


def _outproj_kernel(conv_ref, of_ref, ob_ref, z_ref, attn_ref, x_ref, mod_ref, gg_ref, bd_ref, w_ref,
                    n2_ref, wr_ref, br_ref, xo_ref, h2_ref, ri_ref, rw_ref):
    o = of_ref[0] + ob_ref[0]
    ss = _head_sums(o * o, bd_ref[...])
    gdn = o * lax.rsqrt(ss * (1.0 / GDN_DK) + EPS) * gg_ref[...] * _silu(z_ref[0])
    mix = jnp.concatenate([conv_ref[0].astype(BF16), gdn.astype(BF16), attn_ref[0].astype(BF16)], axis=1)
    y = _dot(mix, w_ref[...])
    x = x_ref[0] + mod_ref[0, 2:3, :] * y
    xo_ref[0] = x
    h2 = _rms(x, n2_ref[...]) * (1.0 + mod_ref[0, 4:5, :]) + mod_ref[0, 3:4, :]
    h2_ref[0] = h2
    h_hi = h2.astype(BF16)
    h_lo = (h2 - h_hi.astype(F32)).astype(BF16)
    t = _dot(h_hi, wr_ref[...])
    lg = t[:, :LANES] + t[:, LANES:] + _dot(h_lo, wr_ref[:, 0:LANES]) + br_ref[...]
    lane = lax.broadcasted_iota(jnp.int32, lg.shape, 1)
    lane_f = lane.astype(F32)

    def first_max(v):
        mx = jnp.max(v, axis=-1, keepdims=True)
        return mx, jnp.min(jnp.where(v == mx, lane_f, float(LANES)), axis=-1, keepdims=True)

    lgg = jnp.where(lane < N_GROUPS, lg, -jnp.inf)
    gmax, grp = first_max(lgg)
    pg_sel = 1.0 / jnp.sum(jnp.exp(lgg - gmax), axis=-1, keepdims=True)
    in_grp = jnp.logical_and(lane >= N_GROUPS, ((lane - N_GROUPS) >> 3).astype(F32) == grp)
    le = jnp.where(in_grp, lg, -jnp.inf)
    e1max, i1 = first_max(le)
    e2max, i2 = first_max(jnp.where(lane_f == i1, -jnp.inf, le))
    r = jnp.exp(e2max - e1max)
    w1 = pg_sel / (1.0 + r)
    ri_ref[0] = jnp.where(lane == 0, i1, jnp.where(lane == 1, i2, float(N_GROUPS))).astype(jnp.int32) - N_GROUPS
    rw_ref[0] = jnp.where(lane == 0, w1, jnp.where(lane == 1, w1 * r, 0.0))


def _out_proj(conv, o_f, o_b, z, attn, x, mod, gdn_g, bd, w_out, n2, wr, br, nct):
    nb, s, d = x.shape
    nt = s // TM
    tok = lambda w: pl.BlockSpec((1, TM, w), lambda b, t: (b, t, 0))
    full = lambda a: pl.BlockSpec(a.shape, lambda b, t: (0,) * a.ndim)
    return pl.pallas_call(
        _outproj_kernel,
        grid=(nb, nt),
        in_specs=[tok(CONV_CH), tok(GDN_WIDTH), tok(GDN_WIDTH), tok(GDN_WIDTH), tok(ATTN_WIDTH), tok(d),
                  pl.BlockSpec((1, 6, d), lambda b, t: (jnp.where(t < nct, nb, b), 0, 0)),
                  full(gdn_g), full(bd), full(w_out), full(n2), full(wr), full(br)],
        out_specs=[tok(d), tok(d), tok(LANES), tok(LANES)],
        out_shape=[jax.ShapeDtypeStruct((nb, s, d), F32), jax.ShapeDtypeStruct((nb, s, d), F32),
                   jax.ShapeDtypeStruct((nb, s, LANES), jnp.int32), jax.ShapeDtypeStruct((nb, s, LANES), F32)],
        compiler_params=_cparams(("parallel", "parallel")),
        name="out_proj",
    )(conv, o_f, o_b, z, attn, x, mod, gdn_g, bd, w_out, n2, wr, br)


def _dispatch_kernel(slot_ref, h_ref, xe_in_ref, xe_ref, sem):
    del xe_in_ref
    base = pl.program_id(0) * (TM * TOP_K)

    def row_copy(r, k):
        dst = slot_ref[base + TOP_K * r + k]
        return pltpu.make_async_copy(h_ref.at[pl.ds(r, 1)], xe_ref.at[pl.ds(dst, 1)], sem)

    def issue(r, carry):
        for k in range(TOP_K):
            row_copy(r, k).start()
        return carry

    lax.fori_loop(0, TM, issue, 0, unroll=8)
    for _ in range(TOP_K):
        pltpu.make_async_copy(h_ref, xe_ref.at[pl.ds(0, TM)], sem).wait()


def _dispatch(h2, slots, n_slots):
    n_tok, d = h2.shape
    grid_spec = pltpu.PrefetchScalarGridSpec(
        num_scalar_prefetch=1,
        grid=(n_tok // TM,),
        in_specs=[pl.BlockSpec((TM, d), lambda i, sl: (i, 0)),
                  pl.BlockSpec(memory_space=pl.ANY)],
        out_specs=pl.BlockSpec(memory_space=pl.ANY),
        scratch_shapes=[pltpu.SemaphoreType.DMA(())],
    )
    return pl.pallas_call(
        _dispatch_kernel,
        grid_spec=grid_spec,
        out_shape=jax.ShapeDtypeStruct((n_slots, d), F32),
        input_output_aliases={2: 0},
        compiler_params=_cparams(("arbitrary",)),
        name="moe_dispatch",
    )(slots, h2, jnp.zeros((n_slots, d), F32))


def _expert_kernel(be_ref, nu_ref, x_ref, wg_ref, wu_ref, wd_ref, o_ref, wg_s, wu_s, wd_s):
    i = pl.program_id(0)
    used = i < nu_ref[0]
    new_expert = jnp.logical_or(i == 0, be_ref[i] != be_ref[jnp.maximum(i - 1, 0)])

    @pl.when(jnp.logical_and(used, new_expert))
    def _():
        wg_s[...] = wg_ref[0, 0].astype(BF16)
        wu_s[...] = wu_ref[0, 0].astype(BF16)
        wd_s[...] = wd_ref[0, 0].astype(BF16)

    @pl.when(used)
    def _():
        x = x_ref[...].astype(BF16)
        g = _dot(x, wg_s[...])
        u = _dot(x, wu_s[...])
        h = (_silu(g) * u).astype(BF16)
        o_ref[...] = _dot(h, wd_s[...])

    @pl.when(i >= nu_ref[0])
    def _():
        o_ref[...] = jnp.zeros(o_ref.shape, F32)


def _experts(xs, blk_e, n_used, w_gate, w_up, w_down, layer):
    n_slots, d = xs.shape
    nblk = n_slots // MOE_BM
    de = w_gate.shape[-1]
    grid_spec = pltpu.PrefetchScalarGridSpec(
        num_scalar_prefetch=2,
        grid=(nblk,),
        in_specs=[pl.BlockSpec((MOE_BM, d), lambda i, be, nu: (jnp.minimum(i, nu[0] - 1), 0)),
                  pl.BlockSpec((1, 1, d, de), lambda i, be, nu: (layer, be[i], 0, 0)),
                  pl.BlockSpec((1, 1, d, de), lambda i, be, nu: (layer, be[i], 0, 0)),
                  pl.BlockSpec((1, 1, de, d), lambda i, be, nu: (layer, be[i], 0, 0))],
        out_specs=pl.BlockSpec((MOE_BM, d), lambda i, be, nu: (i, 0)),
        scratch_shapes=[pltpu.VMEM((d, de), BF16), pltpu.VMEM((d, de), BF16), pltpu.VMEM((de, d), BF16)],
    )
    return pl.pallas_call(
        _expert_kernel,
        grid_spec=grid_spec,
        out_shape=jax.ShapeDtypeStruct((n_slots, d), F32),
        compiler_params=_cparams(("arbitrary",)),
        name="experts",
    )(blk_e, n_used, xs, w_gate, w_up, w_down)


def _combine_kernel(slot_ref, x_ref, rw_ref, mod_ref, fg_ref, ye_ref, o_ref, buf_ref, sem, *, final, nt, ntile):
    g = pl.program_id(0) * nt + pl.program_id(1)

    def row_copy(tile, r, k, par):
        src = slot_ref[tile * (TM * TOP_K) + TOP_K * r + k]
        return pltpu.make_async_copy(ye_ref.at[pl.ds(src, 1)], buf_ref.at[par, k, pl.ds(r, 1)], sem.at[par, k])

    def issue_tile(tile, par):
        def issue(r, carry):
            for k in range(TOP_K):
                row_copy(tile, r, k, par).start()
            return carry
        lax.fori_loop(0, TM, issue, 0, unroll=8)

    def finish(par):
        for k in range(TOP_K):
            pltpu.make_async_copy(ye_ref.at[pl.ds(0, TM)], buf_ref.at[par, k], sem.at[par, k]).wait()
        rw = rw_ref[0]
        y = rw[:, 0:1] * buf_ref[par, 0] + rw[:, 1:2] * buf_ref[par, 1]
        x = x_ref[0] + mod_ref[0, 5:6, :] * y
        if final:
            x = _rms(x, fg_ref[...])
        o_ref[0] = x

    @pl.when(g == 0)
    def _():
        issue_tile(0, 0)

    for par in range(2):
        @pl.when(g % 2 == par)
        def _(par=par):
            @pl.when(g + 1 < ntile)
            def _():
                issue_tile(g + 1, 1 - par)

            finish(par)


def _combine(x, ye, slots, rw, mod, fg, nct, final):
    nb, s, d = x.shape
    nt = s // TM
    tok = lambda w: pl.BlockSpec((1, TM, w), lambda b, t, sl: (b, t, 0))
    grid_spec = pltpu.PrefetchScalarGridSpec(
        num_scalar_prefetch=1,
        grid=(nb, nt),
        in_specs=[tok(d), tok(LANES),
                  pl.BlockSpec((1, 6, d), lambda b, t, sl: (jnp.where(t < nct, nb, b), 0, 0)),
                  pl.BlockSpec((1, d), lambda b, t, sl: (0, 0)),
                  pl.BlockSpec(memory_space=pl.ANY)],
        out_specs=(pl.BlockSpec((1, TM, d), lambda b, t, sl: (b, jnp.maximum(t - nct, 0), 0)) if final
                   else tok(d)),
        scratch_shapes=[pltpu.VMEM((2, TOP_K, TM, d), F32), pltpu.SemaphoreType.DMA((2, TOP_K))],
    )
    return pl.pallas_call(
        functools.partial(_combine_kernel, final=final, nt=nt, ntile=nb * nt),
        grid_spec=grid_spec,
        out_shape=jax.ShapeDtypeStruct((nb, s - nct * TM if final else s, d), F32),
        compiler_params=_cparams(("arbitrary", "arbitrary")),
        name="moe_combine",
    )(slots, x, rw, mod, fg, ye)


def _rank_kernel(ri_ref, rank_ref, cnt_ref, carry_ref):
    @pl.when(pl.program_id(0) == 0)
    def _():
        carry_ref[...] = jnp.zeros(carry_ref.shape, F32)

    ri = ri_ref[0]
    lane = lax.broadcasted_iota(jnp.int32, ri.shape, 1)
    oh1 = lane == ri[:, 0:1]
    oh2 = lane == ri[:, 1:2]
    oh = jnp.logical_or(oh1, oh2).astype(F32)
    row = lax.broadcasted_iota(jnp.int32, (TM, TM), 0)
    col = lax.broadcasted_iota(jnp.int32, (TM, TM), 1)
    before = carry_ref[0:1, :] + _dot((row > col).astype(BF16), oh.astype(BF16))
    r1 = jnp.sum(jnp.where(oh1, before, 0.0), axis=-1, keepdims=True)
    r2 = jnp.sum(jnp.where(oh2, before, 0.0), axis=-1, keepdims=True)
    rank_ref[0] = jnp.where(lane == 0, r1, jnp.where(lane == 1, r2, 0.0)).astype(jnp.int32)
    carry_ref[...] = carry_ref[...] + jnp.sum(oh, axis=0, keepdims=True)
    cnt_ref[...] = carry_ref[...]


def _slot_kernel(ri_ref, rank_ref, ps_ref, slot_ref):
    ri = ri_ref[0]
    rk = rank_ref[0]
    lane = lax.broadcasted_iota(jnp.int32, ri.shape, 1)
    ps = ps_ref[...]
    s1 = jnp.sum(jnp.where(lane == ri[:, 0:1], ps, 0.0), axis=-1, keepdims=True).astype(jnp.int32) + rk[:, 0:1]
    s2 = jnp.sum(jnp.where(lane == ri[:, 1:2], ps, 0.0), axis=-1, keepdims=True).astype(jnp.int32) + rk[:, 1:2]
    slot_ref[0] = jnp.where(lane == 0, s1, jnp.where(lane == 1, s2, 0))


def _route(ri, n_tok):
    ntile = n_tok // TM
    ri = ri.reshape(ntile, TM, LANES)
    tile = pl.BlockSpec((1, TM, LANES), lambda i: (i, 0, 0))
    rank, cnt = pl.pallas_call(
        _rank_kernel,
        grid=(ntile,),
        in_specs=[tile],
        out_specs=[tile, pl.BlockSpec((8, LANES), lambda i: (0, 0))],
        out_shape=[jax.ShapeDtypeStruct((ntile, TM, LANES), jnp.int32), jax.ShapeDtypeStruct((8, LANES), F32)],
        scratch_shapes=[pltpu.VMEM((8, LANES), F32)],
        compiler_params=_cparams(("arbitrary",)),
        name="moe_rank",
    )(ri)
    counts = cnt[0, :N_EXPERTS].astype(jnp.int32)
    pcounts = (counts + MOE_BM - 1) // MOE_BM * MOE_BM
    pends = jnp.cumsum(pcounts)
    pstarts = jnp.zeros((1, LANES), F32).at[0, :N_EXPERTS].set((pends - pcounts).astype(F32))
    slot = pl.pallas_call(
        _slot_kernel,
        grid=(ntile,),
        in_specs=[tile, tile, pl.BlockSpec((1, LANES), lambda i: (0, 0))],
        out_specs=tile,
        out_shape=jax.ShapeDtypeStruct((ntile, TM, LANES), jnp.int32),
        compiler_params=_cparams(("parallel",)),
        name="moe_slot",
    )(ri, rank, pstarts)
    slots = slot.reshape(n_tok, LANES)[:, :TOP_K].reshape(-1)
    n_slots = (-(-n_tok * TOP_K // MOE_BM) + N_EXPERTS) * MOE_BM
    blk_start = jnp.arange(n_slots // MOE_BM, dtype=jnp.int32) * MOE_BM
    blk_e = jnp.minimum(jnp.sum((pends[None, :] <= blk_start[:, None]).astype(jnp.int32), axis=1), N_EXPERTS - 1)
    n_used = (pends[-1] // MOE_BM).astype(jnp.int32).reshape(1)
    return slots, n_slots, blk_e, n_used


def _rope_tables(lc, l):
    rows = l // GRID_W
    row = jnp.repeat(jnp.arange(rows), GRID_W).astype(F32)
    col = (jnp.arange(rows * GRID_W) % GRID_W).astype(F32)
    n_freq = HEAD_DIM // 4
    inv = ROPE_THETA ** (-jnp.arange(n_freq, dtype=F32) / n_freq)
    ar, ac = row[:, None] * inv, col[:, None] * inv
    cos = jnp.concatenate([jnp.cos(ar), jnp.cos(ar), jnp.cos(ac), jnp.cos(ac)], axis=-1)
    sin = jnp.concatenate([-jnp.sin(ar), jnp.sin(ar), -jnp.sin(ac), jnp.sin(ac)], axis=-1)
    cos = jnp.concatenate([jnp.ones((lc, HEAD_DIM), F32), cos], axis=0)
    sin = jnp.concatenate([jnp.zeros((lc, HEAD_DIM), F32), sin], axis=0)
    return jnp.tile(cos, (1, 2)), jnp.tile(sin, (1, 2))


def _block_diag_ones(n, blk):
    i = jnp.arange(n)
    return (i[:, None] // blk == i[None, :] // blk).astype(BF16)


def kernel(x, c, ctx, c_ctx, mod_w, mod_b, norm1_g, norm2_g, w_in, conv_dw_w, conv_dw_b, conv_ln_g, conv_ln_b, conv_pw_w, conv_pw_b, gdn_conv_w, gdn_a_log, gdn_dt_bias, gdn_norm_g, attn_q_norm_g, attn_k_norm_g, w_out, router_group_w, router_group_b, router_expert_w, router_expert_b, expert_w_gate, expert_w_up, expert_w_down, final_norm_g):
    nb, l, d = x.shape
    lc = ctx.shape[1]
    depth = mod_w.shape[0]
    s = lc + l
    assert lc % TM == 0 and l % TM == 0 and lc % GDN_C == 0 and l % GDN_C == 0
    assert lc % ATT_TK == 0 and s % ATT_TK == 0 and lc % ATT_TQ == 0
    assert (lc // ATT_TK - 1) % 4 == 0 and (s // ATT_TK - 1) % 4 == 0
    nct = lc // TM
    n_tok = nb * s

    nr = -(-(nb + 1) // 8) * 8
    cvec = jnp.zeros((nr, d), F32).at[:nb].set(c).at[nb].set(c_ctx)
    mod_all = _modulation(cvec, mod_w, mod_b).reshape(depth, nr, 6, d)

    cos_t, sin_t = _rope_tables(lc, l)
    bd64 = _block_diag_ones(LANES, HEAD_DIM)
    bd_gdn = _block_diag_ones(GDN_WIDTH, GDN_DK)
    q_scale = (HEAD_DIM ** -0.5) * math.log2(math.e)
    row = lambda v: v.reshape(1, -1).astype(F32)
    pad_lanes = lambda v: jnp.zeros((1, LANES), F32).at[0, :v.size].set(v.reshape(-1))

    xs = jnp.concatenate([ctx, x], axis=1)
    for layer in range(depth):
        mod = mod_all[layer]
        wi = w_in[layer]
        w_in_p = jnp.concatenate(
            [wi[:, :IN_A0 + 16], jnp.zeros((d, IN_PAD), F32), wi[:, IN_A0 + 16:]], axis=1).astype(BF16)
        qkv, ab, kv, gvgg, z, q = _in_proj(xs, mod, row(norm1_g[layer]), w_in_p, nct)

        dww = jnp.concatenate([conv_dw_w[layer], jnp.zeros((1, CONV_CH), F32)], axis=0)
        conv = _conformer(gvgg, dww, row(conv_dw_b[layer]), row(conv_ln_g[layer]), row(conv_ln_b[layer]),
                          conv_pw_w[layer].astype(BF16), row(conv_pw_b[layer]), nct)

        cw = jnp.concatenate([gdn_conv_w[layer], jnp.zeros((8 - SHORT_CONV, GDN_QKV), F32)], axis=0)
        gq, gk, gv, gb = _gdn_features(qkv, ab, cw, pad_lanes(gdn_a_log[layer]), pad_lanes(gdn_dt_bias[layer]),
                                       bd_gdn, nct)
        o_f, o_b = _gdn_scan(gq, gk, gv, gb, lc // GDN_C)

        qt, kh, vt = _attn_prep(q, kv, cos_t, sin_t,
                                row(jnp.tile(attn_q_norm_g[layer], 2)), row(jnp.tile(attn_k_norm_g[layer], 2)),
                                bd64, q_scale)
        attn = _attention(qt, kh, vt, lc)

        wr = jnp.zeros((d, LANES), F32).at[:, :N_GROUPS].set(router_group_w[layer])
        wr = wr.at[:, N_GROUPS:N_GROUPS + N_EXPERTS].set(router_expert_w[layer])
        wr_hi = wr.astype(BF16)
        wr = jnp.concatenate([wr_hi, (wr - wr_hi.astype(F32)).astype(BF16)], axis=1)
        br =jnp.zeros((1, LANES), F32).at[0, :N_GROUPS].set(router_group_b[layer])
        br = br.at[0, N_GROUPS:N_GROUPS + N_EXPERTS].set(router_expert_b[layer])
        xs, h2, ri, rw = _out_proj(conv, o_f, o_b, z, attn, xs, mod,
                                   row(jnp.tile(gdn_norm_g[layer], GDN_HEADS)), bd_gdn,
                                   w_out[layer].astype(BF16), row(norm2_g[layer]), wr, br, nct)

        slots, n_slots, blk_e, n_used = _route(ri, n_tok)
        xe = _dispatch(h2.reshape(n_tok, d), slots, n_slots)
        ye = _experts(xe, blk_e, n_used, expert_w_gate, expert_w_up, expert_w_down, layer)
        xs = _combine(xs, ye, slots, rw, mod, row(final_norm_g), nct, layer == depth - 1)
    return xs
```
